```python
import jax, jax.numpy as jnp
from jax import lax
import numpy as np

D_MODEL = 1024
BATCH = 4
SEQ = 8192
DEPTH = 1
DEC_BATCH = 32
DEC_SEQ = 32
PAST_LEN = 1024

CHUNK = 64
Q_BLOCK = 128
RMS_EPS = 1e-6
NEG_INF = -1e30
MLA_HEADS = 8
QK_NOPE = 64
QK_ROPE = 32
QK_DIM = QK_NOPE + QK_ROPE
V_DIM = 64
Q_LORA = 256
KV_LORA = 128
ROPE_BASE = 10000.0
ATTN_SCALE = QK_DIM ** -0.5
RWKV_HEADS = 8
RWKV_HEAD = 64
RWKV_DIM = RWKV_HEADS * RWKV_HEAD
W_LORA = 64
A_LORA = 64
G_LORA = 128
LN_X_EPS = 64e-5
N_BRANCH = 2
BRANCH_DIM = 512
MLA_COLS = Q_LORA + KV_LORA + QK_ROPE
RWKV_COLS = 3 * RWKV_DIM + W_LORA + A_LORA + G_LORA
GATE_COLS = N_BRANCH * D_MODEL
IN_COLS = MLA_COLS + RWKV_COLS + GATE_COLS
N_GROUPS = 4
EXPERTS_PER_GROUP = 8
N_EXPERTS = N_GROUPS * EXPERTS_PER_GROUP
EXPERT_FF = 256
TOP_K = 2
MOE_BLOCK = 128

kernel_name = "hybrid_mla_rwkv7_hiermoe_stream_step"


def rmsnorm(x, g, eps=RMS_EPS):
    xf = x.astype(jnp.float32)
    xf = xf * lax.rsqrt(jnp.mean(xf * xf, axis=-1, keepdims=True) + eps)
    return (xf * g.astype(jnp.float32)).astype(x.dtype)


def rope(x, pos):
    half = QK_ROPE // 2
    inv = 1.0 / (ROPE_BASE ** (jnp.arange(half, dtype=jnp.float32) * (2.0 / QK_ROPE)))
    ang = pos.astype(jnp.float32)[:, None] * inv[None, :]
    ang = ang.reshape((ang.shape[0],) + (1,) * (x.ndim - 3) + (half,))
    cos, sin = jnp.cos(ang), jnp.sin(ang)
    xf = x.astype(jnp.float32)
    x1, x2 = xf[..., :half], xf[..., half:]
    return jnp.concatenate([x1 * cos - x2 * sin, x1 * sin + x2 * cos], axis=-1).astype(x.dtype)


def mla_queries(c_q, pos, q_norm_g, w_uq, q_head_norm_g):
    cq = rmsnorm(c_q, q_norm_g)
    q = jnp.einsum('blc,chd->blhd', cq, w_uq)
    q = jnp.concatenate([q[..., :QK_NOPE], rope(q[..., QK_NOPE:], pos)], axis=-1)
    return rmsnorm(q, q_head_norm_g)


def mla_keys_values(lat, kpe, w_ukv, k_head_norm_g):
    B, Lk, _ = lat.shape
    kv = jnp.einsum('blc,chd->blhd', lat, w_ukv)
    k_nope, v = kv[..., :QK_NOPE], kv[..., QK_NOPE:]
    k_rot = jnp.broadcast_to(kpe[:, :, None, :], (B, Lk, MLA_HEADS, QK_ROPE)).astype(k_nope.dtype)
    k = rmsnorm(jnp.concatenate([k_nope, k_rot], axis=-1), k_head_norm_g)
    return k, v


def chunk_causal_attention(q, k, v, q_pos, k_pos):
    s = jnp.einsum('bqhd,bkhd->bhqk', q.astype(jnp.float32), k.astype(jnp.float32)) * ATTN_SCALE
    visible = (k_pos[None, :] // CHUNK) <= (q_pos[:, None] // CHUNK)
    s = jnp.where(visible[None, None], s, NEG_INF)
    p = jax.nn.softmax(s, axis=-1)
    return jnp.einsum('bhqk,bkhd->bqhd', p, v.astype(jnp.float32)).astype(v.dtype)


def wkv7_scan(r, w, k, v, a, b, s0):
    def step(s, inp):
        r_t, w_t, k_t, v_t, a_t, b_t = inp
        sa = jnp.einsum('bhvk,bhk->bhv', s, a_t)
        s = s * w_t[:, :, None, :] + sa[..., None] * b_t[:, :, None, :] + v_t[..., None] * k_t[:, :, None, :]
        return s, jnp.einsum('bhvk,bhk->bhv', s, r_t)
    xs = tuple(jnp.moveaxis(t, 1, 0) for t in (r, w, k, v, a, b))
    s_final, ys = lax.scan(step, s0, xs)
    return jnp.moveaxis(ys, 0, 1), s_final


def rwkv7_time_mix(z, shift0, wkv0, mu, w0, w2, a0, a2, g2, k_k, k_a, r_k, lnx_g, lnx_b):
    B, L, _ = z.shape
    f32 = jnp.float32
    zf = z.astype(f32)
    prev = jnp.concatenate([shift0.astype(f32), zf[:, :-1]], axis=1)
    zm = zf + (prev - zf) * mu.astype(f32)
    r = zm[..., :RWKV_DIM]
    k = zm[..., RWKV_DIM:2 * RWKV_DIM]
    v = zm[..., 2 * RWKV_DIM:3 * RWKV_DIM]
    xw = zm[..., 3 * RWKV_DIM:3 * RWKV_DIM + W_LORA]
    xa = zm[..., 3 * RWKV_DIM + W_LORA:3 * RWKV_DIM + W_LORA + A_LORA]
    xg = zm[..., 3 * RWKV_DIM + W_LORA + A_LORA:]
    wlog = -jax.nn.softplus(-(w0.astype(f32) + jnp.tanh(xw) @ w2.astype(f32))) - 0.5
    decay = jnp.exp(-jnp.exp(wlog))
    a = jax.nn.sigmoid(a0.astype(f32) + xa @ a2.astype(f32))
    g = jax.nn.sigmoid(xg) @ g2.astype(f32)
    heads = lambda t: t.reshape(B, L, RWKV_HEADS, RWKV_HEAD)
    kk = heads(k * k_k.astype(f32))
    kk = kk * lax.rsqrt(jnp.maximum(jnp.sum(kk * kk, axis=-1, keepdims=True), 1e-24))
    k = k * (1.0 + (a - 1.0) * k_a.astype(f32))
    rh, kh, vh, ah = heads(r), heads(k), heads(v), heads(a)
    y, s_new = wkv7_scan(rh, heads(decay), kh, vh, -kk, kk * ah, wkv0.astype(f32))
    mean = jnp.mean(y, axis=-1, keepdims=True)
    var = jnp.mean(jnp.square(y - mean), axis=-1, keepdims=True)
    y = (y - mean) * lax.rsqrt(var + LN_X_EPS)
    y = y.reshape(B, L, RWKV_DIM) * lnx_g.astype(f32) + lnx_b.astype(f32)
    bonus = jnp.sum(rh * kh * r_k.astype(f32), axis=-1, keepdims=True) * vh
    out = (y + bonus.reshape(B, L, RWKV_DIM)) * g
    return out.astype(z.dtype), s_new.astype(wkv0.dtype), z[:, -1:]


def routed_expert_ffn(hf, expert_id, weight, w_gate, w_up, w_down):
    T = hf.shape[0]
    A = T * TOP_K
    flat_e = expert_id.reshape(A)
    flat_t = jnp.repeat(jnp.arange(T, dtype=jnp.int32), TOP_K)
    flat_w = weight.reshape(A).astype(hf.dtype)
    order = jnp.argsort(flat_e)
    se = flat_e[order]
    counts = jnp.zeros((N_EXPERTS,), jnp.int32).at[flat_e].add(1)
    pcounts = (counts + MOE_BLOCK - 1) // MOE_BLOCK * MOE_BLOCK
    pend = jnp.cumsum(pcounts)
    pstart = pend - pcounts
    start = jnp.cumsum(counts) - counts
    dest = pstart[se] + (jnp.arange(A, dtype=jnp.int32) - start[se])
    n_blocks = -(-A // MOE_BLOCK) + N_EXPERTS
    P = n_blocks * MOE_BLOCK
    row_tok = jnp.full((P,), T, jnp.int32).at[dest].set(flat_t[order])
    row_w = jnp.zeros((P,), hf.dtype).at[dest].set(flat_w[order])
    block_pos = jnp.arange(n_blocks, dtype=jnp.int32) * MOE_BLOCK
    block_e = jnp.minimum(jnp.searchsorted(pend, block_pos, side='right'), N_EXPERTS - 1)
    h_pad = jnp.concatenate([hf, jnp.zeros((1, hf.shape[1]), hf.dtype)], axis=0)
    xb = h_pad[row_tok].reshape(n_blocks, MOE_BLOCK, hf.shape[1])

    def run_block(args):
        xblk, e = args
        return (jax.nn.silu(xblk @ w_gate[e]) * (xblk @ w_up[e])) @ w_down[e]

    yb = lax.map(run_block, (xb, block_e)).reshape(P, hf.shape[1])
    y = jax.ops.segment_sum(yb * row_w[:, None], row_tok, num_segments=T + 1)
    return y[:T]


def hier_moe(h, w_rg, b_rg, w_re, b_re, w_eg, w_eu, w_ed):
    B, L, D = h.shape
    T = B * L
    hf = h.reshape(T, D)
    hf32 = hf.astype(jnp.float32)
    g_prob = jax.nn.softmax(hf32 @ w_rg.astype(jnp.float32) + b_rg.astype(jnp.float32), axis=-1)
    g_top, g_idx = lax.top_k(g_prob, 1)
    e_logits = jnp.einsum('td,dge->tge', hf32, w_re.astype(jnp.float32)) + b_re.astype(jnp.float32)
    e_in_group = e_logits[jnp.arange(T), g_idx[:, 0]]
    e_top, e_idx = lax.top_k(e_in_group, TOP_K)
    e_w = jax.nn.softmax(e_top, axis=-1) * g_top
    expert_id = g_idx * EXPERTS_PER_GROUP + e_idx
    return routed_expert_ffn(hf, expert_id, e_w, w_eg, w_eu, w_ed).reshape(B, L, D)


def layer_forward(x, pos, lat_past, kpe_past, wkv0, shift0,
                  norm_mix_g, w_in, q_norm_g, w_uq, kv_norm_g, w_ukv, q_head_norm_g, k_head_norm_g,
                  rwkv_mu, rwkv_w0, rwkv_w2, rwkv_a0, rwkv_a2, rwkv_g2, rwkv_k_k, rwkv_k_a, rwkv_r_k,
                  rwkv_lnx_g, rwkv_lnx_b, w_out, norm_ffn_g, w_router_group, b_router_group,
                  w_router_expert, b_router_expert, w_expert_gate, w_expert_up, w_expert_down):
    B, L, _ = x.shape
    h = rmsnorm(x, norm_mix_g)
    p = h @ w_in
    c_q = p[..., :Q_LORA]
    c_kv = p[..., Q_LORA:Q_LORA + KV_LORA]
    k_pe = p[..., Q_LORA + KV_LORA:MLA_COLS]
    z = p[..., MLA_COLS:MLA_COLS + RWKV_COLS]
    gate_logits = p[..., MLA_COLS + RWKV_COLS:]
    lat = rmsnorm(c_kv, kv_norm_g)
    kpe = rope(k_pe, pos)
    q = mla_queries(c_q, pos, q_norm_g, w_uq, q_head_norm_g)
    if lat_past is None:
        lat_all, kpe_all, k_pos = lat, kpe, pos
    else:
        lat_all = jnp.concatenate([lat_past.astype(lat.dtype), lat], axis=1)
        kpe_all = jnp.concatenate([kpe_past.astype(kpe.dtype), kpe], axis=1)
        k_pos = jnp.arange(lat_all.shape[1], dtype=jnp.int32)
    k, v = mla_keys_values(lat_all, kpe_all, w_ukv, k_head_norm_g)
    n_past = lat_all.shape[1] - L
    qb = min(Q_BLOCK, L)
    outs = []
    for s0 in range(0, L, qb):
        e0 = min(s0 + qb, L)
        kend = n_past + e0
        outs.append(chunk_causal_attention(q[:, s0:e0], k[:, :kend], v[:, :kend], pos[s0:e0], k_pos[:kend]))
    o_a = jnp.concatenate(outs, axis=1).reshape(B, L, BRANCH_DIM)
    o_b, wkv_new, shift_new = rwkv7_time_mix(z, shift0, wkv0, rwkv_mu, rwkv_w0, rwkv_w2, rwkv_a0, rwkv_a2,
                                             rwkv_g2, rwkv_k_k, rwkv_k_a, rwkv_r_k, rwkv_lnx_g, rwkv_lnx_b)
    branches = jnp.stack([o_a, o_b.astype(o_a.dtype)], axis=2)
    proj = jnp.einsum('blnc,ncd->blnd', branches, w_out)
    gates = jax.nn.sigmoid(gate_logits.astype(jnp.float32)).reshape(B, L, N_BRANCH, D_MODEL)
    x = x + jnp.sum(gates * proj.astype(jnp.float32), axis=2).astype(x.dtype)
    h2 = rmsnorm(x, norm_ffn_g)
    x = x + hier_moe(h2, w_router_group, b_router_group, w_router_expert, b_router_expert,
                     w_expert_gate, w_expert_up, w_expert_down)
    return x, lat, kpe, wkv_new, shift_new


def setup_inputs(seed: int = 0) -> dict:
    key = jax.random.key(seed)
    ks = iter(jax.random.split(key, 40))
    nrm = lambda shape, scale: jax.random.normal(next(ks), shape, jnp.float32) * scale
    gain = lambda shape: 1.0 + nrm(shape, 0.02)
    D = D_MODEL
    return {
        "x_prompt": nrm((BATCH, SEQ, D), 1.0),
        "x_sample": nrm((DEC_BATCH, DEC_SEQ, D), 1.0),
        "cache_kv_latent": nrm((DEPTH, DEC_BATCH, PAST_LEN, KV_LORA), 1.0),
        "cache_k_rope": nrm((DEPTH, DEC_BATCH, PAST_LEN, QK_ROPE), 1.0),
        "state_wkv": nrm((DEPTH, DEC_BATCH, RWKV_HEADS, RWKV_HEAD, RWKV_HEAD), 0.5),
        "state_shift": nrm((DEPTH, DEC_BATCH, 1, RWKV_COLS), 1.0),
        "norm_mix_g": gain((DEPTH, D)),
        "w_in": nrm((DEPTH, D, IN_COLS), D ** -0.5),
        "q_norm_g": gain((DEPTH, Q_LORA)),
        "w_uq": nrm((DEPTH, Q_LORA, MLA_HEADS, QK_DIM), Q_LORA ** -0.5),
        "kv_norm_g": gain((DEPTH, KV_LORA)),
        "w_ukv": nrm((DEPTH, KV_LORA, MLA_HEADS, QK_NOPE + V_DIM), KV_LORA ** -0.5),
        "q_head_norm_g": gain((DEPTH, QK_DIM)),
        "k_head_norm_g": gain((DEPTH, QK_DIM)),
        "rwkv_mu": jax.random.uniform(next(ks), (DEPTH, RWKV_COLS), jnp.float32),
        "rwkv_w0": jax.random.uniform(next(ks), (DEPTH, RWKV_DIM), jnp.float32, -5.0, 1.0),
        "rwkv_w2": nrm((DEPTH, W_LORA, RWKV_DIM), 0.5 * W_LORA ** -0.5),
        "rwkv_a0": nrm((DEPTH, RWKV_DIM), 0.1),
        "rwkv_a2": nrm((DEPTH, A_LORA, RWKV_DIM), A_LORA ** -0.5),
        "rwkv_g2": nrm((DEPTH, G_LORA, RWKV_DIM), G_LORA ** -0.5),
        "rwkv_k_k": 0.85 + nrm((DEPTH, RWKV_DIM), 0.05),
        "rwkv_k_a": 1.0 + nrm((DEPTH, RWKV_DIM), 0.05),
        "rwkv_r_k": nrm((DEPTH, RWKV_HEADS, RWKV_HEAD), 0.1),
        "rwkv_lnx_g": gain((DEPTH, RWKV_DIM)),
        "rwkv_lnx_b": nrm((DEPTH, RWKV_DIM), 0.02),
        "w_out": nrm((DEPTH, N_BRANCH, BRANCH_DIM, D), BRANCH_DIM ** -0.5),
        "norm_ffn_g": gain((DEPTH, D)),
        "w_router_group": nrm((DEPTH, D, N_GROUPS), D ** -0.5),
        "b_router_group": nrm((DEPTH, N_GROUPS), 0.01),
        "w_router_expert": nrm((DEPTH, D, N_GROUPS, EXPERTS_PER_GROUP), D ** -0.5),
        "b_router_expert": nrm((DEPTH, N_GROUPS, EXPERTS_PER_GROUP), 0.01),
        "w_expert_gate": nrm((DEPTH, N_EXPERTS, D, EXPERT_FF), D ** -0.5),
        "w_expert_up": nrm((DEPTH, N_EXPERTS, D, EXPERT_FF), D ** -0.5),
        "w_expert_down": nrm((DEPTH, N_EXPERTS, EXPERT_FF, D), EXPERT_FF ** -0.5),
    }


def reference(x_prompt, x_sample, cache_kv_latent, cache_k_rope, state_wkv, state_shift,
              norm_mix_g, w_in, q_norm_g, w_uq, kv_norm_g, w_ukv, q_head_norm_g, k_head_norm_g,
              rwkv_mu, rwkv_w0, rwkv_w2, rwkv_a0, rwkv_a2, rwkv_g2, rwkv_k_k, rwkv_k_a, rwkv_r_k,
              rwkv_lnx_g, rwkv_lnx_b, w_out, norm_ffn_g, w_router_group, b_router_group,
              w_router_expert, b_router_expert, w_expert_gate, w_expert_up, w_expert_down):
    weights = (norm_mix_g, w_in, q_norm_g, w_uq, kv_norm_g, w_ukv, q_head_norm_g, k_head_norm_g,
               rwkv_mu, rwkv_w0, rwkv_w2, rwkv_a0, rwkv_a2, rwkv_g2, rwkv_k_k, rwkv_k_a, rwkv_r_k,
               rwkv_lnx_g, rwkv_lnx_b, w_out, norm_ffn_g, w_router_group, b_router_group,
               w_router_expert, b_router_expert, w_expert_gate, w_expert_up, w_expert_down)
    bp = x_prompt.shape[0]
    pos_p = jnp.arange(x_prompt.shape[1], dtype=jnp.int32)
    pos_s = cache_kv_latent.shape[2] + jnp.arange(x_sample.shape[1], dtype=jnp.int32)
    wkv_zero = jnp.zeros((bp, RWKV_HEADS, RWKV_HEAD, RWKV_HEAD), x_prompt.dtype)
    shift_zero = jnp.zeros((bp, 1, RWKV_COLS), x_prompt.dtype)
    xp, xs = x_prompt, x_sample
    lat_p, kpe_p, wkv_p, sh_p = [], [], [], []
    lat_s, kpe_s, wkv_s, sh_s = [], [], [], []
    for l in range(DEPTH):
        lw = tuple(wt[l] for wt in weights)
        xp, a1, a2, a3, a4 = layer_forward(xp, pos_p, None, None, wkv_zero, shift_zero, *lw)
        xs, b1, b2, b3, b4 = layer_forward(xs, pos_s, cache_kv_latent[l], cache_k_rope[l],
                                           state_wkv[l], state_shift[l], *lw)
        lat_p.append(a1); kpe_p.append(a2); wkv_p.append(a3); sh_p.append(a4)
        lat_s.append(b1); kpe_s.append(b2); wkv_s.append(b3); sh_s.append(b4)
    return (xp, xs,
            jnp.stack(lat_p, 0), jnp.stack(kpe_p, 0), jnp.stack(wkv_p, 0), jnp.stack(sh_p, 0),
            jnp.stack(lat_s, 0), jnp.stack(kpe_s, 0), jnp.stack(wkv_s, 0), jnp.stack(sh_s, 0))
```

```python
import functools
import math

import jax
import jax.numpy as jnp
from jax import lax
from jax.experimental import pallas as pl
from jax.experimental.pallas import tpu as pltpu

F32 = jnp.float32
BF16 = jnp.bfloat16
I32 = jnp.int32

D_MODEL = 1024
CHUNK = 64
RMS_EPS = 1e-6
MASK_VALUE = -1e30
MLA_HEADS = 8
QK_NOPE = 64
QK_ROPE = 32
QK_DIM = QK_NOPE + QK_ROPE
V_DIM = 64
Q_LORA = 256
KV_LORA = 128
ROPE_BASE = 10000.0
ATTN_SCALE = QK_DIM ** -0.5
RWKV_HEADS = 8
RWKV_HEAD = 64
RWKV_DIM = RWKV_HEADS * RWKV_HEAD
W_LORA = 64
A_LORA = 64
G_LORA = 128
LN_X_EPS = 64e-5
MLA_COLS = Q_LORA + KV_LORA + QK_ROPE
RWKV_COLS = 3 * RWKV_DIM + W_LORA + A_LORA + G_LORA
GATE_COLS = 2 * D_MODEL
N_GROUPS = 4
EXPERTS_PER_GROUP = 8
N_EXPERTS = N_GROUPS * EXPERTS_PER_GROUP
EXPERT_FF = 256
TOP_K = 2

LANES = 128
HEAD_PAD = LANES
HALF_ROPE = QK_ROPE // 2
SEG_Q = 0
SEG_KV = SEG_Q + Q_LORA
SEG_PE = SEG_KV + KV_LORA
SEG_Z = SEG_PE + LANES
SEG_G = SEG_Z + RWKV_COLS
IN_COLS_PAD = SEG_G + GATE_COLS
ROUTE_GROUP_LANE = N_EXPERTS
SCAN_CHUNK = 64
FFN_ROWS = 256
VMEM_LIMIT = 48 * 1024 * 1024


def _cparams(sem, vmem=VMEM_LIMIT):
    return pltpu.CompilerParams(dimension_semantics=sem, vmem_limit_bytes=vmem)


def _dot(a, b):
    return jnp.dot(a, b, preferred_element_type=F32)


def _dot_nt(a, b):
    return lax.dot_general(a, b, (((1,), (1,)), ((), ())), preferred_element_type=F32)


def _dot_tn(a, b):
    return lax.dot_general(a, b, (((0,), (0,)), ((), ())), preferred_element_type=F32)


def _split(x):
    hi = x.astype(BF16)
    lo = (x - hi.astype(F32)).astype(BF16)
    return hi, lo


def _dot3(a, b, dot=_dot):
    ah, al = _split(a)
    bh, bl = _split(b)
    return dot(ah, bh) + (dot(al, bh) + dot(ah, bl))


def _dot2_exact_rhs(a, b_bf16):
    ah, al = _split(a)
    return _dot(ah, b_bf16) + _dot(al, b_bf16)


def _sigmoid(x):
    return 1.0 / (1.0 + jnp.exp(-x))


def _rms(x, n=None):
    n = x.shape[-1] if n is None else n
    return lax.rsqrt(jnp.sum(x * x, axis=-1, keepdims=True) * (1.0 / n) + RMS_EPS)


def _inproj_kernel(x_ref, g_ref, w_ref, qg_ref, wqa_ref, wqb_ref, ct_ref, st_ref, kt_ref, kvg_ref, qhg_ref,
                   q_ref, lat_ref, kpe_ref, z_ref, gl_ref):
    x = x_ref[...]
    h = (x * _rms(x) * g_ref[...]).astype(BF16)
    cq = _dot(h, w_ref[:, SEG_Q:SEG_KV])
    ckv = _dot(h, w_ref[:, SEG_KV:SEG_PE])
    pe = _dot(h, w_ref[:, SEG_PE:SEG_Z])
    z_ref[...] = _dot(h, w_ref[:, SEG_Z:SEG_G])
    gl_ref[...] = _dot(h, w_ref[:, SEG_G:IN_COLS_PAD])
    lat_ref[...] = ckv * _rms(ckv) * kvg_ref[...]
    x1 = pe[:, :HALF_ROPE]
    x2 = pe[:, HALF_ROPE:QK_ROPE]
    c = kt_ref[:, :HALF_ROPE]
    s = kt_ref[:, HALF_ROPE:QK_ROPE]
    kpe_ref[...] = jnp.concatenate([x1 * c - x2 * s, x1 * s + x2 * c], axis=-1)
    cqn = (cq * _rms(cq) * qg_ref[...]).astype(BF16)
    ct = ct_ref[...]
    st = st_ref[...]
    qhg = qhg_ref[...]
    for hd in range(MLA_HEADS):
        sl = slice(hd * HEAD_PAD, (hd + 1) * HEAD_PAD)
        xh = _dot(cqn, wqa_ref[:, sl]) * ct + _dot(cqn, wqb_ref[:, sl]) * st
        q_ref[:, sl] = (xh * _rms(xh, QK_DIM) * qhg).astype(BF16)


def _inproj(x, tabs, wts, tm):
    T = x.shape[0]
    ct, st, kt = tabs
    nt = ct.shape[0] // tm
    row = lambda i: (i, 0)
    tab = lambda i: (i % nt, 0)
    fixed = lambda i: (0, 0)
    full = lambda a: pl.BlockSpec(a.shape, fixed)
    return pl.pallas_call(
        _inproj_kernel,
        grid=(T // tm,),
        in_specs=[pl.BlockSpec((tm, D_MODEL), row), full(wts["mix_g"]), full(wts["w_in"]), full(wts["q_g"]),
                  full(wts["wqa"]), full(wts["wqb"]),
                  pl.BlockSpec((tm, HEAD_PAD), tab), pl.BlockSpec((tm, HEAD_PAD), tab),
                  pl.BlockSpec((tm, QK_ROPE), tab), full(wts["kv_g"]), full(wts["qh_g"])],
        out_specs=[pl.BlockSpec((tm, MLA_HEADS * HEAD_PAD), row), pl.BlockSpec((tm, KV_LORA), row),
                   pl.BlockSpec((tm, QK_ROPE), row), pl.BlockSpec((tm, RWKV_COLS), row),
                   pl.BlockSpec((tm, GATE_COLS), row)],
        out_shape=[jax.ShapeDtypeStruct((T, MLA_HEADS * HEAD_PAD), BF16), jax.ShapeDtypeStruct((T, KV_LORA), F32),
                   jax.ShapeDtypeStruct((T, QK_ROPE), F32), jax.ShapeDtypeStruct((T, RWKV_COLS), F32),
                   jax.ShapeDtypeStruct((T, GATE_COLS), F32)],
        compiler_params=_cparams(("arbitrary",)),
        name="inproj",
    )(x, wts["mix_g"], wts["w_in"], wts["q_g"], wts["wqa"], wts["wqb"], ct, st, kt, wts["kv_g"], wts["qh_g"])


def _kvprep_kernel(lat_ref, kpe_ref, wk_ref, wv_ref, place_ref, kg_ref, k_ref, v_ref):
    lat = lat_ref[...].astype(BF16)
    rot = _dot2_exact_rhs(kpe_ref[...], place_ref[...])
    kg = kg_ref[...]
    for hd in range(MLA_HEADS):
        sl = slice(hd * HEAD_PAD, (hd + 1) * HEAD_PAD)
        kh = _dot(lat, wk_ref[:, sl]) + rot
        k_ref[:, sl] = (kh * _rms(kh, QK_DIM) * kg).astype(BF16)
    v_ref[...] = _dot(lat, wv_ref[...]).astype(BF16)


def _kvprep(lat, kpe, wts, tm):
    T = lat.shape[0]
    row = lambda i: (i, 0)
    fixed = lambda i: (0, 0)
    full = lambda a: pl.BlockSpec(a.shape, fixed)
    return pl.pallas_call(
        _kvprep_kernel,
        grid=(T // tm,),
        in_specs=[pl.BlockSpec((tm, KV_LORA), row), pl.BlockSpec((tm, QK_ROPE), row), full(wts["wk"]),
                  full(wts["wv"]), full(wts["place"]), full(wts["kh_g"])],
        out_specs=[pl.BlockSpec((tm, MLA_HEADS * HEAD_PAD), row), pl.BlockSpec((tm, MLA_HEADS * V_DIM), row)],
        out_shape=[jax.ShapeDtypeStruct((T, MLA_HEADS * HEAD_PAD), BF16),
                   jax.ShapeDtypeStruct((T, MLA_HEADS * V_DIM), BF16)],
        compiler_params=_cparams(("arbitrary",)),
        name="kvprep",
    )(lat, kpe, wts["wk"], wts["wv"], wts["place"], wts["kh_g"])


def _attn_kernel(q_ref, k_ref, v_ref, o_ref, *, tq, tk, n_past, lk_valid, diag_tiles):
    i = pl.program_id(2)
    n_full = i * (tq // tk) if diag_tiles else 0
    lane = lax.broadcasted_iota(I32, (tq, 2 * V_DIM), 1)
    outs = []
    for hh in range(2):
        hs = slice(hh * HEAD_PAD, (hh + 1) * HEAD_PAD)
        q = q_ref[:, hs]

        def step(j, carry, masked):
            m, l, acc = carry
            ks = pl.multiple_of(j * tk, tk)
            k = k_ref[pl.ds(ks, tk), hs]
            v = v_ref[pl.ds(ks, tk), :]
            s = _dot_nt(q, k)
            if masked:
                qpos = n_past + i * tq + lax.broadcasted_iota(I32, (tq, tk), 0)
                kpos = j * tk + lax.broadcasted_iota(I32, (tq, tk), 1)
                vis = jnp.logical_and(jnp.right_shift(kpos, 6) <= jnp.right_shift(qpos, 6), kpos < lk_valid)
                s = jnp.where(vis, s, MASK_VALUE)
            m_new = jnp.maximum(m, jnp.max(s, axis=-1, keepdims=True))
            alpha = jnp.exp2(m - m_new)
            p = jnp.exp2(s - m_new)
            l = alpha * l + jnp.sum(p, axis=-1, keepdims=True)
            acc = alpha * acc + _dot(p.astype(BF16), v)
            return m_new, l, acc

        carry = (jnp.full((tq, 1), MASK_VALUE, F32), jnp.zeros((tq, 1), F32), jnp.zeros((tq, 2 * V_DIM), F32))
        if diag_tiles:
            carry = lax.fori_loop(0, n_full, functools.partial(step, masked=False), carry)
        for d in range(tq // tk if diag_tiles else 1):
            carry = step(n_full + d, carry, True)
        m, l, acc = carry
        outs.append(acc / l)
    o_ref[...] = jnp.where(lane < V_DIM, outs[0], outs[1]).astype(o_ref.dtype)


def _attention(q, k, v, *, tq, tk, n_past, lk_valid, diag_tiles):
    B, L, _ = q.shape
    Lk = k.shape[1]
    kern = functools.partial(_attn_kernel, tq=tq, tk=tk, n_past=n_past, lk_valid=lk_valid, diag_tiles=diag_tiles)
    return pl.pallas_call(
        kern,
        grid=(B, MLA_HEADS // 2, L // tq),
        in_specs=[pl.BlockSpec((None, tq, 2 * HEAD_PAD), lambda b, p, i: (b, i, p)),
                  pl.BlockSpec((None, Lk, 2 * HEAD_PAD), lambda b, p, i: (b, 0, p)),
                  pl.BlockSpec((None, Lk, 2 * V_DIM), lambda b, p, i: (b, 0, p))],
        out_specs=pl.BlockSpec((None, tq, 2 * V_DIM), lambda b, p, i: (b, i, p)),
        out_shape=jax.ShapeDtypeStruct((B, L, MLA_HEADS * V_DIM), BF16),
        compiler_params=_cparams(("arbitrary", "arbitrary", "arbitrary")),
        name="attn",
    )(q, k, v)


def _rwkv_kernel(z_ref, shift_ref, s0_ref, mu_ref, w0_ref, w2_ref, a0_ref, a2_ref, g2_ref, kk_ref, ka_ref, rk_ref,
                 lng_ref, lnb_ref, ones_ref, o_ref, sout_ref, s_scr, prev_scr, *, C):
    c = pl.program_id(1)

    @pl.when(c == 0)
    def _():
        s_scr[...] = s0_ref[...]
        prev_scr[...] = shift_ref[...]

    zf = z_ref[...]
    row = lax.broadcasted_iota(I32, (C, 1), 0)
    pv = jnp.where(row == 0, prev_scr[...], pltpu.roll(zf, 1, axis=0))
    prev_scr[...] = zf[C - 1:C, :]
    zm = zf + (pv - zf) * mu_ref[...]
    H = RWKV_DIM
    r = zm[:, 0:H]
    k = zm[:, H:2 * H]
    v = zm[:, 2 * H:3 * H]
    xwa = zm[:, 3 * H:3 * H + W_LORA + A_LORA]
    xg = zm[:, 3 * H + W_LORA + A_LORA:]
    lw = (-math.exp(-0.5)) * _sigmoid(w0_ref[...] + _dot(jnp.tanh(xwa).astype(BF16), w2_ref[...]))
    a = _sigmoid(a0_ref[...] + _dot(xwa.astype(BF16), a2_ref[...]))
    g = _dot(_sigmoid(xg).astype(BF16), g2_ref[...])
    ones_bd = ones_ref[...]
    kk = k * kk_ref[...]
    kkn = kk * lax.rsqrt(jnp.maximum(_dot2_exact_rhs(kk * kk, ones_bd), 1e-24))
    k2 = k * (1.0 + (a - 1.0) * ka_ref[...])
    bonus = _dot2_exact_rhs(r * k2 * rk_ref[...], ones_bd) * v

    ti = lax.broadcasted_iota(I32, (C, C), 0)
    tj = lax.broadcasted_iota(I32, (C, C), 1)
    incl = ti >= tj
    strict = ti > tj
    tri = jnp.where(incl, 1.0, 0.0).astype(BF16)
    p1 = lw.astype(BF16)
    r1 = lw - p1.astype(F32)
    p2 = r1.astype(BF16)
    p3 = (r1 - p2.astype(F32)).astype(BF16)
    cs = _dot(tri, p1) + (_dot(tri, p2) + _dot(tri, p3))
    w_incl = jnp.exp(cs)
    w_inv = jnp.exp(-cs)
    at = -kkn * jnp.exp(cs - lw)
    rt = r * w_incl
    bt = kkn * a * w_inv
    kt = k2 * w_inv
    wc = w_incl[C - 1:C, :]
    bh = bt * wc
    kh = kt * wc
    eye = jnp.where(ti == tj, 1.0, 0.0).astype(F32)

    ys = []
    for hd in range(RWKV_HEADS):
        sl = slice(hd * RWKV_HEAD, (hd + 1) * RWKV_HEAD)
        at_h, rt_h, bt_h, kt_h, v_h = at[:, sl], rt[:, sl], bt[:, sl], kt[:, sl], v[:, sl]
        a_ab = jnp.where(strict, _dot3(at_h, bt_h, _dot_nt), 0.0)
        a_ak = jnp.where(strict, _dot3(at_h, kt_h, _dot_nt), 0.0)
        a_rb = jnp.where(incl, _dot3(rt_h, bt_h, _dot_nt), 0.0)
        a_rk = jnp.where(incl, _dot3(rt_h, kt_h, _dot_nt), 0.0)
        pw = a_ab
        tinv = eye + pw
        n = 2
        while n < C:
            pw = _dot3(pw, pw)
            tinv = tinv + _dot3(tinv, pw)
            n *= 2
        s_h = s_scr[hd]
        u = _dot3(tinv, _dot3(at_h, s_h, _dot_nt) + _dot3(a_ak, v_h))
        ys.append(_dot3(rt_h, s_h, _dot_nt) + _dot3(a_rb, u) + _dot3(a_rk, v_h))
        s_scr[hd] = s_h * wc[:, sl] + _dot3(u, bh[:, sl], _dot_tn) + _dot3(v_h, kh[:, sl], _dot_tn)

    y = jnp.concatenate(ys, axis=-1)
    inv_n = 1.0 / RWKV_HEAD
    mean = _dot2_exact_rhs(y, ones_bd) * inv_n
    d = y - mean
    var = _dot2_exact_rhs(d * d, ones_bd) * inv_n
    yn = d * lax.rsqrt(var + LN_X_EPS) * lng_ref[...] + lnb_ref[...]
    o_ref[...] = ((yn + bonus) * g).astype(o_ref.dtype)

    @pl.when(c == pl.num_programs(1) - 1)
    def _():
        sout_ref[...] = s_scr[...]


def _rwkv(z, shift0, s0, wts, C):
    B, L, _ = z.shape
    fixed = lambda b, c: (0, 0)
    full = lambda a: pl.BlockSpec(a.shape, fixed)
    names = ["mu", "w0", "w2", "a0", "a2", "g2", "k_k", "k_a", "r_k", "lnx_g", "lnx_b", "ones_bd"]
    return pl.pallas_call(
        functools.partial(_rwkv_kernel, C=C),
        grid=(B, L // C),
        in_specs=[pl.BlockSpec((None, C, RWKV_COLS), lambda b, c: (b, c, 0)),
                  pl.BlockSpec((None, 1, RWKV_COLS), lambda b, c: (b, 0, 0)),
                  pl.BlockSpec((None, RWKV_HEADS, RWKV_HEAD, RWKV_HEAD), lambda b, c: (b, 0, 0, 0))]
                 + [full(wts[n]) for n in names],
        out_specs=[pl.BlockSpec((None, C, RWKV_DIM), lambda b, c: (b, c, 0)),
                   pl.BlockSpec((None, RWKV_HEADS, RWKV_HEAD, RWKV_HEAD), lambda b, c: (b, 0, 0, 0))],
        out_shape=[jax.ShapeDtypeStruct((B, L, RWKV_DIM), BF16),
                   jax.ShapeDtypeStruct((B, RWKV_HEADS, RWKV_HEAD, RWKV_HEAD), F32)],
        scratch_shapes=[pltpu.VMEM((RWKV_HEADS, RWKV_HEAD, RWKV_HEAD), F32), pltpu.VMEM((1, RWKV_COLS), F32)],
        compiler_params=_cparams(("arbitrary", "arbitrary")),
        name="rwkv",
    )(z, shift0, s0, *[wts[n] for n in names])


def _merge_kernel(x_ref, oa_ref, ob_ref, gl_ref, wo_ref, fg_ref, wrh_ref, wrl_ref, br_ref,
                  x1_ref, h2_ref, ri_ref, rw_ref, cnt_ref, run_scr, *, tm):
    @pl.when(pl.program_id(0) == 0)
    def _():
        run_scr[...] = jnp.zeros_like(run_scr)

    pa = _dot(oa_ref[...], wo_ref[0])
    pb = _dot(ob_ref[...], wo_ref[1])
    x1 = x_ref[...] + (_sigmoid(gl_ref[:, :D_MODEL]) * pa + _sigmoid(gl_ref[:, D_MODEL:]) * pb)
    x1_ref[...] = x1
    h2 = x1 * _rms(x1) * fg_ref[...]
    h2_ref[...] = h2
    hh, hl = _split(h2)
    lg = _dot(hh, wrh_ref[...]) + (_dot(hl, wrh_ref[...]) + _dot(hh, wrl_ref[...])) + br_ref[...]

    lane = lax.broadcasted_iota(I32, (tm, LANES), 1)
    lane_f = lane.astype(F32)
    first = lambda hit: jnp.min(jnp.where(hit, lane_f, float(LANES)), axis=-1, keepdims=True).astype(I32)
    is_group = jnp.logical_and(lane >= ROUTE_GROUP_LANE, lane < ROUTE_GROUP_LANE + N_GROUPS)
    gl = jnp.where(is_group, lg, MASK_VALUE)
    gmax = jnp.max(gl, axis=-1, keepdims=True)
    gidx = first(gl == gmax) - ROUTE_GROUP_LANE
    g_top = 1.0 / jnp.sum(jnp.where(is_group, jnp.exp(gl - gmax), 0.0), axis=-1, keepdims=True)
    in_group = jnp.logical_and(lane < N_EXPERTS, jnp.right_shift(lane, 3) == gidx)
    el = jnp.where(in_group, lg, MASK_VALUE)
    e1 = jnp.max(el, axis=-1, keepdims=True)
    i1 = first(el == e1)
    el2 = jnp.where(lane == i1, MASK_VALUE, el)
    e2 = jnp.max(el2, axis=-1, keepdims=True)
    i2 = first(el2 == e2)
    t = jnp.exp(e2 - e1)
    w1 = g_top / (1.0 + t)
    w2 = g_top * t / (1.0 + t)

    hit1 = lane == i1
    hit2 = lane == i2
    oh = jnp.where(jnp.logical_or(hit1, hit2), 1.0, 0.0)
    ti = lax.broadcasted_iota(I32, (tm, tm), 0)
    tj = lax.broadcasted_iota(I32, (tm, tm), 1)
    before = _dot(jnp.where(ti > tj, 1.0, 0.0).astype(BF16), oh.astype(BF16)) + run_scr[...]
    rank1 = jnp.sum(jnp.where(hit1, before, 0.0), axis=-1, keepdims=True)
    rank2 = jnp.sum(jnp.where(hit2, before, 0.0), axis=-1, keepdims=True)
    run = run_scr[...] + jnp.sum(oh, axis=0, keepdims=True)
    run_scr[...] = run
    cnt_ref[...] = run.astype(I32)

    sel = lambda vals: functools.reduce(lambda acc, kv: jnp.where(lane == kv[0], kv[1], acc), enumerate(vals), 0)
    ri_ref[...] = sel([i1, i2, rank1.astype(I32), rank2.astype(I32)])
    rw_ref[...] = sel([w1, w2])


def _merge(x, oa, ob, gl, wts, tm):
    T = x.shape[0]
    row = lambda i: (i, 0)
    full = lambda a: pl.BlockSpec(a.shape, lambda i: (0,) * a.ndim)
    names = ["w_out", "ffn_g", "wr_hi", "wr_lo", "b_r"]
    return pl.pallas_call(
        functools.partial(_merge_kernel, tm=tm),
        grid=(T // tm,),
        in_specs=[pl.BlockSpec((tm, D_MODEL), row), pl.BlockSpec((tm, RWKV_DIM), row),
                  pl.BlockSpec((tm, RWKV_DIM), row), pl.BlockSpec((tm, GATE_COLS), row)]
                 + [full(wts[n]) for n in names],
        out_specs=[pl.BlockSpec((tm, D_MODEL), row), pl.BlockSpec((tm, D_MODEL), row),
                   pl.BlockSpec((tm, LANES), row), pl.BlockSpec((tm, LANES), row),
                   pl.BlockSpec((1, LANES), lambda i: (0, 0))],
        out_shape=[jax.ShapeDtypeStruct((T, D_MODEL), F32), jax.ShapeDtypeStruct((T, D_MODEL), F32),
                   jax.ShapeDtypeStruct((T, LANES), I32), jax.ShapeDtypeStruct((T, LANES), F32),
                   jax.ShapeDtypeStruct((1, LANES), I32)],
        scratch_shapes=[pltpu.VMEM((1, LANES), F32)],
        compiler_params=_cparams(("arbitrary",)),
        name="merge",
    )(x, oa, ob, gl, *[wts[n] for n in names])


def _dispatch_kernel(dest_ref, h_ref, xs_in_ref, xs_ref, sem, *, tm):
    del xs_in_ref

    def issue(t, carry):
        for s in range(TOP_K):
            d = dest_ref[0, 0, TOP_K * t + s]
            pltpu.make_async_copy(h_ref.at[pl.ds(t, 1), :], xs_ref.at[pl.ds(d, 1), :], sem).start()
        return carry

    lax.fori_loop(0, tm, issue, 0)
    for s in range(TOP_K):
        pltpu.make_async_copy(h_ref, xs_ref.at[pl.ds(0, tm), :], sem).wait()


def _dispatch(h2, dest, n_rows, tm):
    T = h2.shape[0]
    xs0 = jnp.zeros((n_rows, D_MODEL), F32)
    dest3 = dest.reshape(T // tm, 1, TOP_K * tm)
    return pl.pallas_call(
        functools.partial(_dispatch_kernel, tm=tm),
        grid=(T // tm,),
        in_specs=[pl.BlockSpec((1, 1, TOP_K * tm), lambda i: (i, 0, 0), memory_space=pltpu.SMEM),
                  pl.BlockSpec((tm, D_MODEL), lambda i: (i, 0)),
                  pl.BlockSpec(memory_space=pl.ANY)],
        out_specs=pl.BlockSpec(memory_space=pl.ANY),
        out_shape=jax.ShapeDtypeStruct((n_rows, D_MODEL), F32),
        scratch_shapes=[pltpu.SemaphoreType.DMA(())],
        input_output_aliases={2: 0},
        compiler_params=_cparams(("arbitrary",)),
        name="dispatch",
    )(dest3, h2, xs0)


def _ffn_kernel(be_ref, nv_ref, x_ref, wg_ref, wu_ref, wd_ref, y_ref):
    del be_ref
    live = pl.program_id(0) < nv_ref[0]

    @pl.when(live)
    def _():
        x = x_ref[...].astype(BF16)
        gate = _dot(x, wg_ref[...])
        up = _dot(x, wu_ref[...])
        y_ref[...] = _dot((gate * _sigmoid(gate) * up).astype(BF16), wd_ref[...])

    @pl.when(jnp.logical_not(live))
    def _():
        y_ref[...] = jnp.zeros_like(y_ref)


def _ffn(xs, block_e, n_live, wts):
    P = xs.shape[0]
    nb = P // FFN_ROWS
    grid_spec = pltpu.PrefetchScalarGridSpec(
        num_scalar_prefetch=2,
        grid=(nb,),
        in_specs=[pl.BlockSpec((FFN_ROWS, D_MODEL), lambda i, be, nv: (i, 0)),
                  pl.BlockSpec((None, D_MODEL, EXPERT_FF), lambda i, be, nv: (be[i], 0, 0)),
                  pl.BlockSpec((None, D_MODEL, EXPERT_FF), lambda i, be, nv: (be[i], 0, 0)),
                  pl.BlockSpec((None, EXPERT_FF, D_MODEL), lambda i, be, nv: (be[i], 0, 0))],
        out_specs=pl.BlockSpec((FFN_ROWS, D_MODEL), lambda i, be, nv: (i, 0)),
    )
    return pl.pallas_call(
        _ffn_kernel,
        grid_spec=grid_spec,
        out_shape=jax.ShapeDtypeStruct((P, D_MODEL), F32),
        compiler_params=_cparams(("arbitrary",)),
        name="ffn",
    )(block_e, n_live, xs, wts["w_eg"], wts["w_eu"], wts["w_ed"])


def _combine_kernel(dest_ref, x1_ref, rw_ref, yb_ref, o_ref, y_scr, sem, *, tm):
    def issue(t, carry):
        for s in range(TOP_K):
            d = dest_ref[0, 0, TOP_K * t + s]
            pltpu.make_async_copy(yb_ref.at[pl.ds(d, 1), :], y_scr.at[s, pl.ds(t, 1), :], sem).start()
        return carry

    lax.fori_loop(0, tm, issue, 0)
    for s in range(TOP_K):
        pltpu.make_async_copy(yb_ref.at[pl.ds(0, tm), :], y_scr.at[s], sem).wait()
    rw = rw_ref[...]
    o_ref[...] = x1_ref[...] + (rw[:, 0:1] * y_scr[0] + rw[:, 1:2] * y_scr[1])


def _combine(x1, rw, yb, dest, tm):
    T = x1.shape[0]
    dest3 = dest.reshape(T // tm, 1, TOP_K * tm)
    return pl.pallas_call(
        functools.partial(_combine_kernel, tm=tm),
        grid=(T // tm,),
        in_specs=[pl.BlockSpec((1, 1, TOP_K * tm), lambda i: (i, 0, 0), memory_space=pltpu.SMEM),
                  pl.BlockSpec((tm, D_MODEL), lambda i: (i, 0)),
                  pl.BlockSpec((tm, LANES), lambda i: (i, 0)),
                  pl.BlockSpec(memory_space=pl.ANY)],
        out_specs=pl.BlockSpec((tm, D_MODEL), lambda i: (i, 0)),
        out_shape=jax.ShapeDtypeStruct((T, D_MODEL), F32),
        scratch_shapes=[pltpu.VMEM((TOP_K, tm, D_MODEL), F32), pltpu.SemaphoreType.DMA(())],
        compiler_params=_cparams(("arbitrary",)),
        name="combine",
    )(dest3, x1, rw, yb)


def _moe(x1, h2, ri, rw, cnt, wts, tm):
    T = x1.shape[0]
    ids = ri[:, 0:TOP_K]
    ranks = ri[:, TOP_K:2 * TOP_K]
    counts = cnt[0, :N_EXPERTS]
    padded = (counts + FFN_ROWS - 1) // FFN_ROWS * FFN_ROWS
    pend = jnp.cumsum(padded)
    pstart = pend - padded
    dest = (pstart[ids] + ranks).astype(I32)
    nb = -(-T * TOP_K // FFN_ROWS) + N_EXPERTS
    block_pos = jnp.arange(nb, dtype=I32) * FFN_ROWS
    block_e = jnp.minimum(jnp.searchsorted(pend, block_pos, side="right"), N_EXPERTS - 1).astype(I32)
    n_live = (pend[-1:] // FFN_ROWS).astype(I32)
    xs = _dispatch(h2, dest, nb * FFN_ROWS, tm)
    yb = _ffn(xs, block_e, n_live, wts)
    return _combine(x1, rw, yb, dest, tm)


def _rope_tables(pos):
    inv = 1.0 / (ROPE_BASE ** (jnp.arange(HALF_ROPE, dtype=F32) * (2.0 / QK_ROPE)))
    ang = pos.astype(F32)[:, None] * inv[None, :]
    cos, sin = jnp.cos(ang), jnp.sin(ang)
    n = pos.shape[0]
    pad = jnp.zeros((n, HEAD_PAD - QK_DIM), F32)
    ct = jnp.concatenate([jnp.ones((n, QK_NOPE), F32), cos, cos, pad], axis=1)
    st = jnp.concatenate([jnp.zeros((n, QK_NOPE), F32), -sin, sin, pad], axis=1)
    kt = jnp.concatenate([cos, sin], axis=1)
    return ct, st, kt


def _prep_weights(norm_mix_g, w_in, q_norm_g, w_uq, kv_norm_g, w_ukv, q_head_norm_g, k_head_norm_g,
                  rwkv_mu, rwkv_w0, rwkv_w2, rwkv_a0, rwkv_a2, rwkv_g2, rwkv_k_k, rwkv_k_a, rwkv_r_k,
                  rwkv_lnx_g, rwkv_lnx_b, w_out, norm_ffn_g, w_router_group, b_router_group,
                  w_router_expert, b_router_expert, w_expert_gate, w_expert_up, w_expert_down):
    row = lambda a: a.reshape(1, -1).astype(F32)
    head_pad = lambda a: jnp.pad(a, ((0, 0), (0, 0), (0, HEAD_PAD - a.shape[-1])))
    w = {}
    w["mix_g"] = row(norm_mix_g)
    w["w_in"] = jnp.concatenate([w_in[:, :MLA_COLS], jnp.zeros((D_MODEL, LANES - QK_ROPE), F32),
                                 w_in[:, MLA_COLS:]], axis=1).astype(BF16)
    w["q_g"] = row(q_norm_g)
    w["kv_g"] = row(kv_norm_g)
    w["wqa"] = head_pad(w_uq).reshape(Q_LORA, MLA_HEADS * HEAD_PAD).astype(BF16)
    partner = jnp.concatenate([jnp.zeros_like(w_uq[..., :QK_NOPE]), w_uq[..., QK_NOPE + HALF_ROPE:],
                               w_uq[..., QK_NOPE:QK_NOPE + HALF_ROPE]], axis=-1)
    w["wqb"] = head_pad(partner).reshape(Q_LORA, MLA_HEADS * HEAD_PAD).astype(BF16)
    pad_g = lambda g: jnp.pad(g, (0, HEAD_PAD - QK_DIM)).reshape(1, HEAD_PAD).astype(F32)
    w["qh_g"] = pad_g(q_head_norm_g) * (ATTN_SCALE * math.log2(math.e))
    w["kh_g"] = pad_g(k_head_norm_g)
    w["wk"] = head_pad(w_ukv[..., :QK_NOPE]).reshape(KV_LORA, MLA_HEADS * HEAD_PAD).astype(BF16)
    w["wv"] = w_ukv[..., QK_NOPE:].reshape(KV_LORA, MLA_HEADS * V_DIM).astype(BF16)
    w["place"] = jnp.pad(jnp.eye(QK_ROPE, dtype=F32), ((0, 0), (QK_NOPE, HEAD_PAD - QK_DIM))).astype(BF16)
    w["mu"] = row(rwkv_mu)
    w["w0"] = row(rwkv_w0)
    w["w2"] = jnp.concatenate([rwkv_w2, jnp.zeros((A_LORA, RWKV_DIM), F32)], axis=0).astype(BF16)
    w["a0"] = row(rwkv_a0)
    w["a2"] = jnp.concatenate([jnp.zeros((W_LORA, RWKV_DIM), F32), rwkv_a2], axis=0).astype(BF16)
    w["g2"] = rwkv_g2.astype(BF16)
    w["k_k"] = row(rwkv_k_k)
    w["k_a"] = row(rwkv_k_a)
    w["r_k"] = row(rwkv_r_k)
    w["lnx_g"] = row(rwkv_lnx_g)
    w["lnx_b"] = row(rwkv_lnx_b)
    head_of = jnp.arange(RWKV_DIM) // RWKV_HEAD
    w["ones_bd"] = (head_of[:, None] == head_of[None, :]).astype(BF16)
    w["w_out"] = w_out.astype(BF16)
    w["ffn_g"] = row(norm_ffn_g)
    wr = jnp.concatenate([w_router_expert.reshape(D_MODEL, N_EXPERTS), w_router_group,
                          jnp.zeros((D_MODEL, LANES - N_EXPERTS - N_GROUPS), F32)], axis=1)
    w["wr_hi"] = wr.astype(BF16)
    w["wr_lo"] = (wr - w["wr_hi"].astype(F32)).astype(BF16)
    w["b_r"] = jnp.concatenate([b_router_expert.reshape(-1), b_router_group,
                                jnp.zeros((LANES - N_EXPERTS - N_GROUPS,), F32)]).reshape(1, LANES)
    w["w_eg"] = w_expert_gate.astype(BF16)
    w["w_eu"] = w_expert_up.astype(BF16)
    w["w_ed"] = w_expert_down.astype(BF16)
    return w


def _round_up(n, m):
    return -(-n // m) * m


def _layer(x, pos, lat_past, kpe_past, wkv0, shift0, wts):
    B, L, _ = x.shape
    T = B * L
    tm = min(256, T)
    ct, st, kt = _rope_tables(pos)
    if L % tm:
        reps = tm // L
        ct, st, kt = (jnp.tile(t, (reps, 1)) for t in (ct, st, kt))
    q, lat, kpe, z, gl = _inproj(x.reshape(T, D_MODEL), (ct, st, kt), wts, tm)
    lat3 = lat.reshape(B, L, KV_LORA)
    kpe3 = kpe.reshape(B, L, QK_ROPE)
    if lat_past is None:
        n_past, lk, lat_all, kpe_all = 0, L, lat3, kpe3
        tq = tk = min(512, L)
        diag = True
    else:
        n_past = lat_past.shape[1]
        lk_valid = n_past + L
        lk = _round_up(lk_valid, LANES)
        padk = lambda a: jnp.pad(a, ((0, 0), (0, lk - lk_valid), (0, 0)))
        lat_all = padk(jnp.concatenate([lat_past, lat3], axis=1))
        kpe_all = padk(jnp.concatenate([kpe_past, kpe3], axis=1))
        tq, tk, diag = L, lk, False
    k, v = _kvprep(lat_all.reshape(B * lk, KV_LORA), kpe_all.reshape(B * lk, QK_ROPE), wts, min(256, B * lk))
    oa = _attention(q.reshape(B, L, -1), k.reshape(B, lk, -1), v.reshape(B, lk, -1), tq=tq, tk=tk, n_past=n_past,
                    lk_valid=n_past + L, diag_tiles=diag)
    z3 = z.reshape(B, L, RWKV_COLS)
    ob, s_new = _rwkv(z3, shift0, wkv0, wts, min(SCAN_CHUNK, L))
    x1, h2, ri, rw, cnt = _merge(x.reshape(T, D_MODEL), oa.reshape(T, -1), ob.reshape(T, -1), gl, wts, tm)
    y = _moe(x1, h2, ri, rw, cnt, wts, tm)
    return y.reshape(B, L, D_MODEL), lat3, kpe3, s_new, z3[:, L - 1:, :]


def kernel(x_prompt, x_sample, cache_kv_latent, cache_k_rope, state_wkv, state_shift, norm_mix_g, w_in, q_norm_g, w_uq, kv_norm_g, w_ukv, q_head_norm_g, k_head_norm_g, rwkv_mu, rwkv_w0, rwkv_w2, rwkv_a0, rwkv_a2, rwkv_g2, rwkv_k_k, rwkv_k_a, rwkv_r_k, rwkv_lnx_g, rwkv_lnx_b, w_out, norm_ffn_g, w_router_group, b_router_group, w_router_expert, b_router_expert, w_expert_gate, w_expert_up, w_expert_down):
    weights = (norm_mix_g, w_in, q_norm_g, w_uq, kv_norm_g, w_ukv, q_head_norm_g, k_head_norm_g,
               rwkv_mu, rwkv_w0, rwkv_w2, rwkv_a0, rwkv_a2, rwkv_g2, rwkv_k_k, rwkv_k_a, rwkv_r_k,
               rwkv_lnx_g, rwkv_lnx_b, w_out, norm_ffn_g, w_router_group, b_router_group,
               w_router_expert, b_router_expert, w_expert_gate, w_expert_up, w_expert_down)
    depth = w_in.shape[0]
    bp, lp, _ = x_prompt.shape
    n_past = cache_kv_latent.shape[2]
    pos_p = jnp.arange(lp, dtype=I32)
    pos_s = n_past + jnp.arange(x_sample.shape[1], dtype=I32)
    wkv_zero = jnp.zeros((bp, RWKV_HEADS, RWKV_HEAD, RWKV_HEAD), F32)
    shift_zero = jnp.zeros((bp, 1, RWKV_COLS), F32)
    xp, xs = x_prompt, x_sample
    outs_p, outs_s = [], []
    for l in range(depth):
        wts = _prep_weights(*(wt[l] for wt in weights))
        xp, *rest_p = _layer(xp, pos_p, None, None, wkv_zero, shift_zero, wts)
        xs, *rest_s = _layer(xs, pos_s, cache_kv_latent[l], cache_k_rope[l], state_wkv[l], state_shift[l], wts)
        outs_p.append(rest_p)
        outs_s.append(rest_s)
    stack = lambda outs, i: jnp.stack([o[i] for o in outs], 0)
    return (xp, xs, stack(outs_p, 0), stack(outs_p, 1), stack(outs_p, 2), stack(outs_p, 3),
            stack(outs_s, 0), stack(outs_s, 1), stack(outs_s, 2), stack(outs_s, 3))
```

```python
import functools
import math

import jax
import jax.numpy as jnp
from jax import lax
from jax.experimental import pallas as pl
from jax.experimental.pallas import tpu as pltpu

F32 = jnp.float32
BF16 = jnp.bfloat16
I32 = jnp.int32

D_MODEL = 1024
CHUNK = 64
RMS_EPS = 1e-6
MASK_VALUE = -1e30
MLA_HEADS = 8
QK_NOPE = 64
QK_ROPE = 32
QK_DIM = QK_NOPE + QK_ROPE
V_DIM = 64
Q_LORA = 256
KV_LORA = 128
ROPE_BASE = 10000.0
ATTN_SCALE = QK_DIM ** -0.5
RWKV_HEADS = 8
RWKV_HEAD = 64
RWKV_DIM = RWKV_HEADS * RWKV_HEAD
W_LORA = 64
A_LORA = 64
G_LORA = 128
LN_X_EPS = 64e-5
MLA_COLS = Q_LORA + KV_LORA + QK_ROPE
RWKV_COLS = 3 * RWKV_DIM + W_LORA + A_LORA + G_LORA
GATE_COLS = 2 * D_MODEL
N_GROUPS = 4
EXPERTS_PER_GROUP = 8
N_EXPERTS = N_GROUPS * EXPERTS_PER_GROUP
EXPERT_FF = 256
TOP_K = 2

LANES = 128
HEAD_PAD = LANES
HALF_ROPE = QK_ROPE // 2
SEG_Q = 0
SEG_KV = SEG_Q + Q_LORA
SEG_PE = SEG_KV + KV_LORA
SEG_Z = SEG_PE + LANES
SEG_G = SEG_Z + RWKV_COLS
IN_COLS_PAD = SEG_G + GATE_COLS
ROUTE_GROUP_LANE = N_EXPERTS
SCAN_CHUNK = 128
FFN_ROWS = 256
MAX_DIRECT_EXP2_SHIFT = 40.0
VMEM_LIMIT = 48 * 1024 * 1024


def _cparams(sem, vmem=VMEM_LIMIT):
    return pltpu.CompilerParams(dimension_semantics=sem, vmem_limit_bytes=vmem)


def _dot(a, b):
    return jnp.dot(a, b, preferred_element_type=F32)


def _dot_nt(a, b):
    return lax.dot_general(a, b, (((1,), (1,)), ((), ())), preferred_element_type=F32)


def _dot_tn(a, b):
    return lax.dot_general(a, b, (((0,), (0,)), ((), ())), preferred_element_type=F32)


def _split(x):
    hi = x.astype(BF16)
    lo = (x - hi.astype(F32)).astype(BF16)
    return hi, lo


def _dot2_exact_rhs(a, b_bf16):
    ah, al = _split(a)
    return _dot(ah, b_bf16) + _dot(al, b_bf16)


def _sigmoid(x):
    return 1.0 / (1.0 + jnp.exp(-x))


def _rms(x, n=None):
    n = x.shape[-1] if n is None else n
    return lax.rsqrt(jnp.sum(x * x, axis=-1, keepdims=True) * (1.0 / n) + RMS_EPS)


def _inproj_kernel(x_ref, g_ref, w_ref, qg_ref, wqa_ref, wqb_ref, ct_ref, st_ref, kt_ref, kvg_ref, qhg_ref, qone_ref,
                   q_ref, lat_ref, kpe_ref, z_ref, gl_ref):
    x = x_ref[...]
    h = (x * _rms(x) * g_ref[...]).astype(BF16)
    cq = _dot(h, w_ref[:, SEG_Q:SEG_KV])
    ckv = _dot(h, w_ref[:, SEG_KV:SEG_PE])
    pe = _dot(h, w_ref[:, SEG_PE:SEG_Z])
    z_ref[...] = _dot(h, w_ref[:, SEG_Z:SEG_G])
    gl_ref[...] = _dot(h, w_ref[:, SEG_G:IN_COLS_PAD])
    lat_ref[...] = ckv * _rms(ckv) * kvg_ref[...]
    x1 = pe[:, :HALF_ROPE]
    x2 = pe[:, HALF_ROPE:QK_ROPE]
    c = kt_ref[:, :HALF_ROPE]
    s = kt_ref[:, HALF_ROPE:QK_ROPE]
    kpe_ref[...] = jnp.concatenate([x1 * c - x2 * s, x1 * s + x2 * c], axis=-1)
    cqn = (cq * _rms(cq) * qg_ref[...]).astype(BF16)
    ct = ct_ref[...]
    st = st_ref[...]
    qhg = qhg_ref[...]
    qone = qone_ref[...]
    for hd in range(MLA_HEADS):
        sl = slice(hd * HEAD_PAD, (hd + 1) * HEAD_PAD)
        xh = _dot(cqn, wqa_ref[:, sl]) * ct + _dot(cqn, wqb_ref[:, sl]) * st
        q_ref[:, sl] = (xh * _rms(xh, QK_DIM) * qhg + qone).astype(BF16)


def _inproj(x, tabs, wts, tm):
    T = x.shape[0]
    ct, st, kt = tabs
    nt = ct.shape[0] // tm
    row = lambda i: (i, 0)
    tab = lambda i: (i % nt, 0)
    fixed = lambda i: (0, 0)
    full = lambda a: pl.BlockSpec(a.shape, fixed)
    return pl.pallas_call(
        _inproj_kernel,
        grid=(T // tm,),
        in_specs=[pl.BlockSpec((tm, D_MODEL), row), full(wts["mix_g"]), full(wts["w_in"]), full(wts["q_g"]),
                  full(wts["wqa"]), full(wts["wqb"]),
                  pl.BlockSpec((tm, HEAD_PAD), tab), pl.BlockSpec((tm, HEAD_PAD), tab),
                  pl.BlockSpec((tm, QK_ROPE), tab), full(wts["kv_g"]), full(wts["qh_g"]), full(wts["q_one"])],
        out_specs=[pl.BlockSpec((tm, MLA_HEADS * HEAD_PAD), row), pl.BlockSpec((tm, KV_LORA), row),
                   pl.BlockSpec((tm, QK_ROPE), row), pl.BlockSpec((tm, RWKV_COLS), row),
                   pl.BlockSpec((tm, GATE_COLS), row)],
        out_shape=[jax.ShapeDtypeStruct((T, MLA_HEADS * HEAD_PAD), BF16), jax.ShapeDtypeStruct((T, KV_LORA), F32),
                   jax.ShapeDtypeStruct((T, QK_ROPE), F32), jax.ShapeDtypeStruct((T, RWKV_COLS), F32),
                   jax.ShapeDtypeStruct((T, GATE_COLS), F32)],
        compiler_params=_cparams(("arbitrary",)),
        name="inproj",
    )(x, wts["mix_g"], wts["w_in"], wts["q_g"], wts["wqa"], wts["wqb"], ct, st, kt, wts["kv_g"], wts["qh_g"],
      wts["q_one"])


def _kvprep_kernel(lat_ref, kpe_ref, wk_ref, wv_ref, place_ref, kg_ref, kshift_ref, vone_ref, k_ref, v_ref):
    lat = lat_ref[...].astype(BF16)
    rot = _dot2_exact_rhs(kpe_ref[...], place_ref[...])
    kg = kg_ref[...]
    kshift = kshift_ref[...]
    for hd in range(MLA_HEADS):
        sl = slice(hd * HEAD_PAD, (hd + 1) * HEAD_PAD)
        kh = _dot(lat, wk_ref[:, sl]) + rot
        k_ref[:, sl] = (kh * _rms(kh, QK_DIM) * kg + kshift).astype(BF16)
    v_ref[...] = (_dot(lat, wv_ref[...]) + vone_ref[...]).astype(BF16)


def _kvprep(lat, kpe, wts, tm):
    T = lat.shape[0]
    row = lambda i: (i, 0)
    fixed = lambda i: (0, 0)
    full = lambda a: pl.BlockSpec(a.shape, fixed)
    return pl.pallas_call(
        _kvprep_kernel,
        grid=(T // tm,),
        in_specs=[pl.BlockSpec((tm, KV_LORA), row), pl.BlockSpec((tm, QK_ROPE), row), full(wts["wk"]),
                  full(wts["wv"]), full(wts["place"]), full(wts["kh_g"]), full(wts["k_shift"]),
                  full(wts["v_one"])],
        out_specs=[pl.BlockSpec((tm, MLA_HEADS * HEAD_PAD), row), pl.BlockSpec((tm, MLA_HEADS * HEAD_PAD), row)],
        out_shape=[jax.ShapeDtypeStruct((T, MLA_HEADS * HEAD_PAD), BF16),
                   jax.ShapeDtypeStruct((T, MLA_HEADS * HEAD_PAD), BF16)],
        compiler_params=_cparams(("arbitrary",)),
        name="kvprep",
    )(lat, kpe, wts["wk"], wts["wv"], wts["place"], wts["kh_g"], wts["k_shift"], wts["v_one"])


def _attn_kernel(q_ref, k_ref, v_ref, o_ref, *, tq, tk, n_past, lk_valid, diag_tiles, bounded):
    i = pl.program_id(2)
    n_full = i * (tq // tk) if diag_tiles else 0
    head_lanes = [slice(hh * HEAD_PAD, (hh + 1) * HEAD_PAD) for hh in range(2)]
    qs = [q_ref[:, hs] for hs in head_lanes]

    def scores(hh, j, masked):
        ks = pl.multiple_of(j * tk, tk)
        s = _dot_nt(qs[hh], k_ref[pl.ds(ks, tk), head_lanes[hh]])
        if masked:
            qpos = n_past + i * tq + lax.broadcasted_iota(I32, (tq, tk), 0)
            kpos = j * tk + lax.broadcasted_iota(I32, (tq, tk), 1)
            vis = jnp.logical_and(jnp.right_shift(kpos, 6) <= jnp.right_shift(qpos, 6), kpos < lk_valid)
            s = jnp.where(vis, s, MASK_VALUE)
        return s, v_ref[pl.ds(ks, tk), head_lanes[hh]]

    if bounded:
        def head_step(hh, j, acc, masked):
            s, v = scores(hh, j, masked)
            return acc + _dot(jnp.exp2(s).astype(BF16), v)

        init = jnp.zeros((tq, HEAD_PAD), F32)
    else:
        def head_step(hh, j, carry, masked):
            m, acc = carry
            s, v = scores(hh, j, masked)
            m_new = jnp.maximum(m, jnp.max(s, axis=-1, keepdims=True))
            return m_new, jnp.exp2(m - m_new) * acc + _dot(jnp.exp2(s - m_new).astype(BF16), v)

        init = (jnp.full((tq, 1), MASK_VALUE, F32), jnp.zeros((tq, HEAD_PAD), F32))

    step = lambda j, carry, masked: tuple(head_step(hh, j, carry[hh], masked) for hh in range(2))
    carry = (init, init)
    if diag_tiles:
        carry = lax.fori_loop(0, n_full, functools.partial(step, masked=False), carry)
    for d in range(tq // tk if diag_tiles else 1):
        carry = step(n_full + d, carry, True)
    for hh in range(2):
        acc = carry[hh] if bounded else carry[hh][1]
        o_ref[:, head_lanes[hh]] = (acc / acc[:, V_DIM:V_DIM + 1]).astype(o_ref.dtype)


def _attention(q, k, v, bound_ok, *, tq, tk, n_past, lk_valid, diag_tiles):
    B, L, _ = q.shape
    Lk = k.shape[1]

    def run(bounded):
        kern = functools.partial(_attn_kernel, tq=tq, tk=tk, n_past=n_past, lk_valid=lk_valid,
                                 diag_tiles=diag_tiles, bounded=bounded)
        return pl.pallas_call(
            kern,
            grid=(B, MLA_HEADS // 2, L // tq),
            in_specs=[pl.BlockSpec((None, tq, 2 * HEAD_PAD), lambda b, p, i: (b, i, p)),
                      pl.BlockSpec((None, Lk, 2 * HEAD_PAD), lambda b, p, i: (b, 0, p)),
                      pl.BlockSpec((None, Lk, 2 * HEAD_PAD), lambda b, p, i: (b, 0, p))],
            out_specs=pl.BlockSpec((None, tq, 2 * HEAD_PAD), lambda b, p, i: (b, i, p)),
            out_shape=jax.ShapeDtypeStruct((B, L, MLA_HEADS * HEAD_PAD), BF16),
            compiler_params=_cparams(("arbitrary", "arbitrary", "arbitrary")),
            name="attn_bounded" if bounded else "attn_running_max",
        )(q, k, v)

    return lax.cond(bound_ok, lambda: run(True), lambda: run(False))


def _rwkv_kernel(z_ref, shift_ref, s0_ref, mu_ref, w0_ref, w2_ref, a0_ref, a2_ref, g2_ref, kk_ref, ka_ref, rk_ref,
                 lng_ref, lnb_ref, ones_ref, o_ref, sout_ref, s_scr, prev_scr, *, C):
    c = pl.program_id(1)

    @pl.when(c == 0)
    def _():
        s_scr[...] = s0_ref[...]
        prev_scr[...] = shift_ref[...]

    zf = z_ref[...]
    row = lax.broadcasted_iota(I32, (C, 1), 0)
    pv = jnp.where(row == 0, prev_scr[...], pltpu.roll(zf, 1, axis=0))
    prev_scr[...] = zf[C - 1:C, :]
    zm = zf + (pv - zf) * mu_ref[...]
    H = RWKV_DIM
    r = zm[:, 0:H]
    k = zm[:, H:2 * H]
    v = zm[:, 2 * H:3 * H]
    xwa = zm[:, 3 * H:3 * H + W_LORA + A_LORA]
    xg = zm[:, 3 * H + W_LORA + A_LORA:]
    lw = (-math.exp(-0.5)) * _sigmoid(w0_ref[...] + _dot(jnp.tanh(xwa).astype(BF16), w2_ref[...]))
    a = _sigmoid(a0_ref[...] + _dot(xwa.astype(BF16), a2_ref[...]))
    g = _dot(_sigmoid(xg).astype(BF16), g2_ref[...])
    ones_bd = ones_ref[...]
    kk = k * kk_ref[...]
    kkn = kk * lax.rsqrt(jnp.maximum(_dot2_exact_rhs(kk * kk, ones_bd), 1e-24))
    k2 = k * (1.0 + (a - 1.0) * ka_ref[...])
    bonus = _dot2_exact_rhs(r * k2 * rk_ref[...], ones_bd) * v

    ti = lax.broadcasted_iota(I32, (C, C), 0)
    tj = lax.broadcasted_iota(I32, (C, C), 1)
    incl = ti >= tj
    strict = ti > tj
    tri = jnp.where(incl, 1.0, 0.0).astype(BF16)
    p1 = lw.astype(BF16)
    r1 = lw - p1.astype(F32)
    p2 = r1.astype(BF16)
    p3 = (r1 - p2.astype(F32)).astype(BF16)
    cs = _dot(tri, p1) + (_dot(tri, p2) + _dot(tri, p3))
    w_incl = jnp.exp(cs)
    w_inv = jnp.exp(-cs)
    at = -kkn * jnp.exp(cs - lw)
    rt = r * w_incl
    bt = kkn * a * w_inv
    kt = k2 * w_inv
    wc = w_incl[C - 1:C, :]
    bh = bt * wc
    kh = kt * wc
    eye = jnp.where(ti == tj, 1.0, 0.0).astype(F32)

    heads = range(RWKV_HEADS)
    per_head = lambda x: [x[:, hd * RWKV_HEAD:(hd + 1) * RWKV_HEAD] for hd in heads]
    rows = lambda xs: jnp.concatenate(xs, axis=0)
    hi_lo = lambda x: tuple(per_head(p) for p in _split(x))
    (at_hi, at_lo), (bt_hi, bt_lo), (kt_hi, kt_lo) = hi_lo(at), hi_lo(bt), hi_lo(kt)
    (v_hi, v_lo), (bh_hi, bh_lo), (kh_hi, kh_lo) = hi_lo(v), hi_lo(bh), hi_lo(kh)
    rt_b = per_head(rt.astype(BF16))
    wc_h = per_head(wc)
    s_all = [s_scr[hd] for hd in heads]
    s_hi, s_lo = zip(*[_split(s) for s in s_all])

    g1 = [_dot_nt(rows([at_hi[h], at_lo[h], rt_b[h]]), rows([bt_hi[h], kt_hi[h]])) for h in heads]
    g2 = [_dot_nt(at_hi[h], rows([bt_lo[h], kt_lo[h]])) for h in heads]
    a_ab = [jnp.where(strict, g1[h][:C, :C] + g1[h][C:2 * C, :C] + g2[h][:, :C], 0.0) for h in heads]
    a_ak = [jnp.where(strict, g1[h][:C, C:] + g1[h][C:2 * C, C:] + g2[h][:, C:], 0.0) for h in heads]
    a_rb = [jnp.where(incl, g1[h][2 * C:, :C], 0.0).astype(BF16) for h in heads]
    a_rk = [jnp.where(incl, g1[h][2 * C:, C:], 0.0).astype(BF16) for h in heads]

    def dot_hl(a, b_hi, b_lo):
        ah, al = _split(a)
        m = a.shape[0]
        r = _dot(rows([ah, al]), b_hi)
        return r[:m] + r[m:] + _dot(ah, b_lo)

    pw = a_ab
    tinv = [eye + p for p in pw]
    n = 2
    while n < C:
        pw_b = [p.astype(BF16) for p in pw]
        pw = [_dot(p, p) for p in pw_b]
        tinv = [t + _dot(t.astype(BF16), p.astype(BF16)) for t, p in zip(tinv, pw)]
        n *= 2
    resid = [(eye - t) + dot_hl(a, *_split(t)) for a, t in zip(a_ab, tinv)]
    tinv = [t + _dot(t.astype(BF16), r_.astype(BF16)) for t, r_ in zip(tinv, resid)]

    x_s = [_dot_nt(rows([at_hi[h], at_lo[h]]), s_hi[h]) for h in heads]
    x = [x_s[h][:C] + x_s[h][C:] + _dot_nt(at_hi[h], s_lo[h]) + dot_hl(a_ak[h], v_hi[h], v_lo[h]) for h in heads]
    u = [dot_hl(tinv[h], *_split(x[h])) for h in heads]
    u_hi, u_lo = zip(*[_split(u_) for u_ in u])
    ys = [_dot_nt(rt_b[h], s_hi[h]) + _dot(a_rb[h], u_hi[h]) + _dot(a_rk[h], v_hi[h]) for h in heads]
    for h in heads:
        upd = _dot_tn(rows([u_hi[h], v_hi[h], u_lo[h], v_lo[h], u_hi[h], v_hi[h]]),
                      rows([bh_hi[h], kh_hi[h], bh_hi[h], kh_hi[h], bh_lo[h], kh_lo[h]]))
        s_scr[h] = s_all[h] * wc_h[h] + upd

    y = jnp.concatenate(ys, axis=-1)
    inv_n = 1.0 / RWKV_HEAD
    mean = _dot2_exact_rhs(y, ones_bd) * inv_n
    d = y - mean
    var = _dot2_exact_rhs(d * d, ones_bd) * inv_n
    yn = d * lax.rsqrt(var + LN_X_EPS) * lng_ref[...] + lnb_ref[...]
    o_ref[...] = ((yn + bonus) * g).astype(o_ref.dtype)

    @pl.when(c == pl.num_programs(1) - 1)
    def _():
        sout_ref[...] = s_scr[...]


def _rwkv(z, shift0, s0, wts, C):
    B, L, _ = z.shape
    fixed = lambda b, c: (0, 0)
    full = lambda a: pl.BlockSpec(a.shape, fixed)
    names = ["mu", "w0", "w2", "a0", "a2", "g2", "k_k", "k_a", "r_k", "lnx_g", "lnx_b", "ones_bd"]
    return pl.pallas_call(
        functools.partial(_rwkv_kernel, C=C),
        grid=(B, L // C),
        in_specs=[pl.BlockSpec((None, C, RWKV_COLS), lambda b, c: (b, c, 0)),
                  pl.BlockSpec((None, 1, RWKV_COLS), lambda b, c: (b, 0, 0)),
                  pl.BlockSpec((None, RWKV_HEADS, RWKV_HEAD, RWKV_HEAD), lambda b, c: (b, 0, 0, 0))]
                 + [full(wts[n]) for n in names],
        out_specs=[pl.BlockSpec((None, C, RWKV_DIM), lambda b, c: (b, c, 0)),
                   pl.BlockSpec((None, RWKV_HEADS, RWKV_HEAD, RWKV_HEAD), lambda b, c: (b, 0, 0, 0))],
        out_shape=[jax.ShapeDtypeStruct((B, L, RWKV_DIM), BF16),
                   jax.ShapeDtypeStruct((B, RWKV_HEADS, RWKV_HEAD, RWKV_HEAD), F32)],
        scratch_shapes=[pltpu.VMEM((RWKV_HEADS, RWKV_HEAD, RWKV_HEAD), F32), pltpu.VMEM((1, RWKV_COLS), F32)],
        compiler_params=_cparams(("arbitrary", "arbitrary")),
        name="rwkv",
    )(z, shift0, s0, *[wts[n] for n in names])


def _merge_kernel(x_ref, oa_ref, ob_ref, gl_ref, woa_ref, wob_ref, fg_ref, wrh_ref, wrl_ref, br_ref,
                  x1_ref, h2_ref, ri_ref, rw_ref, cnt_ref, run_scr, *, tm):
    @pl.when(pl.program_id(0) == 0)
    def _():
        run_scr[...] = jnp.zeros_like(run_scr)

    pa = _dot(oa_ref[...], woa_ref[...])
    pb = _dot(ob_ref[...], wob_ref[...])
    x1 = x_ref[...] + (_sigmoid(gl_ref[:, :D_MODEL]) * pa + _sigmoid(gl_ref[:, D_MODEL:]) * pb)
    x1_ref[...] = x1
    h2 = x1 * _rms(x1) * fg_ref[...]
    h2_ref[...] = h2
    hh, hl = _split(h2)
    lg = _dot(hh, wrh_ref[...]) + (_dot(hl, wrh_ref[...]) + _dot(hh, wrl_ref[...])) + br_ref[...]

    lane = lax.broadcasted_iota(I32, (tm, LANES), 1)
    lane_f = lane.astype(F32)
    first = lambda hit: jnp.min(jnp.where(hit, lane_f, float(LANES)), axis=-1, keepdims=True).astype(I32)
    is_group = jnp.logical_and(lane >= ROUTE_GROUP_LANE, lane < ROUTE_GROUP_LANE + N_GROUPS)
    gl = jnp.where(is_group, lg, MASK_VALUE)
    gmax = jnp.max(gl, axis=-1, keepdims=True)
    gidx = first(gl == gmax) - ROUTE_GROUP_LANE
    g_top = 1.0 / jnp.sum(jnp.where(is_group, jnp.exp(gl - gmax), 0.0), axis=-1, keepdims=True)
    in_group = jnp.logical_and(lane < N_EXPERTS, jnp.right_shift(lane, 3) == gidx)
    el = jnp.where(in_group, lg, MASK_VALUE)
    e1 = jnp.max(el, axis=-1, keepdims=True)
    i1 = first(el == e1)
    el2 = jnp.where(lane == i1, MASK_VALUE, el)
    e2 = jnp.max(el2, axis=-1, keepdims=True)
    i2 = first(el2 == e2)
    t = jnp.exp(e2 - e1)
    w1 = g_top / (1.0 + t)
    w2 = g_top * t / (1.0 + t)

    hit1 = lane == i1
    hit2 = lane == i2
    oh = jnp.where(jnp.logical_or(hit1, hit2), 1.0, 0.0)
    ti = lax.broadcasted_iota(I32, (tm, tm), 0)
    tj = lax.broadcasted_iota(I32, (tm, tm), 1)
    before = _dot(jnp.where(ti > tj, 1.0, 0.0).astype(BF16), oh.astype(BF16)) + run_scr[...]
    rank1 = jnp.sum(jnp.where(hit1, before, 0.0), axis=-1, keepdims=True)
    rank2 = jnp.sum(jnp.where(hit2, before, 0.0), axis=-1, keepdims=True)
    run = run_scr[...] + jnp.sum(oh, axis=0, keepdims=True)
    run_scr[...] = run
    cnt_ref[...] = run.astype(I32)

    sel = lambda vals: functools.reduce(lambda acc, kv: jnp.where(lane == kv[0], kv[1], acc), enumerate(vals), 0)
    ri_ref[...] = sel([i1, i2, rank1.astype(I32), rank2.astype(I32)])
    rw_ref[...] = sel([w1, w2])


def _merge(x, oa, ob, gl, wts, tm):
    T = x.shape[0]
    row = lambda i: (i, 0)
    full = lambda a: pl.BlockSpec(a.shape, lambda i: (0,) * a.ndim)
    names = ["w_out_a", "w_out_b", "ffn_g", "wr_hi", "wr_lo", "b_r"]
    return pl.pallas_call(
        functools.partial(_merge_kernel, tm=tm),
        grid=(T // tm,),
        in_specs=[pl.BlockSpec((tm, D_MODEL), row), pl.BlockSpec((tm, MLA_HEADS * HEAD_PAD), row),
                  pl.BlockSpec((tm, RWKV_DIM), row), pl.BlockSpec((tm, GATE_COLS), row)]
                 + [full(wts[n]) for n in names],
        out_specs=[pl.BlockSpec((tm, D_MODEL), row), pl.BlockSpec((tm, D_MODEL), row),
                   pl.BlockSpec((tm, LANES), row), pl.BlockSpec((tm, LANES), row),
                   pl.BlockSpec((1, LANES), lambda i: (0, 0))],
        out_shape=[jax.ShapeDtypeStruct((T, D_MODEL), F32), jax.ShapeDtypeStruct((T, D_MODEL), F32),
                   jax.ShapeDtypeStruct((T, LANES), I32), jax.ShapeDtypeStruct((T, LANES), F32),
                   jax.ShapeDtypeStruct((1, LANES), I32)],
        scratch_shapes=[pltpu.VMEM((1, LANES), F32)],
        compiler_params=_cparams(("arbitrary",)),
        name="merge",
    )(x, oa, ob, gl, *[wts[n] for n in names])


def _dispatch_kernel(dest_ref, h_ref, xs_in_ref, xs_ref, sem, *, tm):
    del xs_in_ref

    def issue(t, carry):
        for s in range(TOP_K):
            d = dest_ref[0, 0, TOP_K * t + s]
            pltpu.make_async_copy(h_ref.at[pl.ds(t, 1), :], xs_ref.at[pl.ds(d, 1), :], sem).start()
        return carry

    lax.fori_loop(0, tm, issue, 0)
    for s in range(TOP_K):
        pltpu.make_async_copy(h_ref, xs_ref.at[pl.ds(0, tm), :], sem).wait()


def _dispatch(h2, dest, n_rows, tm):
    T = h2.shape[0]
    xs0 = jnp.zeros((n_rows, D_MODEL), F32)
    dest3 = dest.reshape(T // tm, 1, TOP_K * tm)
    return pl.pallas_call(
        functools.partial(_dispatch_kernel, tm=tm),
        grid=(T // tm,),
        in_specs=[pl.BlockSpec((1, 1, TOP_K * tm), lambda i: (i, 0, 0), memory_space=pltpu.SMEM),
                  pl.BlockSpec((tm, D_MODEL), lambda i: (i, 0)),
                  pl.BlockSpec(memory_space=pl.ANY)],
        out_specs=pl.BlockSpec(memory_space=pl.ANY),
        out_shape=jax.ShapeDtypeStruct((n_rows, D_MODEL), F32),
        scratch_shapes=[pltpu.SemaphoreType.DMA(())],
        input_output_aliases={2: 0},
        compiler_params=_cparams(("arbitrary",)),
        name="dispatch",
    )(dest3, h2, xs0)


def _ffn_kernel(be_ref, nv_ref, x_ref, wg_ref, wu_ref, wd_ref, y_ref):
    del be_ref
    live = pl.program_id(0) < nv_ref[0]

    @pl.when(live)
    def _():
        x = x_ref[...].astype(BF16)
        gate = _dot(x, wg_ref[...])
        up = _dot(x, wu_ref[...])
        y_ref[...] = _dot((gate * _sigmoid(gate) * up).astype(BF16), wd_ref[...])

    @pl.when(jnp.logical_not(live))
    def _():
        y_ref[...] = jnp.zeros_like(y_ref)


def _ffn(xs, block_e, n_live, wts):
    P = xs.shape[0]
    nb = P // FFN_ROWS
    grid_spec = pltpu.PrefetchScalarGridSpec(
        num_scalar_prefetch=2,
        grid=(nb,),
        in_specs=[pl.BlockSpec((FFN_ROWS, D_MODEL), lambda i, be, nv: (i, 0)),
                  pl.BlockSpec((None, D_MODEL, EXPERT_FF), lambda i, be, nv: (be[i], 0, 0)),
                  pl.BlockSpec((None, D_MODEL, EXPERT_FF), lambda i, be, nv: (be[i], 0, 0)),
                  pl.BlockSpec((None, EXPERT_FF, D_MODEL), lambda i, be, nv: (be[i], 0, 0))],
        out_specs=pl.BlockSpec((FFN_ROWS, D_MODEL), lambda i, be, nv: (i, 0)),
    )
    return pl.pallas_call(
        _ffn_kernel,
        grid_spec=grid_spec,
        out_shape=jax.ShapeDtypeStruct((P, D_MODEL), F32),
        compiler_params=_cparams(("arbitrary",)),
        name="ffn",
    )(block_e, n_live, xs, wts["w_eg"], wts["w_eu"], wts["w_ed"])


def _combine_kernel(dest_ref, x1_ref, rw_ref, yb_ref, o_ref, y_scr, sem, *, tm):
    def issue(t, carry):
        for s in range(TOP_K):
            d = dest_ref[0, 0, TOP_K * t + s]
            pltpu.make_async_copy(yb_ref.at[pl.ds(d, 1), :], y_scr.at[s, pl.ds(t, 1), :], sem).start()
        return carry

    lax.fori_loop(0, tm, issue, 0)
    for s in range(TOP_K):
        pltpu.make_async_copy(yb_ref.at[pl.ds(0, tm), :], y_scr.at[s], sem).wait()
    rw = rw_ref[...]
    o_ref[...] = x1_ref[...] + (rw[:, 0:1] * y_scr[0] + rw[:, 1:2] * y_scr[1])


def _combine(x1, rw, yb, dest, tm):
    T = x1.shape[0]
    dest3 = dest.reshape(T // tm, 1, TOP_K * tm)
    return pl.pallas_call(
        functools.partial(_combine_kernel, tm=tm),
        grid=(T // tm,),
        in_specs=[pl.BlockSpec((1, 1, TOP_K * tm), lambda i: (i, 0, 0), memory_space=pltpu.SMEM),
                  pl.BlockSpec((tm, D_MODEL), lambda i: (i, 0)),
                  pl.BlockSpec((tm, LANES), lambda i: (i, 0)),
                  pl.BlockSpec(memory_space=pl.ANY)],
        out_specs=pl.BlockSpec((tm, D_MODEL), lambda i: (i, 0)),
        out_shape=jax.ShapeDtypeStruct((T, D_MODEL), F32),
        scratch_shapes=[pltpu.VMEM((TOP_K, tm, D_MODEL), F32), pltpu.SemaphoreType.DMA(())],
        compiler_params=_cparams(("arbitrary",)),
        name="combine",
    )(dest3, x1, rw, yb)


def _moe(x1, h2, ri, rw, cnt, wts, tm):
    T = x1.shape[0]
    ids = ri[:, 0:TOP_K]
    ranks = ri[:, TOP_K:2 * TOP_K]
    counts = cnt[0, :N_EXPERTS]
    padded = (counts + FFN_ROWS - 1) // FFN_ROWS * FFN_ROWS
    pend = jnp.cumsum(padded)
    pstart = pend - padded
    dest = (pstart[ids] + ranks).astype(I32)
    nb = -(-T * TOP_K // FFN_ROWS) + N_EXPERTS
    block_pos = jnp.arange(nb, dtype=I32) * FFN_ROWS
    block_e = jnp.minimum(jnp.sum(pend[None, :] <= block_pos[:, None], axis=1), N_EXPERTS - 1).astype(I32)
    n_live = (pend[-1:] // FFN_ROWS).astype(I32)
    xs = _dispatch(h2, dest, nb * FFN_ROWS, tm)
    yb = _ffn(xs, block_e, n_live, wts)
    return _combine(x1, rw, yb, dest, tm)


def _rope_tables(pos):
    inv = 1.0 / (ROPE_BASE ** (jnp.arange(HALF_ROPE, dtype=F32) * (2.0 / QK_ROPE)))
    ang = pos.astype(F32)[:, None] * inv[None, :]
    cos, sin = jnp.cos(ang), jnp.sin(ang)
    n = pos.shape[0]
    pad = jnp.zeros((n, HEAD_PAD - QK_DIM), F32)
    ct = jnp.concatenate([jnp.ones((n, QK_NOPE), F32), cos, cos, pad], axis=1)
    st = jnp.concatenate([jnp.zeros((n, QK_NOPE), F32), -sin, sin, pad], axis=1)
    kt = jnp.concatenate([cos, sin], axis=1)
    return ct, st, kt


def _score_bound(q_head_norm_g, k_head_norm_g):
    return (1.02 * QK_DIM * ATTN_SCALE * math.log2(math.e)) * (
        jnp.max(jnp.abs(q_head_norm_g)) * jnp.max(jnp.abs(k_head_norm_g)))


def _prep_weights(norm_mix_g, w_in, q_norm_g, w_uq, kv_norm_g, w_ukv, q_head_norm_g, k_head_norm_g,
                  rwkv_mu, rwkv_w0, rwkv_w2, rwkv_a0, rwkv_a2, rwkv_g2, rwkv_k_k, rwkv_k_a, rwkv_r_k,
                  rwkv_lnx_g, rwkv_lnx_b, w_out, norm_ffn_g, w_router_group, b_router_group,
                  w_router_expert, b_router_expert, w_expert_gate, w_expert_up, w_expert_down):
    row = lambda a: a.reshape(1, -1).astype(F32)
    head_pad = lambda a: jnp.pad(a, ((0, 0), (0, 0), (0, HEAD_PAD - a.shape[-1])))
    w = {}
    w["mix_g"] = row(norm_mix_g)
    w["w_in"] = jnp.concatenate([w_in[:, :MLA_COLS], jnp.zeros((D_MODEL, LANES - QK_ROPE), F32),
                                 w_in[:, MLA_COLS:]], axis=1).astype(BF16)
    w["q_g"] = row(q_norm_g)
    w["kv_g"] = row(kv_norm_g)
    w["wqa"] = head_pad(w_uq).reshape(Q_LORA, MLA_HEADS * HEAD_PAD).astype(BF16)
    partner = jnp.concatenate([jnp.zeros_like(w_uq[..., :QK_NOPE]), w_uq[..., QK_NOPE + HALF_ROPE:],
                               w_uq[..., QK_NOPE:QK_NOPE + HALF_ROPE]], axis=-1)
    w["wqb"] = head_pad(partner).reshape(Q_LORA, MLA_HEADS * HEAD_PAD).astype(BF16)
    pad_g = lambda g: jnp.pad(g, (0, HEAD_PAD - QK_DIM)).reshape(1, HEAD_PAD).astype(F32)
    w["qh_g"] = pad_g(q_head_norm_g) * (ATTN_SCALE * math.log2(math.e))
    w["kh_g"] = pad_g(k_head_norm_g)
    lane_one = lambda lane: (jnp.arange(HEAD_PAD) == lane).astype(F32).reshape(1, HEAD_PAD)
    w["score_bound"] = _score_bound(q_head_norm_g, k_head_norm_g)
    w["q_one"] = lane_one(QK_DIM)
    w["k_shift"] = -w["score_bound"] * lane_one(QK_DIM)
    w["v_one"] = jnp.tile(lane_one(V_DIM), (1, MLA_HEADS))
    w["wk"] = head_pad(w_ukv[..., :QK_NOPE]).reshape(KV_LORA, MLA_HEADS * HEAD_PAD).astype(BF16)
    w["wv"] = head_pad(w_ukv[..., QK_NOPE:]).reshape(KV_LORA, MLA_HEADS * HEAD_PAD).astype(BF16)
    w["place"] = jnp.pad(jnp.eye(QK_ROPE, dtype=F32), ((0, 0), (QK_NOPE, HEAD_PAD - QK_DIM))).astype(BF16)
    w["mu"] = row(rwkv_mu)
    w["w0"] = row(rwkv_w0)
    w["w2"] = jnp.concatenate([rwkv_w2, jnp.zeros((A_LORA, RWKV_DIM), F32)], axis=0).astype(BF16)
    w["a0"] = row(rwkv_a0)
    w["a2"] = jnp.concatenate([jnp.zeros((W_LORA, RWKV_DIM), F32), rwkv_a2], axis=0).astype(BF16)
    w["g2"] = rwkv_g2.astype(BF16)
    w["k_k"] = row(rwkv_k_k)
    w["k_a"] = row(rwkv_k_a)
    w["r_k"] = row(rwkv_r_k)
    w["lnx_g"] = row(rwkv_lnx_g)
    w["lnx_b"] = row(rwkv_lnx_b)
    head_of = jnp.arange(RWKV_DIM) // RWKV_HEAD
    w["ones_bd"] = (head_of[:, None] == head_of[None, :]).astype(BF16)
    w["w_out_a"] = jnp.pad(w_out[0].reshape(MLA_HEADS, V_DIM, D_MODEL), ((0, 0), (0, HEAD_PAD - V_DIM), (0, 0))
                           ).reshape(MLA_HEADS * HEAD_PAD, D_MODEL).astype(BF16)
    w["w_out_b"] = w_out[1].astype(BF16)
    w["ffn_g"] = row(norm_ffn_g)
    wr = jnp.concatenate([w_router_expert.reshape(D_MODEL, N_EXPERTS), w_router_group,
                          jnp.zeros((D_MODEL, LANES - N_EXPERTS - N_GROUPS), F32)], axis=1)
    w["wr_hi"] = wr.astype(BF16)
    w["wr_lo"] = (wr - w["wr_hi"].astype(F32)).astype(BF16)
    w["b_r"] = jnp.concatenate([b_router_expert.reshape(-1), b_router_group,
                                jnp.zeros((LANES - N_EXPERTS - N_GROUPS,), F32)]).reshape(1, LANES)
    w["w_eg"] = w_expert_gate.astype(BF16)
    w["w_eu"] = w_expert_up.astype(BF16)
    w["w_ed"] = w_expert_down.astype(BF16)
    return w


def _round_up(n, m):
    return -(-n // m) * m


def _layer(x, pos, lat_past, kpe_past, wkv0, shift0, wts):
    B, L, _ = x.shape
    T = B * L
    tm = min(256, T)
    ct, st, kt = _rope_tables(pos)
    if L % tm:
        reps = tm // L
        ct, st, kt = (jnp.tile(t, (reps, 1)) for t in (ct, st, kt))
    q, lat, kpe, z, gl = _inproj(x.reshape(T, D_MODEL), (ct, st, kt), wts, tm)
    lat3 = lat.reshape(B, L, KV_LORA)
    kpe3 = kpe.reshape(B, L, QK_ROPE)
    if lat_past is None:
        n_past, lk, lat_all, kpe_all = 0, L, lat3, kpe3
        tq = tk = min(512, L)
        diag = True
    else:
        n_past = lat_past.shape[1]
        lk_valid = n_past + L
        lk = _round_up(lk_valid, LANES)
        padk = lambda a: jnp.pad(a, ((0, 0), (0, lk - lk_valid), (0, 0)))
        lat_all = padk(jnp.concatenate([lat_past, lat3], axis=1))
        kpe_all = padk(jnp.concatenate([kpe_past, kpe3], axis=1))
        tq, tk, diag = L, lk, False
    k, v = _kvprep(lat_all.reshape(B * lk, KV_LORA), kpe_all.reshape(B * lk, QK_ROPE), wts, min(256, B * lk))
    oa = _attention(q.reshape(B, L, -1), k.reshape(B, lk, -1), v.reshape(B, lk, -1),
                    wts["score_bound"] <= MAX_DIRECT_EXP2_SHIFT, tq=tq, tk=tk, n_past=n_past,
                    lk_valid=n_past + L, diag_tiles=diag)
    z3 = z.reshape(B, L, RWKV_COLS)
    ob, s_new = _rwkv(z3, shift0, wkv0, wts, min(SCAN_CHUNK, L))
    x1, h2, ri, rw, cnt = _merge(x.reshape(T, D_MODEL), oa.reshape(T, -1), ob.reshape(T, -1), gl, wts, tm)
    y = _moe(x1, h2, ri, rw, cnt, wts, tm)
    return y.reshape(B, L, D_MODEL), lat3, kpe3, s_new, z3[:, L - 1:, :]


def kernel(x_prompt, x_sample, cache_kv_latent, cache_k_rope, state_wkv, state_shift, norm_mix_g, w_in, q_norm_g, w_uq, kv_norm_g, w_ukv, q_head_norm_g, k_head_norm_g, rwkv_mu, rwkv_w0, rwkv_w2, rwkv_a0, rwkv_a2, rwkv_g2, rwkv_k_k, rwkv_k_a, rwkv_r_k, rwkv_lnx_g, rwkv_lnx_b, w_out, norm_ffn_g, w_router_group, b_router_group, w_router_expert, b_router_expert, w_expert_gate, w_expert_up, w_expert_down):
    weights = (norm_mix_g, w_in, q_norm_g, w_uq, kv_norm_g, w_ukv, q_head_norm_g, k_head_norm_g,
               rwkv_mu, rwkv_w0, rwkv_w2, rwkv_a0, rwkv_a2, rwkv_g2, rwkv_k_k, rwkv_k_a, rwkv_r_k,
               rwkv_lnx_g, rwkv_lnx_b, w_out, norm_ffn_g, w_router_group, b_router_group,
               w_router_expert, b_router_expert, w_expert_gate, w_expert_up, w_expert_down)
    depth = w_in.shape[0]
    bp, lp, _ = x_prompt.shape
    n_past = cache_kv_latent.shape[2]
    pos_p = jnp.arange(lp, dtype=I32)
    pos_s = n_past + jnp.arange(x_sample.shape[1], dtype=I32)
    wkv_zero = jnp.zeros((bp, RWKV_HEADS, RWKV_HEAD, RWKV_HEAD), F32)
    shift_zero = jnp.zeros((bp, 1, RWKV_COLS), F32)
    xp, xs = x_prompt, x_sample
    outs_p, outs_s = [], []
    for l in range(depth):
        wts = _prep_weights(*(wt[l] for wt in weights))
        xp, *rest_p = _layer(xp, pos_p, None, None, wkv_zero, shift_zero, wts)
        xs, *rest_s = _layer(xs, pos_s, cache_kv_latent[l], cache_k_rope[l], state_wkv[l], state_shift[l], wts)
        outs_p.append(rest_p)
        outs_s.append(rest_s)
    stack = lambda outs, i: jnp.stack([o[i] for o in outs], 0)
    return (xp, xs, stack(outs_p, 0), stack(outs_p, 1), stack(outs_p, 2), stack(outs_p, 3),
            stack(outs_s, 0), stack(outs_s, 1), stack(outs_s, 2), stack(outs_s, 3))
```

```python
import functools
import math

import jax
import jax.numpy as jnp
from jax import lax
from jax.experimental import pallas as pl
from jax.experimental.pallas import tpu as pltpu

F32 = jnp.float32
BF16 = jnp.bfloat16
I32 = jnp.int32

D_MODEL = 1024
CHUNK = 64
RMS_EPS = 1e-6
MASK_VALUE = -1e30
MLA_HEADS = 8
QK_NOPE = 64
QK_ROPE = 32
QK_DIM = QK_NOPE + QK_ROPE
V_DIM = 64
Q_LORA = 256
KV_LORA = 128
ROPE_BASE = 10000.0
ATTN_SCALE = QK_DIM ** -0.5
RWKV_HEADS = 8
RWKV_HEAD = 64
RWKV_DIM = RWKV_HEADS * RWKV_HEAD
W_LORA = 64
A_LORA = 64
G_LORA = 128
LN_X_EPS = 64e-5
MLA_COLS = Q_LORA + KV_LORA + QK_ROPE
RWKV_COLS = 3 * RWKV_DIM + W_LORA + A_LORA + G_LORA
GATE_COLS = 2 * D_MODEL
N_GROUPS = 4
EXPERTS_PER_GROUP = 8
N_EXPERTS = N_GROUPS * EXPERTS_PER_GROUP
EXPERT_FF = 256
TOP_K = 2

LANES = 128
ROW_TILE = D_MODEL // LANES
DMA_ISSUE_UNROLL = 4
HEAD_PAD = LANES
HALF_ROPE = QK_ROPE // 2
SEG_Q = 0
SEG_KV = SEG_Q + Q_LORA
SEG_PE = SEG_KV + KV_LORA
SEG_Z = SEG_PE + LANES
SEG_G = SEG_Z + RWKV_COLS
IN_COLS_PAD = SEG_G + GATE_COLS
ROUTE_GROUP_LANE = N_EXPERTS
SCAN_CHUNK = 128
FFN_ROWS = 256
ATTN_TILE = 512
MAX_DIRECT_EXP2_SHIFT = 40.0
VMEM_LIMIT = 48 * 1024 * 1024


def _cparams(sem, vmem=VMEM_LIMIT):
    return pltpu.CompilerParams(dimension_semantics=sem, vmem_limit_bytes=vmem)


def _dot(a, b):
    return jnp.dot(a, b, preferred_element_type=F32)


def _dot_nt(a, b):
    return lax.dot_general(a, b, (((1,), (1,)), ((), ())), preferred_element_type=F32)


def _dot_tn(a, b):
    return lax.dot_general(a, b, (((0,), (0,)), ((), ())), preferred_element_type=F32)


def _split(x):
    hi = x.astype(BF16)
    lo = (x - hi.astype(F32)).astype(BF16)
    return hi, lo


def _dot2_exact_rhs(a, b_bf16):
    ah, al = _split(a)
    return _dot(ah, b_bf16) + _dot(al, b_bf16)


def _sigmoid(x):
    return 1.0 / (1.0 + jnp.exp(-x))


def _rms(x, n=None):
    n = x.shape[-1] if n is None else n
    return lax.rsqrt(jnp.sum(x * x, axis=-1, keepdims=True) * (1.0 / n) + RMS_EPS)


def _store_token_tiles(ref, x):
    n = x.shape[0]
    for c in range(ROW_TILE):
        ref[pl.ds(c, n, stride=ROW_TILE), :] = x[:, c * LANES:(c + 1) * LANES]


def _load_token_tiles(ref, n):
    return jnp.concatenate([ref[pl.ds(c, n, stride=ROW_TILE), :] for c in range(ROW_TILE)], axis=1)


def _inproj_kernel(x_ref, g_ref, w_ref, qg_ref, wqa_ref, wqb_ref, ct_ref, st_ref, kt_ref, kvg_ref, qhg_ref, qone_ref,
                   q_ref, lat_ref, kpe_ref, z_ref, gl_ref):
    x = x_ref[...]
    h = (x * _rms(x) * g_ref[...]).astype(BF16)
    cq = _dot(h, w_ref[:, SEG_Q:SEG_KV])
    ckv = _dot(h, w_ref[:, SEG_KV:SEG_PE])
    pe = _dot(h, w_ref[:, SEG_PE:SEG_Z])
    z_ref[...] = _dot(h, w_ref[:, SEG_Z:SEG_G])
    gl_ref[...] = _dot(h, w_ref[:, SEG_G:IN_COLS_PAD])
    lat_ref[...] = ckv * _rms(ckv) * kvg_ref[...]
    x1 = pe[:, :HALF_ROPE]
    x2 = pe[:, HALF_ROPE:QK_ROPE]
    c = kt_ref[:, :HALF_ROPE]
    s = kt_ref[:, HALF_ROPE:QK_ROPE]
    kpe_ref[...] = jnp.concatenate([x1 * c - x2 * s, x1 * s + x2 * c], axis=-1)
    cqn = (cq * _rms(cq) * qg_ref[...]).astype(BF16)
    ct = ct_ref[...]
    st = st_ref[...]
    qhg = qhg_ref[...]
    qone = qone_ref[...]
    for hd in range(MLA_HEADS):
        sl = slice(hd * HEAD_PAD, (hd + 1) * HEAD_PAD)
        xh = _dot(cqn, wqa_ref[:, sl]) * ct + _dot(cqn, wqb_ref[:, sl]) * st
        q_ref[:, sl] = (xh * _rms(xh, QK_DIM) * qhg + qone).astype(BF16)


def _inproj(x, tabs, wts, tm):
    T = x.shape[0]
    ct, st, kt = tabs
    nt = ct.shape[0] // tm
    row = lambda i: (i, 0)
    tab = lambda i: (i % nt, 0)
    fixed = lambda i: (0, 0)
    full = lambda a: pl.BlockSpec(a.shape, fixed)
    return pl.pallas_call(
        _inproj_kernel,
        grid=(T // tm,),
        in_specs=[pl.BlockSpec((tm, D_MODEL), row), full(wts["mix_g"]), full(wts["w_in"]), full(wts["q_g"]),
                  full(wts["wqa"]), full(wts["wqb"]),
                  pl.BlockSpec((tm, HEAD_PAD), tab), pl.BlockSpec((tm, HEAD_PAD), tab),
                  pl.BlockSpec((tm, QK_ROPE), tab), full(wts["kv_g"]), full(wts["qh_g"]), full(wts["q_one"])],
        out_specs=[pl.BlockSpec((tm, MLA_HEADS * HEAD_PAD), row), pl.BlockSpec((tm, KV_LORA), row),
                   pl.BlockSpec((tm, QK_ROPE), row), pl.BlockSpec((tm, RWKV_COLS), row),
                   pl.BlockSpec((tm, GATE_COLS), row)],
        out_shape=[jax.ShapeDtypeStruct((T, MLA_HEADS * HEAD_PAD), BF16), jax.ShapeDtypeStruct((T, KV_LORA), F32),
                   jax.ShapeDtypeStruct((T, QK_ROPE), F32), jax.ShapeDtypeStruct((T, RWKV_COLS), F32),
                   jax.ShapeDtypeStruct((T, GATE_COLS), F32)],
        compiler_params=_cparams(("arbitrary",)),
        name="inproj",
    )(x, wts["mix_g"], wts["w_in"], wts["q_g"], wts["wqa"], wts["wqb"], ct, st, kt, wts["kv_g"], wts["qh_g"],
      wts["q_one"])


def _kvprep_kernel(lat_ref, kpe_ref, wk_ref, wv_ref, place_ref, kg_ref, kshift_ref, vone_ref, k_ref, v_ref):
    lat = lat_ref[...].astype(BF16)
    rot = _dot2_exact_rhs(kpe_ref[...], place_ref[...])
    kg = kg_ref[...]
    kshift = kshift_ref[...]
    for hd in range(MLA_HEADS):
        sl = slice(hd * HEAD_PAD, (hd + 1) * HEAD_PAD)
        kh = _dot(lat, wk_ref[:, sl]) + rot
        k_ref[:, sl] = (kh * _rms(kh, QK_DIM) * kg + kshift).astype(BF16)
    v_ref[...] = (_dot(lat, wv_ref[...]) + vone_ref[...]).astype(BF16)


def _kvprep(lat, kpe, wts, tm):
    T = lat.shape[0]
    row = lambda i: (i, 0)
    fixed = lambda i: (0, 0)
    full = lambda a: pl.BlockSpec(a.shape, fixed)
    return pl.pallas_call(
        _kvprep_kernel,
        grid=(T // tm,),
        in_specs=[pl.BlockSpec((tm, KV_LORA), row), pl.BlockSpec((tm, QK_ROPE), row), full(wts["wk"]),
                  full(wts["wv"]), full(wts["place"]), full(wts["kh_g"]), full(wts["k_shift"]),
                  full(wts["v_one"])],
        out_specs=[pl.BlockSpec((tm, MLA_HEADS * HEAD_PAD), row), pl.BlockSpec((tm, MLA_HEADS * HEAD_PAD), row)],
        out_shape=[jax.ShapeDtypeStruct((T, MLA_HEADS * HEAD_PAD), BF16),
                   jax.ShapeDtypeStruct((T, MLA_HEADS * HEAD_PAD), BF16)],
        compiler_params=_cparams(("arbitrary",)),
        name="kvprep",
    )(lat, kpe, wts["wk"], wts["wv"], wts["place"], wts["kh_g"], wts["k_shift"], wts["v_one"])


def _attn_kernel(q_ref, k_ref, v_ref, o_ref, *, tq, tk, n_past, lk_valid, diag_tiles, bounded):
    i = pl.program_id(2)
    n_full = i * (tq // tk) if diag_tiles else 0
    head_lanes = [slice(hh * HEAD_PAD, (hh + 1) * HEAD_PAD) for hh in range(2)]
    qs = [q_ref[:, hs] for hs in head_lanes]

    def scores(hh, j, masked):
        ks = pl.multiple_of(j * tk, tk)
        s = _dot_nt(qs[hh], k_ref[pl.ds(ks, tk), head_lanes[hh]])
        if masked:
            qpos = n_past + i * tq + lax.broadcasted_iota(I32, (tq, tk), 0)
            kpos = j * tk + lax.broadcasted_iota(I32, (tq, tk), 1)
            vis = jnp.logical_and(jnp.right_shift(kpos, 6) <= jnp.right_shift(qpos, 6), kpos < lk_valid)
            s = jnp.where(vis, s, MASK_VALUE)
        return s, v_ref[pl.ds(ks, tk), head_lanes[hh]]

    if bounded:
        def head_step(hh, j, acc, masked):
            s, v = scores(hh, j, masked)
            return acc + _dot(jnp.exp2(s).astype(BF16), v)

        init = jnp.zeros((tq, HEAD_PAD), F32)
    else:
        def head_step(hh, j, carry, masked):
            m, acc = carry
            s, v = scores(hh, j, masked)
            m_new = jnp.maximum(m, jnp.max(s, axis=-1, keepdims=True))
            return m_new, jnp.exp2(m - m_new) * acc + _dot(jnp.exp2(s - m_new).astype(BF16), v)

        init = (jnp.full((tq, 1), MASK_VALUE, F32), jnp.zeros((tq, HEAD_PAD), F32))

    step = lambda j, carry, masked: tuple(head_step(hh, j, carry[hh], masked) for hh in range(2))
    carry = (init, init)
    if diag_tiles:
        carry = lax.fori_loop(0, n_full, functools.partial(step, masked=False), carry)
    for d in range(tq // tk if diag_tiles else 1):
        carry = step(n_full + d, carry, True)
    for hh in range(2):
        acc = carry[hh] if bounded else carry[hh][1]
        o_ref[:, head_lanes[hh]] = (acc / acc[:, V_DIM:V_DIM + 1]).astype(o_ref.dtype)


def _attention(q, k, v, bound_ok, *, tq, tk, n_past, lk_valid, diag_tiles):
    B, L, _ = q.shape
    Lk = k.shape[1]

    def run(bounded):
        kern = functools.partial(_attn_kernel, tq=tq, tk=tk, n_past=n_past, lk_valid=lk_valid,
                                 diag_tiles=diag_tiles, bounded=bounded)
        return pl.pallas_call(
            kern,
            grid=(B, MLA_HEADS // 2, L // tq),
            in_specs=[pl.BlockSpec((None, tq, 2 * HEAD_PAD), lambda b, p, i: (b, i, p)),
                      pl.BlockSpec((None, Lk, 2 * HEAD_PAD), lambda b, p, i: (b, 0, p)),
                      pl.BlockSpec((None, Lk, 2 * HEAD_PAD), lambda b, p, i: (b, 0, p))],
            out_specs=pl.BlockSpec((None, tq, 2 * HEAD_PAD), lambda b, p, i: (b, i, p)),
            out_shape=jax.ShapeDtypeStruct((B, L, MLA_HEADS * HEAD_PAD), BF16),
            compiler_params=_cparams(("arbitrary", "arbitrary", "arbitrary")),
            name="attn_bounded" if bounded else "attn_running_max",
        )(q, k, v)

    return lax.cond(bound_ok, lambda: run(True), lambda: run(False))


def _rwkv_kernel(z_ref, shift_ref, s0_ref, mu_ref, w0_ref, w2_ref, a0_ref, a2_ref, g2_ref, kk_ref, ka_ref, rk_ref,
                 lng_ref, lnb_ref, ones_ref, o_ref, sout_ref, s_scr, prev_scr, *, C):
    c = pl.program_id(1)

    @pl.when(c == 0)
    def _():
        s_scr[...] = s0_ref[...]
        prev_scr[...] = shift_ref[...]

    zf = z_ref[...]
    row = lax.broadcasted_iota(I32, (C, 1), 0)
    pv = jnp.where(row == 0, prev_scr[...], pltpu.roll(zf, 1, axis=0))
    prev_scr[...] = zf[C - 1:C, :]
    zm = zf + (pv - zf) * mu_ref[...]
    H = RWKV_DIM
    r = zm[:, 0:H]
    k = zm[:, H:2 * H]
    v = zm[:, 2 * H:3 * H]
    xwa = zm[:, 3 * H:3 * H + W_LORA + A_LORA]
    xg = zm[:, 3 * H + W_LORA + A_LORA:]
    lw = (-math.exp(-0.5)) * _sigmoid(w0_ref[...] + _dot(jnp.tanh(xwa).astype(BF16), w2_ref[...]))
    a = _sigmoid(a0_ref[...] + _dot(xwa.astype(BF16), a2_ref[...]))
    g = _dot(_sigmoid(xg).astype(BF16), g2_ref[...])
    ones_bd = ones_ref[...]
    kk = k * kk_ref[...]
    kkn = kk * lax.rsqrt(jnp.maximum(_dot2_exact_rhs(kk * kk, ones_bd), 1e-24))
    k2 = k * (1.0 + (a - 1.0) * ka_ref[...])
    head_sum = lambda x: _dot(x.astype(BF16), ones_bd)
    bonus = head_sum(r * k2 * rk_ref[...]) * v

    ti = lax.broadcasted_iota(I32, (C, C), 0)
    tj = lax.broadcasted_iota(I32, (C, C), 1)
    incl = ti >= tj
    strict = ti > tj
    tri = jnp.where(incl, 1.0, 0.0).astype(BF16)
    p1 = lw.astype(BF16)
    r1 = lw - p1.astype(F32)
    p2 = r1.astype(BF16)
    p3 = (r1 - p2.astype(F32)).astype(BF16)
    cs = _dot(tri, p1) + (_dot(tri, p2) + _dot(tri, p3))
    w_incl = jnp.exp(cs)
    w_inv = jnp.exp(-cs)
    at = -kkn * jnp.exp(cs - lw)
    rt = r * w_incl
    bt = kkn * a * w_inv
    kt = k2 * w_inv
    wc = w_incl[C - 1:C, :]
    bh = bt * wc
    kh = kt * wc
    eye = jnp.where(ti == tj, 1.0, 0.0).astype(F32)

    heads = range(RWKV_HEADS)
    per_head = lambda x: [x[:, hd * RWKV_HEAD:(hd + 1) * RWKV_HEAD] for hd in heads]
    rows = lambda xs: jnp.concatenate(xs, axis=0)
    hi_lo = lambda x: tuple(per_head(p) for p in _split(x))
    (at_hi, at_lo), (bt_hi, bt_lo), (kt_hi, kt_lo) = hi_lo(at), hi_lo(bt), hi_lo(kt)
    (v_hi, v_lo), (bh_hi, bh_lo), (kh_hi, kh_lo) = hi_lo(v), hi_lo(bh), hi_lo(kh)
    rt_b = per_head(rt.astype(BF16))
    wc_h = per_head(wc)
    s_all = [s_scr[hd] for hd in heads]
    s_hi, s_lo = zip(*[_split(s) for s in s_all])

    g1 = [_dot_nt(rows([at_hi[h], at_lo[h], rt_b[h]]), rows([bt_hi[h], kt_hi[h]])) for h in heads]
    g2 = [_dot_nt(at_hi[h], rows([bt_lo[h], kt_lo[h]])) for h in heads]
    a_ab = [jnp.where(strict, g1[h][:C, :C] + g1[h][C:2 * C, :C] + g2[h][:, :C], 0.0) for h in heads]
    a_ak = [jnp.where(strict, g1[h][:C, C:] + g1[h][C:2 * C, C:] + g2[h][:, C:], 0.0) for h in heads]
    a_rb = [jnp.where(incl, g1[h][2 * C:, :C], 0.0).astype(BF16) for h in heads]
    a_rk = [jnp.where(incl, g1[h][2 * C:, C:], 0.0).astype(BF16) for h in heads]

    side = lambda xs: jnp.concatenate(xs, axis=1)

    def dot_hl(a, b_hi, b_lo):
        ah, al = _split(a)
        return _dot(side([ah, al, ah]), rows([b_hi, b_hi, b_lo]))

    pw = a_ab
    tinv = [eye + p for p in pw]
    n = 2
    while n < C:
        pw_b = [p.astype(BF16) for p in pw]
        pw = [_dot(p, p) for p in pw_b]
        tinv = [t + _dot(t.astype(BF16), p.astype(BF16)) for t, p in zip(tinv, pw)]
        n *= 2
    resid = [(eye - t) + dot_hl(a, *_split(t)) for a, t in zip(a_ab, tinv)]
    tinv = [t + _dot(t.astype(BF16), r_.astype(BF16)) for t, r_ in zip(tinv, resid)]

    x_s = [_dot_nt(rows([at_hi[h], at_lo[h]]), s_hi[h]) for h in heads]
    x = [x_s[h][:C] + x_s[h][C:] + _dot_nt(at_hi[h], s_lo[h]) + dot_hl(a_ak[h], v_hi[h], v_lo[h]) for h in heads]
    u = [dot_hl(tinv[h], *_split(x[h])) for h in heads]
    u_hi, u_lo = zip(*[_split(u_) for u_ in u])
    ys = [_dot_nt(rt_b[h], s_hi[h]) + _dot(side([a_rb[h], a_rk[h]]), rows([u_hi[h], v_hi[h]])) for h in heads]
    for h in heads:
        upd = _dot_tn(rows([u_hi[h], v_hi[h], u_lo[h], v_lo[h], u_hi[h], v_hi[h]]),
                      rows([bh_hi[h], kh_hi[h], bh_hi[h], kh_hi[h], bh_lo[h], kh_lo[h]]))
        s_scr[h] = s_all[h] * wc_h[h] + upd

    y = jnp.concatenate(ys, axis=-1)
    inv_n = 1.0 / RWKV_HEAD
    mean = head_sum(y) * inv_n
    d = y - mean
    var = head_sum(d * d) * inv_n
    yn = d * lax.rsqrt(var + LN_X_EPS) * lng_ref[...] + lnb_ref[...]
    o_ref[...] = ((yn + bonus) * g).astype(o_ref.dtype)

    @pl.when(c == pl.num_programs(1) - 1)
    def _():
        sout_ref[...] = s_scr[...]


def _rwkv(z, shift0, s0, wts, C):
    B, L, _ = z.shape
    fixed = lambda b, c: (0, 0)
    full = lambda a: pl.BlockSpec(a.shape, fixed)
    names = ["mu", "w0", "w2", "a0", "a2", "g2", "k_k", "k_a", "r_k", "lnx_g", "lnx_b", "ones_bd"]
    return pl.pallas_call(
        functools.partial(_rwkv_kernel, C=C),
        grid=(B, L // C),
        in_specs=[pl.BlockSpec((None, C, RWKV_COLS), lambda b, c: (b, c, 0)),
                  pl.BlockSpec((None, 1, RWKV_COLS), lambda b, c: (b, 0, 0)),
                  pl.BlockSpec((None, RWKV_HEADS, RWKV_HEAD, RWKV_HEAD), lambda b, c: (b, 0, 0, 0))]
                 + [full(wts[n]) for n in names],
        out_specs=[pl.BlockSpec((None, C, RWKV_DIM), lambda b, c: (b, c, 0)),
                   pl.BlockSpec((None, RWKV_HEADS, RWKV_HEAD, RWKV_HEAD), lambda b, c: (b, 0, 0, 0))],
        out_shape=[jax.ShapeDtypeStruct((B, L, RWKV_DIM), BF16),
                   jax.ShapeDtypeStruct((B, RWKV_HEADS, RWKV_HEAD, RWKV_HEAD), F32)],
        scratch_shapes=[pltpu.VMEM((RWKV_HEADS, RWKV_HEAD, RWKV_HEAD), F32), pltpu.VMEM((1, RWKV_COLS), F32)],
        compiler_params=_cparams(("arbitrary", "arbitrary")),
        name="rwkv",
    )(z, shift0, s0, *[wts[n] for n in names])


def _merge_kernel(x_ref, oa_ref, ob_ref, gl_ref, woa_ref, wob_ref, fg_ref, wrh_ref, wrl_ref, br_ref,
                  x1_ref, h2_ref, ri_ref, rw_ref, cnt_ref, run_scr, *, tm):
    @pl.when(pl.program_id(0) == 0)
    def _():
        run_scr[...] = jnp.zeros_like(run_scr)

    pa = _dot(oa_ref[...], woa_ref[...])
    pb = _dot(ob_ref[...], wob_ref[...])
    x1 = x_ref[...] + (_sigmoid(gl_ref[:, :D_MODEL]) * pa + _sigmoid(gl_ref[:, D_MODEL:]) * pb)
    x1_ref[...] = x1
    h2 = x1 * _rms(x1) * fg_ref[...]
    _store_token_tiles(h2_ref, h2)
    hh, hl = _split(h2)
    lg = _dot(hh, wrh_ref[...]) + (_dot(hl, wrh_ref[...]) + _dot(hh, wrl_ref[...])) + br_ref[...]

    lane = lax.broadcasted_iota(I32, (tm, LANES), 1)
    lane_f = lane.astype(F32)
    first = lambda hit: jnp.min(jnp.where(hit, lane_f, float(LANES)), axis=-1, keepdims=True).astype(I32)
    is_group = jnp.logical_and(lane >= ROUTE_GROUP_LANE, lane < ROUTE_GROUP_LANE + N_GROUPS)
    gl = jnp.where(is_group, lg, MASK_VALUE)
    gmax = jnp.max(gl, axis=-1, keepdims=True)
    gidx = first(gl == gmax) - ROUTE_GROUP_LANE
    g_top = 1.0 / jnp.sum(jnp.where(is_group, jnp.exp(gl - gmax), 0.0), axis=-1, keepdims=True)
    in_group = jnp.logical_and(lane < N_EXPERTS, jnp.right_shift(lane, 3) == gidx)
    el = jnp.where(in_group, lg, MASK_VALUE)
    e1 = jnp.max(el, axis=-1, keepdims=True)
    i1 = first(el == e1)
    el2 = jnp.where(lane == i1, MASK_VALUE, el)
    e2 = jnp.max(el2, axis=-1, keepdims=True)
    i2 = first(el2 == e2)
    t = jnp.exp(e2 - e1)
    w1 = g_top / (1.0 + t)
    w2 = g_top * t / (1.0 + t)

    hit1 = lane == i1
    hit2 = lane == i2
    oh = jnp.where(jnp.logical_or(hit1, hit2), 1.0, 0.0)
    ti = lax.broadcasted_iota(I32, (tm, tm), 0)
    tj = lax.broadcasted_iota(I32, (tm, tm), 1)
    before = _dot(jnp.where(ti > tj, 1.0, 0.0).astype(BF16), oh.astype(BF16)) + run_scr[...]
    rank1 = jnp.sum(jnp.where(hit1, before, 0.0), axis=-1, keepdims=True)
    rank2 = jnp.sum(jnp.where(hit2, before, 0.0), axis=-1, keepdims=True)
    run = run_scr[...] + jnp.sum(oh, axis=0, keepdims=True)
    run_scr[...] = run
    cnt_ref[...] = run.astype(I32)

    sel = lambda vals: functools.reduce(lambda acc, kv: jnp.where(lane == kv[0], kv[1], acc), enumerate(vals), 0)
    ri_ref[...] = sel([i1, i2, rank1.astype(I32), rank2.astype(I32)])
    rw_ref[...] = sel([w1, w2])


def _merge(x, oa, ob, gl, wts, tm):
    T = x.shape[0]
    row = lambda i: (i, 0)
    full = lambda a: pl.BlockSpec(a.shape, lambda i: (0,) * a.ndim)
    names = ["w_out_a", "w_out_b", "ffn_g", "wr_hi", "wr_lo", "b_r"]
    return pl.pallas_call(
        functools.partial(_merge_kernel, tm=tm),
        grid=(T // tm,),
        in_specs=[pl.BlockSpec((tm, D_MODEL), row), pl.BlockSpec((tm, MLA_HEADS * HEAD_PAD), row),
                  pl.BlockSpec((tm, RWKV_DIM), row), pl.BlockSpec((tm, GATE_COLS), row)]
                 + [full(wts[n]) for n in names],
        out_specs=[pl.BlockSpec((tm, D_MODEL), row), pl.BlockSpec((tm * ROW_TILE, LANES), row),
                   pl.BlockSpec((tm, LANES), row), pl.BlockSpec((tm, LANES), row),
                   pl.BlockSpec((1, LANES), lambda i: (0, 0))],
        out_shape=[jax.ShapeDtypeStruct((T, D_MODEL), F32), jax.ShapeDtypeStruct((T * ROW_TILE, LANES), F32),
                   jax.ShapeDtypeStruct((T, LANES), I32), jax.ShapeDtypeStruct((T, LANES), F32),
                   jax.ShapeDtypeStruct((1, LANES), I32)],
        scratch_shapes=[pltpu.VMEM((1, LANES), F32)],
        compiler_params=_cparams(("arbitrary",)),
        name="merge",
    )(x, oa, ob, gl, *[wts[n] for n in names])


def _dispatch_kernel(pend_ref, cnt_ref, dest_ref, h_ref, xs_ref, zero_scr, sems, *, tm):
    i = pl.program_id(0)
    n = pl.num_programs(0)
    tile = lambda ref, slot, rows=1: ref.at[pl.ds(slot * ROW_TILE, rows * ROW_TILE), :]
    step_bytes = lambda sem: pltpu.make_async_copy(tile(h_ref, 0, tm), tile(xs_ref, 0, tm), sem)

    @pl.when(i == 0)
    def _():
        zero_scr[...] = jnp.zeros_like(zero_scr)
        fill = lambda e: pltpu.make_async_copy(zero_scr, tile(xs_ref, pend_ref[e] - FFN_ROWS, FFN_ROWS), sems.at[1])
        for e in range(N_EXPERTS):
            pl.when(cnt_ref[e] > 0)(lambda e=e: fill(e).start())
        for e in range(N_EXPERTS):
            pl.when(cnt_ref[e] > 0)(lambda e=e: fill(e).wait())
        dead = lambda b: pltpu.make_async_copy(zero_scr, tile(xs_ref, b * FFN_ROWS, FFN_ROWS), sems.at[1])
        first_dead = pend_ref[N_EXPERTS - 1] // FFN_ROWS
        n_blocks = xs_ref.shape[0] // (FFN_ROWS * ROW_TILE)
        lax.fori_loop(first_dead, n_blocks, lambda b, c: (dead(b).start(), c)[1], 0)
        lax.fori_loop(first_dead, n_blocks, lambda b, c: (dead(b).wait(), c)[1], 0)

    sem = sems.at[i % 2]

    def issue(t, carry):
        for s in range(TOP_K):
            d = dest_ref[0, 0, TOP_K * t + s]
            pltpu.make_async_copy(tile(h_ref, i * tm + t), tile(xs_ref, d), sem).start()
        return carry

    lax.fori_loop(0, tm, issue, 0, unroll=DMA_ISSUE_UNROLL)

    @pl.when(i > 0)
    def _():
        for s in range(TOP_K):
            step_bytes(sems.at[(i + 1) % 2]).wait()

    @pl.when(i == n - 1)
    def _():
        for s in range(TOP_K):
            step_bytes(sem).wait()


def _dispatch(h2t, dest, pend, counts, n_slots, tm):
    T = h2t.shape[0] // ROW_TILE
    dest3 = dest.reshape(T // tm, 1, TOP_K * tm)
    grid_spec = pltpu.PrefetchScalarGridSpec(
        num_scalar_prefetch=2,
        grid=(T // tm,),
        in_specs=[pl.BlockSpec((1, 1, TOP_K * tm), lambda i, pe, cn: (i, 0, 0), memory_space=pltpu.SMEM),
                  pl.BlockSpec(memory_space=pl.ANY)],
        out_specs=pl.BlockSpec(memory_space=pl.ANY),
        scratch_shapes=[pltpu.VMEM((FFN_ROWS * ROW_TILE, LANES), F32), pltpu.SemaphoreType.DMA((2,))],
    )
    return pl.pallas_call(
        functools.partial(_dispatch_kernel, tm=tm),
        grid_spec=grid_spec,
        out_shape=jax.ShapeDtypeStruct((n_slots * ROW_TILE, LANES), F32),
        compiler_params=_cparams(("arbitrary",)),
        name="dispatch",
    )(pend, counts, dest3, h2t)


def _ffn_kernel(be_ref, nv_ref, x_ref, wg_ref, wu_ref, wd_ref, y_ref):
    del be_ref
    live = pl.program_id(0) < nv_ref[0]

    @pl.when(live)
    def _():
        x = _load_token_tiles(x_ref, FFN_ROWS).astype(BF16)
        gate = _dot(x, wg_ref[...])
        up = _dot(x, wu_ref[...])
        _store_token_tiles(y_ref, _dot((gate * _sigmoid(gate) * up).astype(BF16), wd_ref[...]))

    @pl.when(jnp.logical_not(live))
    def _():
        y_ref[...] = jnp.zeros_like(y_ref)


def _ffn(xs, block_e, n_live, wts):
    nb = xs.shape[0] // (FFN_ROWS * ROW_TILE)
    live_block = lambda i, be, nv: (jnp.minimum(i, nv[0] - 1), 0)
    grid_spec = pltpu.PrefetchScalarGridSpec(
        num_scalar_prefetch=2,
        grid=(nb,),
        in_specs=[pl.BlockSpec((FFN_ROWS * ROW_TILE, LANES), live_block),
                  pl.BlockSpec((None, D_MODEL, EXPERT_FF), lambda i, be, nv: (be[i], 0, 0)),
                  pl.BlockSpec((None, D_MODEL, EXPERT_FF), lambda i, be, nv: (be[i], 0, 0)),
                  pl.BlockSpec((None, EXPERT_FF, D_MODEL), lambda i, be, nv: (be[i], 0, 0))],
        out_specs=pl.BlockSpec((FFN_ROWS * ROW_TILE, LANES), lambda i, be, nv: (i, 0)),
    )
    return pl.pallas_call(
        _ffn_kernel,
        grid_spec=grid_spec,
        out_shape=jax.ShapeDtypeStruct(xs.shape, F32),
        compiler_params=_cparams(("arbitrary",)),
        name="ffn",
    )(block_e, n_live, xs, wts["w_eg"], wts["w_eu"], wts["w_ed"])


def _combine_kernel(dest_ref, dest_next_ref, x1_ref, rw_ref, yb_ref, o_ref, y_scr, sems, *, tm):
    i = pl.program_id(0)
    n = pl.num_programs(0)
    slot = i % 2
    tile = lambda ref, t, rows=1: ref.at[pl.ds(t * ROW_TILE, rows * ROW_TILE), :]

    def gather(dref, buf):
        def issue(t, carry):
            for s in range(TOP_K):
                d = dref[0, 0, TOP_K * t + s]
                pltpu.make_async_copy(tile(yb_ref, d), tile(y_scr.at[buf, s], t), sems.at[buf]).start()
            return carry

        lax.fori_loop(0, tm, issue, 0, unroll=DMA_ISSUE_UNROLL)

    pl.when(i == 0)(lambda: gather(dest_ref, 0))
    pl.when(i + 1 < n)(lambda: gather(dest_next_ref, 1 - slot))
    for s in range(TOP_K):
        pltpu.make_async_copy(tile(yb_ref, 0, tm), y_scr.at[slot, s], sems.at[slot]).wait()
    rw = rw_ref[...]
    y = [_load_token_tiles(y_scr.at[slot, s], tm) for s in range(TOP_K)]
    o_ref[...] = x1_ref[...] + (rw[:, 0:1] * y[0] + rw[:, 1:2] * y[1])


def _combine(x1, rw, yb, dest, tm):
    T = x1.shape[0]
    nt = T // tm
    dest3 = dest.reshape(nt, 1, TOP_K * tm)
    dest_spec = lambda step: pl.BlockSpec((1, 1, TOP_K * tm), lambda i: (step(i), 0, 0), memory_space=pltpu.SMEM)
    return pl.pallas_call(
        functools.partial(_combine_kernel, tm=tm),
        grid=(nt,),
        in_specs=[dest_spec(lambda i: i), dest_spec(lambda i: jnp.minimum(i + 1, nt - 1)),
                  pl.BlockSpec((tm, D_MODEL), lambda i: (i, 0)),
                  pl.BlockSpec((tm, LANES), lambda i: (i, 0)),
                  pl.BlockSpec(memory_space=pl.ANY)],
        out_specs=pl.BlockSpec((tm, D_MODEL), lambda i: (i, 0)),
        out_shape=jax.ShapeDtypeStruct((T, D_MODEL), F32),
        scratch_shapes=[pltpu.VMEM((2, TOP_K, tm * ROW_TILE, LANES), F32), pltpu.SemaphoreType.DMA((2,))],
        compiler_params=_cparams(("arbitrary",)),
        name="combine",
    )(dest3, dest3, x1, rw, yb)


def _moe(x1, h2, ri, rw, cnt, wts, tm):
    T = x1.shape[0]
    ids = ri[:, 0:TOP_K]
    ranks = ri[:, TOP_K:2 * TOP_K]
    counts = cnt[0, :N_EXPERTS]
    padded = (counts + FFN_ROWS - 1) // FFN_ROWS * FFN_ROWS
    pend = jnp.cumsum(padded)
    pstart = pend - padded
    dest = (pstart[ids] + ranks).astype(I32)
    nb = -(-T * TOP_K // FFN_ROWS) + N_EXPERTS
    block_pos = jnp.arange(nb, dtype=I32) * FFN_ROWS
    block_e = jnp.minimum(jnp.sum(pend[None, :] <= block_pos[:, None], axis=1), N_EXPERTS - 1).astype(I32)
    n_live = (pend[-1:] // FFN_ROWS).astype(I32)
    xs = _dispatch(h2, dest, pend.astype(I32), counts.astype(I32), nb * FFN_ROWS, tm)
    yb = _ffn(xs, block_e, n_live, wts)
    return _combine(x1, rw, yb, dest, tm)


def _rope_tables(pos):
    inv = 1.0 / (ROPE_BASE ** (jnp.arange(HALF_ROPE, dtype=F32) * (2.0 / QK_ROPE)))
    ang = pos.astype(F32)[:, None] * inv[None, :]
    cos, sin = jnp.cos(ang), jnp.sin(ang)
    n = pos.shape[0]
    pad = jnp.zeros((n, HEAD_PAD - QK_DIM), F32)
    ct = jnp.concatenate([jnp.ones((n, QK_NOPE), F32), cos, cos, pad], axis=1)
    st = jnp.concatenate([jnp.zeros((n, QK_NOPE), F32), -sin, sin, pad], axis=1)
    kt = jnp.concatenate([cos, sin], axis=1)
    return ct, st, kt


def _score_bound(q_head_norm_g, k_head_norm_g):
    return (1.02 * QK_DIM * ATTN_SCALE * math.log2(math.e)) * (
        jnp.max(jnp.abs(q_head_norm_g)) * jnp.max(jnp.abs(k_head_norm_g)))


def _prep_weights(norm_mix_g, w_in, q_norm_g, w_uq, kv_norm_g, w_ukv, q_head_norm_g, k_head_norm_g,
                  rwkv_mu, rwkv_w0, rwkv_w2, rwkv_a0, rwkv_a2, rwkv_g2, rwkv_k_k, rwkv_k_a, rwkv_r_k,
                  rwkv_lnx_g, rwkv_lnx_b, w_out, norm_ffn_g, w_router_group, b_router_group,
                  w_router_expert, b_router_expert, w_expert_gate, w_expert_up, w_expert_down):
    row = lambda a: a.reshape(1, -1).astype(F32)
    head_pad = lambda a: jnp.pad(a, ((0, 0), (0, 0), (0, HEAD_PAD - a.shape[-1])))
    w = {}
    w["mix_g"] = row(norm_mix_g)
    w["w_in"] = jnp.concatenate([w_in[:, :MLA_COLS], jnp.zeros((D_MODEL, LANES - QK_ROPE), F32),
                                 w_in[:, MLA_COLS:]], axis=1).astype(BF16)
    w["q_g"] = row(q_norm_g)
    w["kv_g"] = row(kv_norm_g)
    w["wqa"] = head_pad(w_uq).reshape(Q_LORA, MLA_HEADS * HEAD_PAD).astype(BF16)
    partner = jnp.concatenate([jnp.zeros_like(w_uq[..., :QK_NOPE]), w_uq[..., QK_NOPE + HALF_ROPE:],
                               w_uq[..., QK_NOPE:QK_NOPE + HALF_ROPE]], axis=-1)
    w["wqb"] = head_pad(partner).reshape(Q_LORA, MLA_HEADS * HEAD_PAD).astype(BF16)
    pad_g = lambda g: jnp.pad(g, (0, HEAD_PAD - QK_DIM)).reshape(1, HEAD_PAD).astype(F32)
    w["qh_g"] = pad_g(q_head_norm_g) * (ATTN_SCALE * math.log2(math.e))
    w["kh_g"] = pad_g(k_head_norm_g)
    lane_one = lambda lane: (jnp.arange(HEAD_PAD) == lane).astype(F32).reshape(1, HEAD_PAD)
    w["score_bound"] = _score_bound(q_head_norm_g, k_head_norm_g)
    w["q_one"] = lane_one(QK_DIM)
    w["k_shift"] = -w["score_bound"] * lane_one(QK_DIM)
    w["v_one"] = jnp.tile(lane_one(V_DIM), (1, MLA_HEADS))
    w["wk"] = head_pad(w_ukv[..., :QK_NOPE]).reshape(KV_LORA, MLA_HEADS * HEAD_PAD).astype(BF16)
    w["wv"] = head_pad(w_ukv[..., QK_NOPE:]).reshape(KV_LORA, MLA_HEADS * HEAD_PAD).astype(BF16)
    w["place"] = jnp.pad(jnp.eye(QK_ROPE, dtype=F32), ((0, 0), (QK_NOPE, HEAD_PAD - QK_DIM))).astype(BF16)
    w["mu"] = row(rwkv_mu)
    w["w0"] = row(rwkv_w0)
    w["w2"] = jnp.concatenate([rwkv_w2, jnp.zeros((A_LORA, RWKV_DIM), F32)], axis=0).astype(BF16)
    w["a0"] = row(rwkv_a0)
    w["a2"] = jnp.concatenate([jnp.zeros((W_LORA, RWKV_DIM), F32), rwkv_a2], axis=0).astype(BF16)
    w["g2"] = rwkv_g2.astype(BF16)
    w["k_k"] = row(rwkv_k_k)
    w["k_a"] = row(rwkv_k_a)
    w["r_k"] = row(rwkv_r_k)
    w["lnx_g"] = row(rwkv_lnx_g)
    w["lnx_b"] = row(rwkv_lnx_b)
    head_of = jnp.arange(RWKV_DIM) // RWKV_HEAD
    w["ones_bd"] = (head_of[:, None] == head_of[None, :]).astype(BF16)
    w["w_out_a"] = jnp.pad(w_out[0].reshape(MLA_HEADS, V_DIM, D_MODEL), ((0, 0), (0, HEAD_PAD - V_DIM), (0, 0))
                           ).reshape(MLA_HEADS * HEAD_PAD, D_MODEL).astype(BF16)
    w["w_out_b"] = w_out[1].astype(BF16)
    w["ffn_g"] = row(norm_ffn_g)
    wr = jnp.concatenate([w_router_expert.reshape(D_MODEL, N_EXPERTS), w_router_group,
                          jnp.zeros((D_MODEL, LANES - N_EXPERTS - N_GROUPS), F32)], axis=1)
    w["wr_hi"] = wr.astype(BF16)
    w["wr_lo"] = (wr - w["wr_hi"].astype(F32)).astype(BF16)
    w["b_r"] = jnp.concatenate([b_router_expert.reshape(-1), b_router_group,
                                jnp.zeros((LANES - N_EXPERTS - N_GROUPS,), F32)]).reshape(1, LANES)
    w["w_eg"] = w_expert_gate.astype(BF16)
    w["w_eu"] = w_expert_up.astype(BF16)
    w["w_ed"] = w_expert_down.astype(BF16)
    return w


def _round_up(n, m):
    return -(-n // m) * m


def _layer(x, pos, lat_past, kpe_past, wkv0, shift0, wts):
    B, L, _ = x.shape
    T = B * L
    tm = min(256, T)
    ct, st, kt = _rope_tables(pos)
    if L % tm:
        reps = tm // L
        ct, st, kt = (jnp.tile(t, (reps, 1)) for t in (ct, st, kt))
    q, lat, kpe, z, gl = _inproj(x.reshape(T, D_MODEL), (ct, st, kt), wts, tm)
    lat3 = lat.reshape(B, L, KV_LORA)
    kpe3 = kpe.reshape(B, L, QK_ROPE)
    if lat_past is None:
        n_past, lk, lat_all, kpe_all = 0, L, lat3, kpe3
        tq = tk = min(ATTN_TILE, L)
        diag = True
    else:
        n_past = lat_past.shape[1]
        lk_valid = n_past + L
        lk = _round_up(lk_valid, LANES)
        padk = lambda a: jnp.pad(a, ((0, 0), (0, lk - lk_valid), (0, 0)))
        lat_all = padk(jnp.concatenate([lat_past, lat3], axis=1))
        kpe_all = padk(jnp.concatenate([kpe_past, kpe3], axis=1))
        tq, tk, diag = L, lk, False
    k, v = _kvprep(lat_all.reshape(B * lk, KV_LORA), kpe_all.reshape(B * lk, QK_ROPE), wts, min(512, B * lk))
    oa = _attention(q.reshape(B, L, -1), k.reshape(B, lk, -1), v.reshape(B, lk, -1),
                    wts["score_bound"] <= MAX_DIRECT_EXP2_SHIFT, tq=tq, tk=tk, n_past=n_past,
                    lk_valid=n_past + L, diag_tiles=diag)
    z3 = z.reshape(B, L, RWKV_COLS)
    ob, s_new = _rwkv(z3, shift0, wkv0, wts, min(SCAN_CHUNK, L))
    x1, h2, ri, rw, cnt = _merge(x.reshape(T, D_MODEL), oa.reshape(T, -1), ob.reshape(T, -1), gl, wts, tm)
    y = _moe(x1, h2, ri, rw, cnt, wts, tm)
    return y.reshape(B, L, D_MODEL), lat3, kpe3, s_new, z3[:, L - 1:, :]


def kernel(x_prompt, x_sample, cache_kv_latent, cache_k_rope, state_wkv, state_shift, norm_mix_g, w_in, q_norm_g, w_uq, kv_norm_g, w_ukv, q_head_norm_g, k_head_norm_g, rwkv_mu, rwkv_w0, rwkv_w2, rwkv_a0, rwkv_a2, rwkv_g2, rwkv_k_k, rwkv_k_a, rwkv_r_k, rwkv_lnx_g, rwkv_lnx_b, w_out, norm_ffn_g, w_router_group, b_router_group, w_router_expert, b_router_expert, w_expert_gate, w_expert_up, w_expert_down):
    weights = (norm_mix_g, w_in, q_norm_g, w_uq, kv_norm_g, w_ukv, q_head_norm_g, k_head_norm_g,
               rwkv_mu, rwkv_w0, rwkv_w2, rwkv_a0, rwkv_a2, rwkv_g2, rwkv_k_k, rwkv_k_a, rwkv_r_k,
               rwkv_lnx_g, rwkv_lnx_b, w_out, norm_ffn_g, w_router_group, b_router_group,
               w_router_expert, b_router_expert, w_expert_gate, w_expert_up, w_expert_down)
    depth = w_in.shape[0]
    bp, lp, _ = x_prompt.shape
    n_past = cache_kv_latent.shape[2]
    pos_p = jnp.arange(lp, dtype=I32)
    pos_s = n_past + jnp.arange(x_sample.shape[1], dtype=I32)
    wkv_zero = jnp.zeros((bp, RWKV_HEADS, RWKV_HEAD, RWKV_HEAD), F32)
    shift_zero = jnp.zeros((bp, 1, RWKV_COLS), F32)
    xp, xs = x_prompt, x_sample
    outs_p, outs_s = [], []
    for l in range(depth):
        wts = _prep_weights(*(wt[l] for wt in weights))
        xp, *rest_p = _layer(xp, pos_p, None, None, wkv_zero, shift_zero, wts)
        xs, *rest_s = _layer(xs, pos_s, cache_kv_latent[l], cache_k_rope[l], state_wkv[l], state_shift[l], wts)
        outs_p.append(rest_p)
        outs_s.append(rest_s)
    stack = lambda outs, i: jnp.stack([o[i] for o in outs], 0)
    return (xp, xs, stack(outs_p, 0), stack(outs_p, 1), stack(outs_p, 2), stack(outs_p, 3),
            stack(outs_s, 0), stack(outs_s, 1), stack(outs_s, 2), stack(outs_s, 3))
```

```python
import functools
import math

import jax
import jax.numpy as jnp
from jax import lax
from jax.experimental import pallas as pl
from jax.experimental.pallas import tpu as pltpu

F32 = jnp.float32
BF16 = jnp.bfloat16
I32 = jnp.int32

D_MODEL = 1024
CHUNK = 64
RMS_EPS = 1e-6
MASK_VALUE = -1e30
MLA_HEADS = 8
QK_NOPE = 64
QK_ROPE = 32
QK_DIM = QK_NOPE + QK_ROPE
V_DIM = 64
Q_LORA = 256
KV_LORA = 128
ROPE_BASE = 10000.0
ATTN_SCALE = QK_DIM ** -0.5
RWKV_HEADS = 8
RWKV_HEAD = 64
RWKV_DIM = RWKV_HEADS * RWKV_HEAD
W_LORA = 64
A_LORA = 64
G_LORA = 128
LN_X_EPS = 64e-5
MLA_COLS = Q_LORA + KV_LORA + QK_ROPE
RWKV_COLS = 3 * RWKV_DIM + W_LORA + A_LORA + G_LORA
GATE_COLS = 2 * D_MODEL
N_GROUPS = 4
EXPERTS_PER_GROUP = 8
N_EXPERTS = N_GROUPS * EXPERTS_PER_GROUP
EXPERT_FF = 256
TOP_K = 2

LANES = 128
ROW_TILE = D_MODEL // LANES
DMA_ISSUE_UNROLL = 4
DISPATCH_TOKENS = 512
HEAD_PAD = LANES
HALF_ROPE = QK_ROPE // 2
SEG_Q = 0
SEG_KV = SEG_Q + Q_LORA
SEG_PE = SEG_KV + KV_LORA
SEG_Z = SEG_PE + LANES
SEG_G = SEG_Z + RWKV_COLS
IN_COLS_PAD = SEG_G + GATE_COLS
ROUTE_GROUP_LANE = N_EXPERTS
SCAN_CHUNK = 128
FFN_ROWS = 256
ATTN_TILE = 512
MAX_DIRECT_EXP2_SHIFT = 40.0
VMEM_LIMIT = 48 * 1024 * 1024


def _cparams(sem, vmem=VMEM_LIMIT):
    return pltpu.CompilerParams(dimension_semantics=sem, vmem_limit_bytes=vmem)


def _dot(a, b):
    return jnp.dot(a, b, preferred_element_type=F32)


def _dot_nt(a, b):
    return lax.dot_general(a, b, (((1,), (1,)), ((), ())), preferred_element_type=F32)


def _dot_tn(a, b):
    return lax.dot_general(a, b, (((0,), (0,)), ((), ())), preferred_element_type=F32)


def _split(x):
    hi = x.astype(BF16)
    lo = (x - hi.astype(F32)).astype(BF16)
    return hi, lo


def _dot2_exact_rhs(a, b_bf16):
    ah, al = _split(a)
    return _dot(ah, b_bf16) + _dot(al, b_bf16)


def _sigmoid(x):
    return 1.0 / (1.0 + jnp.exp(-x))


def _rms(x, n=None):
    n = x.shape[-1] if n is None else n
    return lax.rsqrt(jnp.sum(x * x, axis=-1, keepdims=True) * (1.0 / n) + RMS_EPS)


def _store_token_tiles(ref, x):
    n = x.shape[0]
    for c in range(ROW_TILE):
        ref[pl.ds(c, n, stride=ROW_TILE), :] = x[:, c * LANES:(c + 1) * LANES]


def _load_token_tiles(ref, n):
    return jnp.concatenate([ref[pl.ds(c, n, stride=ROW_TILE), :] for c in range(ROW_TILE)], axis=1)


def _inproj_kernel(x_ref, g_ref, w_ref, qg_ref, wqa_ref, wqb_ref, ct_ref, st_ref, kt_ref, kvg_ref, qhg_ref, qone_ref,
                   q_ref, lat_ref, kpe_ref, z_ref, gl_ref):
    x = x_ref[...]
    h = (x * _rms(x) * g_ref[...]).astype(BF16)
    cq = _dot(h, w_ref[:, SEG_Q:SEG_KV])
    ckv = _dot(h, w_ref[:, SEG_KV:SEG_PE])
    pe = _dot(h, w_ref[:, SEG_PE:SEG_Z])
    z_ref[...] = _dot(h, w_ref[:, SEG_Z:SEG_G])
    gl_ref[...] = _dot(h, w_ref[:, SEG_G:IN_COLS_PAD]).astype(gl_ref.dtype)
    lat_ref[...] = ckv * _rms(ckv) * kvg_ref[...]
    x1 = pe[:, :HALF_ROPE]
    x2 = pe[:, HALF_ROPE:QK_ROPE]
    c = kt_ref[:, :HALF_ROPE]
    s = kt_ref[:, HALF_ROPE:QK_ROPE]
    kpe_ref[...] = jnp.concatenate([x1 * c - x2 * s, x1 * s + x2 * c], axis=-1)
    cqn = (cq * _rms(cq) * qg_ref[...]).astype(BF16)
    ct = ct_ref[...]
    st = st_ref[...]
    qhg = qhg_ref[...]
    qone = qone_ref[...]
    for hd in range(MLA_HEADS):
        sl = slice(hd * HEAD_PAD, (hd + 1) * HEAD_PAD)
        xh = _dot(cqn, wqa_ref[:, sl]) * ct + _dot(cqn, wqb_ref[:, sl]) * st
        q_ref[:, sl] = (xh * _rms(xh, QK_DIM) * qhg + qone).astype(BF16)


def _inproj(x, tabs, wts, tm):
    T = x.shape[0]
    ct, st, kt = tabs
    nt = ct.shape[0] // tm
    row = lambda i: (i, 0)
    tab = lambda i: (i % nt, 0)
    fixed = lambda i: (0, 0)
    full = lambda a: pl.BlockSpec(a.shape, fixed)
    return pl.pallas_call(
        _inproj_kernel,
        grid=(T // tm,),
        in_specs=[pl.BlockSpec((tm, D_MODEL), row), full(wts["mix_g"]), full(wts["w_in"]), full(wts["q_g"]),
                  full(wts["wqa"]), full(wts["wqb"]),
                  pl.BlockSpec((tm, HEAD_PAD), tab), pl.BlockSpec((tm, HEAD_PAD), tab),
                  pl.BlockSpec((tm, QK_ROPE), tab), full(wts["kv_g"]), full(wts["qh_g"]), full(wts["q_one"])],
        out_specs=[pl.BlockSpec((tm, MLA_HEADS * HEAD_PAD), row), pl.BlockSpec((tm, KV_LORA), row),
                   pl.BlockSpec((tm, QK_ROPE), row), pl.BlockSpec((tm, RWKV_COLS), row),
                   pl.BlockSpec((tm, GATE_COLS), row)],
        out_shape=[jax.ShapeDtypeStruct((T, MLA_HEADS * HEAD_PAD), BF16), jax.ShapeDtypeStruct((T, KV_LORA), F32),
                   jax.ShapeDtypeStruct((T, QK_ROPE), F32), jax.ShapeDtypeStruct((T, RWKV_COLS), F32),
                   jax.ShapeDtypeStruct((T, GATE_COLS), BF16)],
        compiler_params=_cparams(("arbitrary",)),
        name="inproj",
    )(x, wts["mix_g"], wts["w_in"], wts["q_g"], wts["wqa"], wts["wqb"], ct, st, kt, wts["kv_g"], wts["qh_g"],
      wts["q_one"])


def _kvprep_kernel(lat_ref, kpe_ref, wk_ref, wv_ref, place_ref, kg_ref, kshift_ref, vone_ref, k_ref, v_ref):
    lat = lat_ref[...].astype(BF16)
    rot = _dot2_exact_rhs(kpe_ref[...], place_ref[...])
    kg = kg_ref[...]
    kshift = kshift_ref[...]
    for hd in range(MLA_HEADS):
        sl = slice(hd * HEAD_PAD, (hd + 1) * HEAD_PAD)
        kh = _dot(lat, wk_ref[:, sl]) + rot
        k_ref[:, sl] = (kh * _rms(kh, QK_DIM) * kg + kshift).astype(BF16)
    v_ref[...] = (_dot(lat, wv_ref[...]) + vone_ref[...]).astype(BF16)


def _kvprep(lat, kpe, wts, tm):
    T = lat.shape[0]
    row = lambda i: (i, 0)
    fixed = lambda i: (0, 0)
    full = lambda a: pl.BlockSpec(a.shape, fixed)
    return pl.pallas_call(
        _kvprep_kernel,
        grid=(T // tm,),
        in_specs=[pl.BlockSpec((tm, KV_LORA), row), pl.BlockSpec((tm, QK_ROPE), row), full(wts["wk"]),
                  full(wts["wv"]), full(wts["place"]), full(wts["kh_g"]), full(wts["k_shift"]),
                  full(wts["v_one"])],
        out_specs=[pl.BlockSpec((tm, MLA_HEADS * HEAD_PAD), row), pl.BlockSpec((tm, MLA_HEADS * HEAD_PAD), row)],
        out_shape=[jax.ShapeDtypeStruct((T, MLA_HEADS * HEAD_PAD), BF16),
                   jax.ShapeDtypeStruct((T, MLA_HEADS * HEAD_PAD), BF16)],
        compiler_params=_cparams(("arbitrary",)),
        name="kvprep",
    )(lat, kpe, wts["wk"], wts["wv"], wts["place"], wts["kh_g"], wts["k_shift"], wts["v_one"])


def _attn_kernel(q_ref, k_ref, v_ref, o_ref, *, tq, tk, n_past, lk_valid, diag_tiles, bounded):
    i = pl.program_id(2)
    n_full = i * (tq // tk) if diag_tiles else 0
    head_lanes = [slice(hh * HEAD_PAD, (hh + 1) * HEAD_PAD) for hh in range(2)]
    qs = [q_ref[:, hs] for hs in head_lanes]

    def scores(hh, j, masked):
        ks = pl.multiple_of(j * tk, tk)
        s = _dot_nt(qs[hh], k_ref[pl.ds(ks, tk), head_lanes[hh]])
        if masked:
            qpos = n_past + i * tq + lax.broadcasted_iota(I32, (tq, tk), 0)
            kpos = j * tk + lax.broadcasted_iota(I32, (tq, tk), 1)
            vis = jnp.logical_and(jnp.right_shift(kpos, 6) <= jnp.right_shift(qpos, 6), kpos < lk_valid)
            s = jnp.where(vis, s, MASK_VALUE)
        return s, v_ref[pl.ds(ks, tk), head_lanes[hh]]

    if bounded:
        def head_step(hh, j, acc, masked):
            s, v = scores(hh, j, masked)
            return acc + _dot(jnp.exp2(s).astype(BF16), v)

        init = jnp.zeros((tq, HEAD_PAD), F32)
    else:
        def head_step(hh, j, carry, masked):
            m, acc = carry
            s, v = scores(hh, j, masked)
            m_new = jnp.maximum(m, jnp.max(s, axis=-1, keepdims=True))
            return m_new, jnp.exp2(m - m_new) * acc + _dot(jnp.exp2(s - m_new).astype(BF16), v)

        init = (jnp.full((tq, 1), MASK_VALUE, F32), jnp.zeros((tq, HEAD_PAD), F32))

    step = lambda j, carry, masked: tuple(head_step(hh, j, carry[hh], masked) for hh in range(2))
    carry = (init, init)
    if diag_tiles:
        carry = lax.fori_loop(0, n_full, functools.partial(step, masked=False), carry)
    for d in range(tq // tk if diag_tiles else 1):
        carry = step(n_full + d, carry, True)
    for hh in range(2):
        acc = carry[hh] if bounded else carry[hh][1]
        o_ref[:, head_lanes[hh]] = (acc / acc[:, V_DIM:V_DIM + 1]).astype(o_ref.dtype)


def _attention(q, k, v, bound_ok, *, tq, tk, n_past, lk_valid, diag_tiles):
    B, L, _ = q.shape
    Lk = k.shape[1]

    def run(bounded):
        kern = functools.partial(_attn_kernel, tq=tq, tk=tk, n_past=n_past, lk_valid=lk_valid,
                                 diag_tiles=diag_tiles, bounded=bounded)
        return pl.pallas_call(
            kern,
            grid=(B, MLA_HEADS // 2, L // tq),
            in_specs=[pl.BlockSpec((None, tq, 2 * HEAD_PAD), lambda b, p, i: (b, i, p)),
                      pl.BlockSpec((None, Lk, 2 * HEAD_PAD), lambda b, p, i: (b, 0, p)),
                      pl.BlockSpec((None, Lk, 2 * HEAD_PAD), lambda b, p, i: (b, 0, p))],
            out_specs=pl.BlockSpec((None, tq, 2 * HEAD_PAD), lambda b, p, i: (b, i, p)),
            out_shape=jax.ShapeDtypeStruct((B, L, MLA_HEADS * HEAD_PAD), BF16),
            compiler_params=_cparams(("arbitrary", "arbitrary", "arbitrary")),
            name="attn_bounded" if bounded else "attn_running_max",
        )(q, k, v)

    return lax.cond(bound_ok, lambda: run(True), lambda: run(False))


def _rwkv_kernel(z_ref, shift_ref, s0_ref, mu_ref, w0_ref, w2_ref, a0_ref, a2_ref, g2_ref, kk_ref, ka_ref, rk_ref,
                 lng_ref, lnb_ref, ones_ref, o_ref, sout_ref, s_scr, prev_scr, *, C):
    c = pl.program_id(1)

    @pl.when(c == 0)
    def _():
        s_scr[...] = s0_ref[...]
        prev_scr[...] = shift_ref[...]

    zf = z_ref[...]
    row = lax.broadcasted_iota(I32, (C, 1), 0)
    pv = jnp.where(row == 0, prev_scr[...], pltpu.roll(zf, 1, axis=0))
    prev_scr[...] = zf[C - 1:C, :]
    zm = zf + (pv - zf) * mu_ref[...]
    H = RWKV_DIM
    r = zm[:, 0:H]
    k = zm[:, H:2 * H]
    v = zm[:, 2 * H:3 * H]
    xwa = zm[:, 3 * H:3 * H + W_LORA + A_LORA]
    xg = zm[:, 3 * H + W_LORA + A_LORA:]
    lw = (-math.exp(-0.5)) * _sigmoid(w0_ref[...] + _dot(jnp.tanh(xwa).astype(BF16), w2_ref[...]))
    a = _sigmoid(a0_ref[...] + _dot(xwa.astype(BF16), a2_ref[...]))
    g = _dot(_sigmoid(xg).astype(BF16), g2_ref[...])
    ones_bd = ones_ref[...]
    kk = k * kk_ref[...]
    kkn = kk * lax.rsqrt(jnp.maximum(_dot2_exact_rhs(kk * kk, ones_bd), 1e-24))
    k2 = k * (1.0 + (a - 1.0) * ka_ref[...])
    head_sum = lambda x: _dot(x.astype(BF16), ones_bd)
    bonus = head_sum(r * k2 * rk_ref[...]) * v

    ti = lax.broadcasted_iota(I32, (C, C), 0)
    tj = lax.broadcasted_iota(I32, (C, C), 1)
    incl = ti >= tj
    strict = ti > tj
    tri = jnp.where(incl, 1.0, 0.0).astype(BF16)
    p1 = lw.astype(BF16)
    r1 = lw - p1.astype(F32)
    p2 = r1.astype(BF16)
    p3 = (r1 - p2.astype(F32)).astype(BF16)
    cs = _dot(tri, p1) + (_dot(tri, p2) + _dot(tri, p3))
    w_incl = jnp.exp(cs)
    w_inv = jnp.exp(-cs)
    at = -kkn * jnp.exp(cs - lw)
    rt = r * w_incl
    bt = kkn * a * w_inv
    kt = k2 * w_inv
    wc = w_incl[C - 1:C, :]
    bh = bt * wc
    kh = kt * wc
    eye = jnp.where(ti == tj, 1.0, 0.0).astype(F32)

    heads = range(RWKV_HEADS)
    per_head = lambda x: [x[:, hd * RWKV_HEAD:(hd + 1) * RWKV_HEAD] for hd in heads]
    rows = lambda xs: jnp.concatenate(xs, axis=0)
    hi_lo = lambda x: tuple(per_head(p) for p in _split(x))
    (at_hi, at_lo), (bt_hi, bt_lo), (kt_hi, kt_lo) = hi_lo(at), hi_lo(bt), hi_lo(kt)
    (v_hi, v_lo), (bh_hi, bh_lo), (kh_hi, kh_lo) = hi_lo(v), hi_lo(bh), hi_lo(kh)
    rt_b = per_head(rt.astype(BF16))
    wc_h = per_head(wc)
    s_all = [s_scr[hd] for hd in heads]
    s_hi, s_lo = zip(*[_split(s) for s in s_all])

    g1 = [_dot_nt(rows([at_hi[h], at_lo[h], rt_b[h]]), rows([bt_hi[h], kt_hi[h]])) for h in heads]
    g2 = [_dot_nt(at_hi[h], rows([bt_lo[h], kt_lo[h]])) for h in heads]
    a_ab = [jnp.where(strict, g1[h][:C, :C] + g1[h][C:2 * C, :C] + g2[h][:, :C], 0.0) for h in heads]
    a_ak = [jnp.where(strict, g1[h][:C, C:] + g1[h][C:2 * C, C:] + g2[h][:, C:], 0.0) for h in heads]
    a_rb = [jnp.where(incl, g1[h][2 * C:, :C], 0.0).astype(BF16) for h in heads]
    a_rk = [jnp.where(incl, g1[h][2 * C:, C:], 0.0).astype(BF16) for h in heads]

    side = lambda xs: jnp.concatenate(xs, axis=1)

    def dot_hl(a, b_hi, b_lo):
        ah, al = _split(a)
        return _dot(side([ah, al, ah]), rows([b_hi, b_hi, b_lo]))

    pw = a_ab
    tinv = [eye + p for p in pw]
    n = 2
    while n < C:
        pw_b = [p.astype(BF16) for p in pw]
        pw = [_dot(p, p) for p in pw_b]
        tinv = [t + _dot(t.astype(BF16), p.astype(BF16)) for t, p in zip(tinv, pw)]
        n *= 2
    resid = [(eye - t) + dot_hl(a, *_split(t)) for a, t in zip(a_ab, tinv)]
    tinv = [t + _dot(t.astype(BF16), r_.astype(BF16)) for t, r_ in zip(tinv, resid)]

    x_s = [_dot_nt(rows([at_hi[h], at_lo[h]]), s_hi[h]) for h in heads]
    x = [x_s[h][:C] + x_s[h][C:] + _dot_nt(at_hi[h], s_lo[h]) + dot_hl(a_ak[h], v_hi[h], v_lo[h]) for h in heads]
    u = [dot_hl(tinv[h], *_split(x[h])) for h in heads]
    u_hi, u_lo = zip(*[_split(u_) for u_ in u])
    ys = [_dot_nt(rt_b[h], s_hi[h]) + _dot(side([a_rb[h], a_rk[h]]), rows([u_hi[h], v_hi[h]])) for h in heads]
    for h in heads:
        upd = _dot_tn(rows([u_hi[h], v_hi[h], u_lo[h], v_lo[h], u_hi[h], v_hi[h]]),
                      rows([bh_hi[h], kh_hi[h], bh_hi[h], kh_hi[h], bh_lo[h], kh_lo[h]]))
        s_scr[h] = s_all[h] * wc_h[h] + upd

    y = jnp.concatenate(ys, axis=-1)
    inv_n = 1.0 / RWKV_HEAD
    mean = head_sum(y) * inv_n
    d = y - mean
    var = head_sum(d * d) * inv_n
    yn = d * lax.rsqrt(var + LN_X_EPS) * lng_ref[...] + lnb_ref[...]
    o_ref[...] = ((yn + bonus) * g).astype(o_ref.dtype)

    @pl.when(c == pl.num_programs(1) - 1)
    def _():
        sout_ref[...] = s_scr[...]


def _rwkv(z, shift0, s0, wts, C):
    B, L, _ = z.shape
    fixed = lambda b, c: (0, 0)
    full = lambda a: pl.BlockSpec(a.shape, fixed)
    names = ["mu", "w0", "w2", "a0", "a2", "g2", "k_k", "k_a", "r_k", "lnx_g", "lnx_b", "ones_bd"]
    return pl.pallas_call(
        functools.partial(_rwkv_kernel, C=C),
        grid=(B, L // C),
        in_specs=[pl.BlockSpec((None, C, RWKV_COLS), lambda b, c: (b, c, 0)),
                  pl.BlockSpec((None, 1, RWKV_COLS), lambda b, c: (b, 0, 0)),
                  pl.BlockSpec((None, RWKV_HEADS, RWKV_HEAD, RWKV_HEAD), lambda b, c: (b, 0, 0, 0))]
                 + [full(wts[n]) for n in names],
        out_specs=[pl.BlockSpec((None, C, RWKV_DIM), lambda b, c: (b, c, 0)),
                   pl.BlockSpec((None, RWKV_HEADS, RWKV_HEAD, RWKV_HEAD), lambda b, c: (b, 0, 0, 0))],
        out_shape=[jax.ShapeDtypeStruct((B, L, RWKV_DIM), BF16),
                   jax.ShapeDtypeStruct((B, RWKV_HEADS, RWKV_HEAD, RWKV_HEAD), F32)],
        scratch_shapes=[pltpu.VMEM((RWKV_HEADS, RWKV_HEAD, RWKV_HEAD), F32), pltpu.VMEM((1, RWKV_COLS), F32)],
        compiler_params=_cparams(("arbitrary", "arbitrary")),
        name="rwkv",
    )(z, shift0, s0, *[wts[n] for n in names])


def _merge_kernel(x_ref, oa_ref, ob_ref, gl_ref, woa_ref, wob_ref, fg_ref, wrh_ref, wrl_ref, br_ref,
                  x1_ref, h2_ref, ri_ref, rw_ref, cnt_ref, run_scr, *, tm):
    @pl.when(pl.program_id(0) == 0)
    def _():
        run_scr[...] = jnp.zeros_like(run_scr)

    pa = _dot(oa_ref[...], woa_ref[...])
    pb = _dot(ob_ref[...], wob_ref[...])
    gate = lambda n: _sigmoid(gl_ref[:, n * D_MODEL:(n + 1) * D_MODEL].astype(F32))
    x1 = x_ref[...] + (gate(0) * pa + gate(1) * pb)
    x1_ref[...] = x1
    h2 = x1 * _rms(x1) * fg_ref[...]
    _store_token_tiles(h2_ref, h2)
    hh, hl = _split(h2)
    lg = _dot(hh, wrh_ref[...]) + (_dot(hl, wrh_ref[...]) + _dot(hh, wrl_ref[...])) + br_ref[...]

    lane = lax.broadcasted_iota(I32, (tm, LANES), 1)
    lane_f = lane.astype(F32)
    first = lambda hit: jnp.min(jnp.where(hit, lane_f, float(LANES)), axis=-1, keepdims=True).astype(I32)
    is_group = jnp.logical_and(lane >= ROUTE_GROUP_LANE, lane < ROUTE_GROUP_LANE + N_GROUPS)
    gl = jnp.where(is_group, lg, MASK_VALUE)
    gmax = jnp.max(gl, axis=-1, keepdims=True)
    gidx = first(gl == gmax) - ROUTE_GROUP_LANE
    g_top = 1.0 / jnp.sum(jnp.where(is_group, jnp.exp(gl - gmax), 0.0), axis=-1, keepdims=True)
    in_group = jnp.logical_and(lane < N_EXPERTS, jnp.right_shift(lane, 3) == gidx)
    el = jnp.where(in_group, lg, MASK_VALUE)
    e1 = jnp.max(el, axis=-1, keepdims=True)
    i1 = first(el == e1)
    el2 = jnp.where(lane == i1, MASK_VALUE, el)
    e2 = jnp.max(el2, axis=-1, keepdims=True)
    i2 = first(el2 == e2)
    t = jnp.exp(e2 - e1)
    w1 = g_top / (1.0 + t)
    w2 = g_top * t / (1.0 + t)

    hit1 = lane == i1
    hit2 = lane == i2
    oh = jnp.where(jnp.logical_or(hit1, hit2), 1.0, 0.0)
    ti = lax.broadcasted_iota(I32, (tm, tm), 0)
    tj = lax.broadcasted_iota(I32, (tm, tm), 1)
    before = _dot(jnp.where(ti > tj, 1.0, 0.0).astype(BF16), oh.astype(BF16)) + run_scr[...]
    rank1 = jnp.sum(jnp.where(hit1, before, 0.0), axis=-1, keepdims=True)
    rank2 = jnp.sum(jnp.where(hit2, before, 0.0), axis=-1, keepdims=True)
    run = run_scr[...] + jnp.sum(oh, axis=0, keepdims=True)
    run_scr[...] = run
    cnt_ref[...] = run.astype(I32)

    sel = lambda vals: functools.reduce(lambda acc, kv: jnp.where(lane == kv[0], kv[1], acc), enumerate(vals), 0)
    ri_ref[...] = sel([i1, i2, rank1.astype(I32), rank2.astype(I32)])
    rw_ref[...] = sel([w1, w2])


def _merge(x, oa, ob, gl, wts, tm):
    T = x.shape[0]
    row = lambda i: (i, 0)
    full = lambda a: pl.BlockSpec(a.shape, lambda i: (0,) * a.ndim)
    names = ["w_out_a", "w_out_b", "ffn_g", "wr_hi", "wr_lo", "b_r"]
    return pl.pallas_call(
        functools.partial(_merge_kernel, tm=tm),
        grid=(T // tm,),
        in_specs=[pl.BlockSpec((tm, D_MODEL), row), pl.BlockSpec((tm, MLA_HEADS * HEAD_PAD), row),
                  pl.BlockSpec((tm, RWKV_DIM), row), pl.BlockSpec((tm, GATE_COLS), row)]
                 + [full(wts[n]) for n in names],
        out_specs=[pl.BlockSpec((tm, D_MODEL), row), pl.BlockSpec((tm * ROW_TILE, LANES), row),
                   pl.BlockSpec((tm, LANES), row), pl.BlockSpec((tm, LANES), row),
                   pl.BlockSpec((1, LANES), lambda i: (0, 0))],
        out_shape=[jax.ShapeDtypeStruct((T, D_MODEL), F32), jax.ShapeDtypeStruct((T * ROW_TILE, LANES), F32),
                   jax.ShapeDtypeStruct((T, LANES), I32), jax.ShapeDtypeStruct((T, LANES), F32),
                   jax.ShapeDtypeStruct((1, LANES), I32)],
        scratch_shapes=[pltpu.VMEM((1, LANES), F32)],
        compiler_params=_cparams(("arbitrary",)),
        name="merge",
    )(x, oa, ob, gl, *[wts[n] for n in names])


def _dispatch_kernel(pend_ref, cnt_ref, dest_ref, h_ref, xs_ref, zero_scr, sems, *, tm):
    i = pl.program_id(0)
    tile = lambda ref, slot, rows=1: ref.at[pl.ds(slot * ROW_TILE, rows * ROW_TILE), :]

    @pl.when(i == 0)
    def _():
        zero_scr[...] = jnp.zeros_like(zero_scr)
        fill = lambda e: pltpu.make_async_copy(zero_scr, tile(xs_ref, pend_ref[e] - FFN_ROWS, FFN_ROWS), sems.at[1])
        for e in range(N_EXPERTS):
            pl.when(cnt_ref[e] > 0)(lambda e=e: fill(e).start())
        for e in range(N_EXPERTS):
            pl.when(cnt_ref[e] > 0)(lambda e=e: fill(e).wait())
        dead = lambda b: pltpu.make_async_copy(zero_scr, tile(xs_ref, b * FFN_ROWS, FFN_ROWS), sems.at[1])
        first_dead = pend_ref[N_EXPERTS - 1] // FFN_ROWS
        n_blocks = xs_ref.shape[0] // (FFN_ROWS * ROW_TILE)
        lax.fori_loop(first_dead, n_blocks, lambda b, c: (dead(b).start(), c)[1], 0)
        lax.fori_loop(first_dead, n_blocks, lambda b, c: (dead(b).wait(), c)[1], 0)

    sem = sems.at[0]

    def issue(t, carry):
        for s in range(TOP_K):
            d = dest_ref[0, 0, TOP_K * t + s]
            pltpu.make_async_copy(tile(h_ref, t), tile(xs_ref, d), sem).start()
        return carry

    lax.fori_loop(0, tm, issue, 0, unroll=DMA_ISSUE_UNROLL)
    for s in range(TOP_K):
        pltpu.make_async_copy(h_ref, tile(xs_ref, 0, tm), sem).wait()


def _dispatch(h2t, dest, pend, counts, n_slots, tm):
    T = h2t.shape[0] // ROW_TILE
    dest3 = dest.reshape(T // tm, 1, TOP_K * tm)
    grid_spec = pltpu.PrefetchScalarGridSpec(
        num_scalar_prefetch=2,
        grid=(T // tm,),
        in_specs=[pl.BlockSpec((1, 1, TOP_K * tm), lambda i, pe, cn: (i, 0, 0), memory_space=pltpu.SMEM),
                  pl.BlockSpec((tm * ROW_TILE, LANES), lambda i, pe, cn: (i, 0))],
        out_specs=pl.BlockSpec(memory_space=pl.ANY),
        scratch_shapes=[pltpu.VMEM((FFN_ROWS * ROW_TILE, LANES), F32), pltpu.SemaphoreType.DMA((2,))],
    )
    return pl.pallas_call(
        functools.partial(_dispatch_kernel, tm=tm),
        grid_spec=grid_spec,
        out_shape=jax.ShapeDtypeStruct((n_slots * ROW_TILE, LANES), F32),
        compiler_params=_cparams(("arbitrary",)),
        name="dispatch",
    )(pend, counts, dest3, h2t)


def _ffn_kernel(be_ref, nv_ref, x_ref, wg_ref, wu_ref, wd_ref, y_ref):
    del be_ref
    live = pl.program_id(0) < nv_ref[0]

    @pl.when(live)
    def _():
        x = _load_token_tiles(x_ref, FFN_ROWS).astype(BF16)
        gate = _dot(x, wg_ref[...])
        up = _dot(x, wu_ref[...])
        _store_token_tiles(y_ref, _dot((gate * _sigmoid(gate) * up).astype(BF16), wd_ref[...]))

    @pl.when(jnp.logical_not(live))
    def _():
        y_ref[...] = jnp.zeros_like(y_ref)


def _ffn(xs, block_e, n_live, wts):
    nb = xs.shape[0] // (FFN_ROWS * ROW_TILE)
    live_block = lambda i, be, nv: (jnp.minimum(i, nv[0] - 1), 0)
    grid_spec = pltpu.PrefetchScalarGridSpec(
        num_scalar_prefetch=2,
        grid=(nb,),
        in_specs=[pl.BlockSpec((FFN_ROWS * ROW_TILE, LANES), live_block),
                  pl.BlockSpec((None, D_MODEL, EXPERT_FF), lambda i, be, nv: (be[i], 0, 0)),
                  pl.BlockSpec((None, D_MODEL, EXPERT_FF), lambda i, be, nv: (be[i], 0, 0)),
                  pl.BlockSpec((None, EXPERT_FF, D_MODEL), lambda i, be, nv: (be[i], 0, 0))],
        out_specs=pl.BlockSpec((FFN_ROWS * ROW_TILE, LANES), lambda i, be, nv: (i, 0)),
    )
    return pl.pallas_call(
        _ffn_kernel,
        grid_spec=grid_spec,
        out_shape=jax.ShapeDtypeStruct(xs.shape, F32),
        compiler_params=_cparams(("arbitrary",)),
        name="ffn",
    )(block_e, n_live, xs, wts["w_eg"], wts["w_eu"], wts["w_ed"])


def _combine_kernel(dest_ref, dest_next_ref, x1_ref, rw_ref, yb_ref, o_ref, y_scr, sems, *, tm):
    i = pl.program_id(0)
    n = pl.num_programs(0)
    slot = i % 2
    tile = lambda ref, t, rows=1: ref.at[pl.ds(t * ROW_TILE, rows * ROW_TILE), :]

    def gather(dref, buf):
        def issue(t, carry):
            for s in range(TOP_K):
                d = dref[0, 0, TOP_K * t + s]
                pltpu.make_async_copy(tile(yb_ref, d), tile(y_scr.at[buf, s], t), sems.at[buf]).start()
            return carry

        lax.fori_loop(0, tm, issue, 0, unroll=DMA_ISSUE_UNROLL)

    pl.when(i == 0)(lambda: gather(dest_ref, 0))
    pl.when(i + 1 < n)(lambda: gather(dest_next_ref, 1 - slot))
    for s in range(TOP_K):
        pltpu.make_async_copy(tile(yb_ref, 0, tm), y_scr.at[slot, s], sems.at[slot]).wait()
    rw = rw_ref[...]
    y = [_load_token_tiles(y_scr.at[slot, s], tm) for s in range(TOP_K)]
    o_ref[...] = x1_ref[...] + (rw[:, 0:1] * y[0] + rw[:, 1:2] * y[1])


def _combine(x1, rw, yb, dest, tm):
    T = x1.shape[0]
    nt = T // tm
    dest3 = dest.reshape(nt, 1, TOP_K * tm)
    dest_spec = lambda step: pl.BlockSpec((1, 1, TOP_K * tm), lambda i: (step(i), 0, 0), memory_space=pltpu.SMEM)
    return pl.pallas_call(
        functools.partial(_combine_kernel, tm=tm),
        grid=(nt,),
        in_specs=[dest_spec(lambda i: i), dest_spec(lambda i: jnp.minimum(i + 1, nt - 1)),
                  pl.BlockSpec((tm, D_MODEL), lambda i: (i, 0)),
                  pl.BlockSpec((tm, LANES), lambda i: (i, 0)),
                  pl.BlockSpec(memory_space=pl.ANY)],
        out_specs=pl.BlockSpec((tm, D_MODEL), lambda i: (i, 0)),
        out_shape=jax.ShapeDtypeStruct((T, D_MODEL), F32),
        scratch_shapes=[pltpu.VMEM((2, TOP_K, tm * ROW_TILE, LANES), F32), pltpu.SemaphoreType.DMA((2,))],
        compiler_params=_cparams(("arbitrary",)),
        name="combine",
    )(dest3, dest3, x1, rw, yb)


def _moe(x1, h2, ri, rw, cnt, wts, tm):
    T = x1.shape[0]
    ids = ri[:, 0:TOP_K]
    ranks = ri[:, TOP_K:2 * TOP_K]
    counts = cnt[0, :N_EXPERTS]
    padded = (counts + FFN_ROWS - 1) // FFN_ROWS * FFN_ROWS
    pend = jnp.cumsum(padded)
    pstart = pend - padded
    dest = (pstart[ids] + ranks).astype(I32)
    nb = -(-T * TOP_K // FFN_ROWS) + N_EXPERTS
    block_pos = jnp.arange(nb, dtype=I32) * FFN_ROWS
    block_e = jnp.minimum(jnp.sum(pend[None, :] <= block_pos[:, None], axis=1), N_EXPERTS - 1).astype(I32)
    n_live = (pend[-1:] // FFN_ROWS).astype(I32)
    xs = _dispatch(h2, dest, pend.astype(I32), counts.astype(I32), nb * FFN_ROWS, min(DISPATCH_TOKENS, T))
    yb = _ffn(xs, block_e, n_live, wts)
    return _combine(x1, rw, yb, dest, tm)


def _rope_tables(pos):
    inv = 1.0 / (ROPE_BASE ** (jnp.arange(HALF_ROPE, dtype=F32) * (2.0 / QK_ROPE)))
    ang = pos.astype(F32)[:, None] * inv[None, :]
    cos, sin = jnp.cos(ang), jnp.sin(ang)
    n = pos.shape[0]
    pad = jnp.zeros((n, HEAD_PAD - QK_DIM), F32)
    ct = jnp.concatenate([jnp.ones((n, QK_NOPE), F32), cos, cos, pad], axis=1)
    st = jnp.concatenate([jnp.zeros((n, QK_NOPE), F32), -sin, sin, pad], axis=1)
    kt = jnp.concatenate([cos, sin], axis=1)
    return ct, st, kt


def _score_bound(q_head_norm_g, k_head_norm_g):
    return (1.02 * QK_DIM * ATTN_SCALE * math.log2(math.e)) * (
        jnp.max(jnp.abs(q_head_norm_g)) * jnp.max(jnp.abs(k_head_norm_g)))


def _prep_weights(norm_mix_g, w_in, q_norm_g, w_uq, kv_norm_g, w_ukv, q_head_norm_g, k_head_norm_g,
                  rwkv_mu, rwkv_w0, rwkv_w2, rwkv_a0, rwkv_a2, rwkv_g2, rwkv_k_k, rwkv_k_a, rwkv_r_k,
                  rwkv_lnx_g, rwkv_lnx_b, w_out, norm_ffn_g, w_router_group, b_router_group,
                  w_router_expert, b_router_expert, w_expert_gate, w_expert_up, w_expert_down):
    row = lambda a: a.reshape(1, -1).astype(F32)
    head_pad = lambda a: jnp.pad(a, ((0, 0), (0, 0), (0, HEAD_PAD - a.shape[-1])))
    w = {}
    w["mix_g"] = row(norm_mix_g)
    w["w_in"] = jnp.concatenate([w_in[:, :MLA_COLS], jnp.zeros((D_MODEL, LANES - QK_ROPE), F32),
                                 w_in[:, MLA_COLS:]], axis=1).astype(BF16)
    w["q_g"] = row(q_norm_g)
    w["kv_g"] = row(kv_norm_g)
    w["wqa"] = head_pad(w_uq).reshape(Q_LORA, MLA_HEADS * HEAD_PAD).astype(BF16)
    partner = jnp.concatenate([jnp.zeros_like(w_uq[..., :QK_NOPE]), w_uq[..., QK_NOPE + HALF_ROPE:],
                               w_uq[..., QK_NOPE:QK_NOPE + HALF_ROPE]], axis=-1)
    w["wqb"] = head_pad(partner).reshape(Q_LORA, MLA_HEADS * HEAD_PAD).astype(BF16)
    pad_g = lambda g: jnp.pad(g, (0, HEAD_PAD - QK_DIM)).reshape(1, HEAD_PAD).astype(F32)
    w["qh_g"] = pad_g(q_head_norm_g) * (ATTN_SCALE * math.log2(math.e))
    w["kh_g"] = pad_g(k_head_norm_g)
    lane_one = lambda lane: (jnp.arange(HEAD_PAD) == lane).astype(F32).reshape(1, HEAD_PAD)
    w["score_bound"] = _score_bound(q_head_norm_g, k_head_norm_g)
    w["q_one"] = lane_one(QK_DIM)
    w["k_shift"] = -w["score_bound"] * lane_one(QK_DIM)
    w["v_one"] = jnp.tile(lane_one(V_DIM), (1, MLA_HEADS))
    w["wk"] = head_pad(w_ukv[..., :QK_NOPE]).reshape(KV_LORA, MLA_HEADS * HEAD_PAD).astype(BF16)
    w["wv"] = head_pad(w_ukv[..., QK_NOPE:]).reshape(KV_LORA, MLA_HEADS * HEAD_PAD).astype(BF16)
    w["place"] = jnp.pad(jnp.eye(QK_ROPE, dtype=F32), ((0, 0), (QK_NOPE, HEAD_PAD - QK_DIM))).astype(BF16)
    w["mu"] = row(rwkv_mu)
    w["w0"] = row(rwkv_w0)
    w["w2"] = jnp.concatenate([rwkv_w2, jnp.zeros((A_LORA, RWKV_DIM), F32)], axis=0).astype(BF16)
    w["a0"] = row(rwkv_a0)
    w["a2"] = jnp.concatenate([jnp.zeros((W_LORA, RWKV_DIM), F32), rwkv_a2], axis=0).astype(BF16)
    w["g2"] = rwkv_g2.astype(BF16)
    w["k_k"] = row(rwkv_k_k)
    w["k_a"] = row(rwkv_k_a)
    w["r_k"] = row(rwkv_r_k)
    w["lnx_g"] = row(rwkv_lnx_g)
    w["lnx_b"] = row(rwkv_lnx_b)
    head_of = jnp.arange(RWKV_DIM) // RWKV_HEAD
    w["ones_bd"] = (head_of[:, None] == head_of[None, :]).astype(BF16)
    w["w_out_a"] = jnp.pad(w_out[0].reshape(MLA_HEADS, V_DIM, D_MODEL), ((0, 0), (0, HEAD_PAD - V_DIM), (0, 0))
                           ).reshape(MLA_HEADS * HEAD_PAD, D_MODEL).astype(BF16)
    w["w_out_b"] = w_out[1].astype(BF16)
    w["ffn_g"] = row(norm_ffn_g)
    wr = jnp.concatenate([w_router_expert.reshape(D_MODEL, N_EXPERTS), w_router_group,
                          jnp.zeros((D_MODEL, LANES - N_EXPERTS - N_GROUPS), F32)], axis=1)
    w["wr_hi"] = wr.astype(BF16)
    w["wr_lo"] = (wr - w["wr_hi"].astype(F32)).astype(BF16)
    w["b_r"] = jnp.concatenate([b_router_expert.reshape(-1), b_router_group,
                                jnp.zeros((LANES - N_EXPERTS - N_GROUPS,), F32)]).reshape(1, LANES)
    w["w_eg"] = w_expert_gate.astype(BF16)
    w["w_eu"] = w_expert_up.astype(BF16)
    w["w_ed"] = w_expert_down.astype(BF16)
    return w


def _round_up(n, m):
    return -(-n // m) * m


def _layer(x, pos, lat_past, kpe_past, wkv0, shift0, wts):
    B, L, _ = x.shape
    T = B * L
    tm = min(256, T)
    ct, st, kt = _rope_tables(pos)
    if L % tm:
        reps = tm // L
        ct, st, kt = (jnp.tile(t, (reps, 1)) for t in (ct, st, kt))
    q, lat, kpe, z, gl = _inproj(x.reshape(T, D_MODEL), (ct, st, kt), wts, tm)
    lat3 = lat.reshape(B, L, KV_LORA)
    kpe3 = kpe.reshape(B, L, QK_ROPE)
    if lat_past is None:
        n_past, lk, lat_all, kpe_all = 0, L, lat3, kpe3
        tq = tk = min(ATTN_TILE, L)
        diag = True
    else:
        n_past = lat_past.shape[1]
        lk_valid = n_past + L
        lk = _round_up(lk_valid, LANES)
        padk = lambda a: jnp.pad(a, ((0, 0), (0, lk - lk_valid), (0, 0)))
        lat_all = padk(jnp.concatenate([lat_past, lat3], axis=1))
        kpe_all = padk(jnp.concatenate([kpe_past, kpe3], axis=1))
        tq, tk, diag = L, lk, False
    k, v = _kvprep(lat_all.reshape(B * lk, KV_LORA), kpe_all.reshape(B * lk, QK_ROPE), wts, min(512, B * lk))
    oa = _attention(q.reshape(B, L, -1), k.reshape(B, lk, -1), v.reshape(B, lk, -1),
                    wts["score_bound"] <= MAX_DIRECT_EXP2_SHIFT, tq=tq, tk=tk, n_past=n_past,
                    lk_valid=n_past + L, diag_tiles=diag)
    z3 = z.reshape(B, L, RWKV_COLS)
    ob, s_new = _rwkv(z3, shift0, wkv0, wts, min(SCAN_CHUNK, L))
    x1, h2, ri, rw, cnt = _merge(x.reshape(T, D_MODEL), oa.reshape(T, -1), ob.reshape(T, -1), gl, wts, tm)
    y = _moe(x1, h2, ri, rw, cnt, wts, tm)
    return y.reshape(B, L, D_MODEL), lat3, kpe3, s_new, z3[:, L - 1:, :]


def kernel(x_prompt, x_sample, cache_kv_latent, cache_k_rope, state_wkv, state_shift, norm_mix_g, w_in, q_norm_g, w_uq, kv_norm_g, w_ukv, q_head_norm_g, k_head_norm_g, rwkv_mu, rwkv_w0, rwkv_w2, rwkv_a0, rwkv_a2, rwkv_g2, rwkv_k_k, rwkv_k_a, rwkv_r_k, rwkv_lnx_g, rwkv_lnx_b, w_out, norm_ffn_g, w_router_group, b_router_group, w_router_expert, b_router_expert, w_expert_gate, w_expert_up, w_expert_down):
    weights = (norm_mix_g, w_in, q_norm_g, w_uq, kv_norm_g, w_ukv, q_head_norm_g, k_head_norm_g,
               rwkv_mu, rwkv_w0, rwkv_w2, rwkv_a0, rwkv_a2, rwkv_g2, rwkv_k_k, rwkv_k_a, rwkv_r_k,
               rwkv_lnx_g, rwkv_lnx_b, w_out, norm_ffn_g, w_router_group, b_router_group,
               w_router_expert, b_router_expert, w_expert_gate, w_expert_up, w_expert_down)
    depth = w_in.shape[0]
    bp, lp, _ = x_prompt.shape
    n_past = cache_kv_latent.shape[2]
    pos_p = jnp.arange(lp, dtype=I32)
    pos_s = n_past + jnp.arange(x_sample.shape[1], dtype=I32)
    wkv_zero = jnp.zeros((bp, RWKV_HEADS, RWKV_HEAD, RWKV_HEAD), F32)
    shift_zero = jnp.zeros((bp, 1, RWKV_COLS), F32)
    xp, xs = x_prompt, x_sample
    outs_p, outs_s = [], []
    for l in range(depth):
        wts = _prep_weights(*(wt[l] for wt in weights))
        xp, *rest_p = _layer(xp, pos_p, None, None, wkv_zero, shift_zero, wts)
        xs, *rest_s = _layer(xs, pos_s, cache_kv_latent[l], cache_k_rope[l], state_wkv[l], state_shift[l], wts)
        outs_p.append(rest_p)
        outs_s.append(rest_s)
    stack = lambda outs, i: jnp.stack([o[i] for o in outs], 0)
    return (xp, xs, stack(outs_p, 0), stack(outs_p, 1), stack(outs_p, 2), stack(outs_p, 3),
            stack(outs_s, 0), stack(outs_s, 1), stack(outs_s, 2), stack(outs_s, 3))
```

```python
import functools
import math

import jax
import jax.numpy as jnp
from jax import lax
from jax.experimental import pallas as pl
from jax.experimental.pallas import tpu as pltpu

F32 = jnp.float32
BF16 = jnp.bfloat16
I32 = jnp.int32

D_MODEL = 1024
CHUNK = 64
RMS_EPS = 1e-6
MASK_VALUE = -1e30
MLA_HEADS = 8
QK_NOPE = 64
QK_ROPE = 32
QK_DIM = QK_NOPE + QK_ROPE
V_DIM = 64
Q_LORA = 256
KV_LORA = 128
ROPE_BASE = 10000.0
ATTN_SCALE = QK_DIM ** -0.5
RWKV_HEADS = 8
RWKV_HEAD = 64
RWKV_DIM = RWKV_HEADS * RWKV_HEAD
W_LORA = 64
A_LORA = 64
G_LORA = 128
LN_X_EPS = 64e-5
MLA_COLS = Q_LORA + KV_LORA + QK_ROPE
RWKV_COLS = 3 * RWKV_DIM + W_LORA + A_LORA + G_LORA
GATE_COLS = 2 * D_MODEL
N_GROUPS = 4
EXPERTS_PER_GROUP = 8
N_EXPERTS = N_GROUPS * EXPERTS_PER_GROUP
EXPERT_FF = 256
TOP_K = 2

LANES = 128
ROW_TILE = D_MODEL // LANES
DMA_ISSUE_UNROLL = 4
DISPATCH_TOKENS = 512
HEAD_PAD = LANES
HALF_ROPE = QK_ROPE // 2
SEG_Q = 0
SEG_KV = SEG_Q + Q_LORA
SEG_PE = SEG_KV + KV_LORA
SEG_Z = SEG_PE + LANES
SEG_G = SEG_Z + RWKV_COLS
IN_COLS_PAD = SEG_G + GATE_COLS
ROUTE_GROUP_LANE = N_EXPERTS
SCAN_CHUNK = 128
SCAN_HEAD_GROUP = 8
FFN_ROWS = 256
ATTN_QUERY_TILE = 512
ATTN_KEY_TILE = 512
MAX_DIRECT_EXP2_SHIFT = 40.0
VMEM_LIMIT = 48 * 1024 * 1024


def _cparams(sem, vmem=VMEM_LIMIT):
    return pltpu.CompilerParams(dimension_semantics=sem, vmem_limit_bytes=vmem)


def _dot(a, b):
    return jnp.dot(a, b, preferred_element_type=F32)


def _dot_nt(a, b):
    return lax.dot_general(a, b, (((1,), (1,)), ((), ())), preferred_element_type=F32)


def _dot_tn(a, b):
    return lax.dot_general(a, b, (((0,), (0,)), ((), ())), preferred_element_type=F32)


def _split(x):
    hi = x.astype(BF16)
    lo = (x - hi.astype(F32)).astype(BF16)
    return hi, lo


def _dot2_exact_rhs(a, b_bf16):
    ah, al = _split(a)
    return _dot(ah, b_bf16) + _dot(al, b_bf16)


def _sigmoid(x):
    return 1.0 / (1.0 + jnp.exp(-x))


def _rms(x, n=None):
    n = x.shape[-1] if n is None else n
    return lax.rsqrt(jnp.sum(x * x, axis=-1, keepdims=True) * (1.0 / n) + RMS_EPS)


def _store_token_tiles(ref, x):
    n = x.shape[0]
    for c in range(ROW_TILE):
        ref[pl.ds(c, n, stride=ROW_TILE), :] = x[:, c * LANES:(c + 1) * LANES]


def _load_token_tiles(ref, n):
    return jnp.concatenate([ref[pl.ds(c, n, stride=ROW_TILE), :] for c in range(ROW_TILE)], axis=1)


def _inproj_kernel(x_ref, g_ref, w_ref, qg_ref, wqa_ref, wqb_ref, ct_ref, st_ref, kt_ref, kvg_ref, qhg_ref, qone_ref,
                   q_ref, lat_ref, kpe_ref, z_ref, gl_ref):
    x = x_ref[...]
    h = (x * _rms(x) * g_ref[...]).astype(BF16)
    cq = _dot(h, w_ref[:, SEG_Q:SEG_KV])
    ckv = _dot(h, w_ref[:, SEG_KV:SEG_PE])
    pe = _dot(h, w_ref[:, SEG_PE:SEG_Z])
    z_ref[...] = _dot(h, w_ref[:, SEG_Z:SEG_G])
    gl_ref[...] = _dot(h, w_ref[:, SEG_G:IN_COLS_PAD]).astype(gl_ref.dtype)
    lat_ref[...] = ckv * _rms(ckv) * kvg_ref[...]
    x1 = pe[:, :HALF_ROPE]
    x2 = pe[:, HALF_ROPE:QK_ROPE]
    c = kt_ref[:, :HALF_ROPE]
    s = kt_ref[:, HALF_ROPE:QK_ROPE]
    kpe_ref[...] = jnp.concatenate([x1 * c - x2 * s, x1 * s + x2 * c], axis=-1)
    cqn = (cq * _rms(cq) * qg_ref[...]).astype(BF16)
    ct = ct_ref[...]
    st = st_ref[...]
    qhg = qhg_ref[...]
    qone = qone_ref[...]
    for hd in range(MLA_HEADS):
        sl = slice(hd * HEAD_PAD, (hd + 1) * HEAD_PAD)
        xh = _dot(cqn, wqa_ref[:, sl]) * ct + _dot(cqn, wqb_ref[:, sl]) * st
        q_ref[:, sl] = (xh * _rms(xh, QK_DIM) * qhg + qone).astype(BF16)


def _inproj(x, tabs, wts, tm):
    T = x.shape[0]
    ct, st, kt = tabs
    nt = ct.shape[0] // tm
    row = lambda i: (i, 0)
    tab = lambda i: (i % nt, 0)
    fixed = lambda i: (0, 0)
    full = lambda a: pl.BlockSpec(a.shape, fixed)
    return pl.pallas_call(
        _inproj_kernel,
        grid=(T // tm,),
        in_specs=[pl.BlockSpec((tm, D_MODEL), row), full(wts["mix_g"]), full(wts["w_in"]), full(wts["q_g"]),
                  full(wts["wqa"]), full(wts["wqb"]),
                  pl.BlockSpec((tm, HEAD_PAD), tab), pl.BlockSpec((tm, HEAD_PAD), tab),
                  pl.BlockSpec((tm, QK_ROPE), tab), full(wts["kv_g"]), full(wts["qh_g"]), full(wts["q_one"])],
        out_specs=[pl.BlockSpec((tm, MLA_HEADS * HEAD_PAD), row), pl.BlockSpec((tm, KV_LORA), row),
                   pl.BlockSpec((tm, QK_ROPE), row), pl.BlockSpec((tm, RWKV_COLS), row),
                   pl.BlockSpec((tm, GATE_COLS), row)],
        out_shape=[jax.ShapeDtypeStruct((T, MLA_HEADS * HEAD_PAD), BF16), jax.ShapeDtypeStruct((T, KV_LORA), F32),
                   jax.ShapeDtypeStruct((T, QK_ROPE), F32), jax.ShapeDtypeStruct((T, RWKV_COLS), F32),
                   jax.ShapeDtypeStruct((T, GATE_COLS), BF16)],
        compiler_params=_cparams(("arbitrary",)),
        name="inproj",
    )(x, wts["mix_g"], wts["w_in"], wts["q_g"], wts["wqa"], wts["wqb"], ct, st, kt, wts["kv_g"], wts["qh_g"],
      wts["q_one"])


def _kvprep_kernel(lat_ref, kpe_ref, wk_ref, wv_ref, place_ref, kg_ref, kshift_ref, vone_ref, k_ref, v_ref):
    lat = lat_ref[...].astype(BF16)
    rot = _dot2_exact_rhs(kpe_ref[...], place_ref[...])
    kg = kg_ref[...]
    kshift = kshift_ref[...]
    for hd in range(MLA_HEADS):
        sl = slice(hd * HEAD_PAD, (hd + 1) * HEAD_PAD)
        kh = _dot(lat, wk_ref[:, sl]) + rot
        k_ref[:, sl] = (kh * _rms(kh, QK_DIM) * kg + kshift).astype(BF16)
    v_ref[...] = (_dot(lat, wv_ref[...]) + vone_ref[...]).astype(BF16)


def _kvprep(lat, kpe, wts, tm):
    T = lat.shape[0]
    row = lambda i: (i, 0)
    fixed = lambda i: (0, 0)
    full = lambda a: pl.BlockSpec(a.shape, fixed)
    return pl.pallas_call(
        _kvprep_kernel,
        grid=(T // tm,),
        in_specs=[pl.BlockSpec((tm, KV_LORA), row), pl.BlockSpec((tm, QK_ROPE), row), full(wts["wk"]),
                  full(wts["wv"]), full(wts["place"]), full(wts["kh_g"]), full(wts["k_shift"]),
                  full(wts["v_one"])],
        out_specs=[pl.BlockSpec((tm, MLA_HEADS * HEAD_PAD), row), pl.BlockSpec((tm, MLA_HEADS * HEAD_PAD), row)],
        out_shape=[jax.ShapeDtypeStruct((T, MLA_HEADS * HEAD_PAD), BF16),
                   jax.ShapeDtypeStruct((T, MLA_HEADS * HEAD_PAD), BF16)],
        compiler_params=_cparams(("arbitrary",)),
        name="kvprep",
    )(lat, kpe, wts["wk"], wts["wv"], wts["place"], wts["kh_g"], wts["k_shift"], wts["v_one"])


def _attn_kernel(q_ref, k_ref, v_ref, o_ref, *, tq, tk, n_past, lk_valid, diag_tiles, bounded):
    i = pl.program_id(2)
    n_full = i * (tq // tk) if diag_tiles else 0
    head_lanes = [slice(hh * HEAD_PAD, (hh + 1) * HEAD_PAD) for hh in range(2)]
    qs = [q_ref[:, hs] for hs in head_lanes]

    def scores(hh, j, masked):
        ks = pl.multiple_of(j * tk, tk)
        s = _dot_nt(qs[hh], k_ref[pl.ds(ks, tk), head_lanes[hh]])
        if masked:
            qpos = n_past + i * tq + lax.broadcasted_iota(I32, (tq, tk), 0)
            kpos = j * tk + lax.broadcasted_iota(I32, (tq, tk), 1)
            vis = jnp.logical_and(jnp.right_shift(kpos, 6) <= jnp.right_shift(qpos, 6), kpos < lk_valid)
            s = jnp.where(vis, s, MASK_VALUE)
        return s, v_ref[pl.ds(ks, tk), head_lanes[hh]]

    if bounded:
        def head_step(hh, j, acc, masked):
            s, v = scores(hh, j, masked)
            return acc + _dot(jnp.exp2(s).astype(BF16), v)

        init = jnp.zeros((tq, HEAD_PAD), F32)
    else:
        def head_step(hh, j, carry, masked):
            m, acc = carry
            s, v = scores(hh, j, masked)
            m_new = jnp.maximum(m, jnp.max(s, axis=-1, keepdims=True))
            return m_new, jnp.exp2(m - m_new) * acc + _dot(jnp.exp2(s - m_new).astype(BF16), v)

        init = (jnp.full((tq, 1), MASK_VALUE, F32), jnp.zeros((tq, HEAD_PAD), F32))

    step = lambda j, carry, masked: tuple(head_step(hh, j, carry[hh], masked) for hh in range(2))
    carry = (init, init)
    if diag_tiles:
        carry = lax.fori_loop(0, n_full, functools.partial(step, masked=False), carry)
    for d in range(tq // tk if diag_tiles else 1):
        carry = step(n_full + d, carry, True)
    for hh in range(2):
        acc = carry[hh] if bounded else carry[hh][1]
        o_ref[:, head_lanes[hh]] = (acc / acc[:, V_DIM:V_DIM + 1]).astype(o_ref.dtype)


def _attention(q, k, v, bound_ok, *, tq, tk, n_past, lk_valid, diag_tiles):
    B, L, _ = q.shape
    Lk = k.shape[1]

    def run(bounded):
        kern = functools.partial(_attn_kernel, tq=tq, tk=tk, n_past=n_past, lk_valid=lk_valid,
                                 diag_tiles=diag_tiles, bounded=bounded)
        return pl.pallas_call(
            kern,
            grid=(B, MLA_HEADS // 2, L // tq),
            in_specs=[pl.BlockSpec((None, tq, 2 * HEAD_PAD), lambda b, p, i: (b, i, p)),
                      pl.BlockSpec((None, Lk, 2 * HEAD_PAD), lambda b, p, i: (b, 0, p)),
                      pl.BlockSpec((None, Lk, 2 * HEAD_PAD), lambda b, p, i: (b, 0, p))],
            out_specs=pl.BlockSpec((None, tq, 2 * HEAD_PAD), lambda b, p, i: (b, i, p)),
            out_shape=jax.ShapeDtypeStruct((B, L, MLA_HEADS * HEAD_PAD), BF16),
            compiler_params=_cparams(("arbitrary", "arbitrary", "arbitrary")),
            name="attn_bounded" if bounded else "attn_running_max",
        )(q, k, v)

    return lax.cond(bound_ok, lambda: run(True), lambda: run(False))


def _rwkv_kernel(z_ref, shift_ref, s0_ref, mu_ref, w0_ref, w2_ref, a0_ref, a2_ref, g2_ref, kk_ref, ka_ref, rk_ref,
                 lng_ref, lnb_ref, ones_ref, o_ref, sout_ref, s_scr, prev_scr, *, C):
    c = pl.program_id(1)

    @pl.when(c == 0)
    def _():
        s_scr[...] = s0_ref[...]
        prev_scr[...] = shift_ref[...]

    zf = z_ref[...]
    row = lax.broadcasted_iota(I32, (C, 1), 0)
    pv = jnp.where(row == 0, prev_scr[...], pltpu.roll(zf, 1, axis=0))
    prev_scr[...] = zf[C - 1:C, :]
    zm = zf + (pv - zf) * mu_ref[...]
    H = RWKV_DIM
    r = zm[:, 0:H]
    k = zm[:, H:2 * H]
    v = zm[:, 2 * H:3 * H]
    xwa = zm[:, 3 * H:3 * H + W_LORA + A_LORA]
    xg = zm[:, 3 * H + W_LORA + A_LORA:]
    lw = (-math.exp(-0.5)) * _sigmoid(w0_ref[...] + _dot(jnp.tanh(xwa).astype(BF16), w2_ref[...]))
    a = _sigmoid(a0_ref[...] + _dot(xwa.astype(BF16), a2_ref[...]))
    g = _dot(_sigmoid(xg).astype(BF16), g2_ref[...])
    ones_bd = ones_ref[...]
    kk = k * kk_ref[...]
    kkn = kk * lax.rsqrt(jnp.maximum(_dot2_exact_rhs(kk * kk, ones_bd), 1e-24))
    k2 = k * (1.0 + (a - 1.0) * ka_ref[...])
    head_sum = lambda x: _dot(x.astype(BF16), ones_bd)
    bonus = head_sum(r * k2 * rk_ref[...]) * v

    ti = lax.broadcasted_iota(I32, (C, C), 0)
    tj = lax.broadcasted_iota(I32, (C, C), 1)
    incl = ti >= tj
    strict = ti > tj
    tri = jnp.where(incl, 1.0, 0.0).astype(BF16)
    p1 = lw.astype(BF16)
    r1 = lw - p1.astype(F32)
    p2 = r1.astype(BF16)
    p3 = (r1 - p2.astype(F32)).astype(BF16)
    cs = _dot(tri, p1) + (_dot(tri, p2) + _dot(tri, p3))
    w_incl = jnp.exp(cs)
    w_inv = jnp.exp(-cs)
    at = -kkn * jnp.exp(cs - lw)
    rt = r * w_incl
    bt = kkn * a * w_inv
    kt = k2 * w_inv
    wc = w_incl[C - 1:C, :]
    bh = bt * wc
    kh = kt * wc
    eye = jnp.where(ti == tj, 1.0, 0.0).astype(F32)

    per_head = lambda x: [x[:, hd * RWKV_HEAD:(hd + 1) * RWKV_HEAD] for hd in range(RWKV_HEADS)]
    rows = lambda xs: jnp.concatenate(xs, axis=0)
    side = lambda xs: jnp.concatenate(xs, axis=1)
    hi_lo = lambda x: tuple(per_head(p) for p in _split(x))
    cast = lambda x: per_head(x.astype(BF16))
    at_b, rt_b, bt_b, kt_b = cast(at), cast(rt), cast(bt), cast(kt)
    (v_hi, v_lo), (bh_hi, bh_lo), (kh_hi, kh_lo) = hi_lo(v), hi_lo(bh), hi_lo(kh)
    wc_h = per_head(wc)

    def dot_hl(a, b_hi, b_lo):
        ah, al = _split(a)
        return _dot(side([ah, al, ah]), rows([b_hi, b_hi, b_lo]))

    def head_group(heads):
        s_all = {h: s_scr[h] for h in heads}
        s_b = {h: s_all[h].astype(BF16) for h in heads}
        pr = {h: _dot_nt(rows([at_b[h], rt_b[h]]), rows([bt_b[h], kt_b[h]])) for h in heads}
        a_ab = {h: jnp.where(strict, pr[h][:C, :C], 0.0) for h in heads}
        a_ak = {h: jnp.where(strict, pr[h][:C, C:], 0.0).astype(BF16) for h in heads}
        a_rb = {h: jnp.where(incl, pr[h][C:, :C], 0.0).astype(BF16) for h in heads}
        a_rk = {h: jnp.where(incl, pr[h][C:, C:], 0.0).astype(BF16) for h in heads}
        pw = a_ab
        tinv = {h: eye + pw[h] for h in heads}
        n = 2
        while n < C:
            pw_b = {h: pw[h].astype(BF16) for h in heads}
            pw = {h: _dot(pw_b[h], pw_b[h]) for h in heads}
            tinv = {h: tinv[h] + _dot(tinv[h].astype(BF16), pw[h].astype(BF16)) for h in heads}
            n *= 2
        resid = {h: (eye - tinv[h]) + dot_hl(a_ab[h], *_split(tinv[h])) for h in heads}
        tinv = {h: tinv[h] + _dot(tinv[h].astype(BF16), resid[h].astype(BF16)) for h in heads}
        x = {h: _dot_nt(at_b[h], s_b[h]) + _dot(a_ak[h], v_hi[h]) for h in heads}
        u = {h: _split(_dot(tinv[h].astype(BF16), x[h].astype(BF16))) for h in heads}
        ys = [_dot_nt(rt_b[h], s_b[h]) + _dot(side([a_rb[h], a_rk[h]]), rows([u[h][0], v_hi[h]])) for h in heads]
        for h in heads:
            u_hi, u_lo = u[h]
            upd = _dot_tn(rows([u_hi, v_hi[h], u_lo, v_lo[h], u_hi, v_hi[h]]),
                          rows([bh_hi[h], kh_hi[h], bh_hi[h], kh_hi[h], bh_lo[h], kh_lo[h]]))
            s_scr[h] = s_all[h] * wc_h[h] + upd
        return ys

    ys = []
    for first in range(0, RWKV_HEADS, SCAN_HEAD_GROUP):
        ys += head_group(range(first, first + SCAN_HEAD_GROUP))
    y = jnp.concatenate(ys, axis=-1)
    inv_n = 1.0 / RWKV_HEAD
    mean = head_sum(y) * inv_n
    d = y - mean
    var = head_sum(d * d) * inv_n
    yn = d * lax.rsqrt(var + LN_X_EPS) * lng_ref[...] + lnb_ref[...]
    o_ref[...] = ((yn + bonus) * g).astype(o_ref.dtype)

    @pl.when(c == pl.num_programs(1) - 1)
    def _():
        sout_ref[...] = s_scr[...]


def _rwkv(z, shift0, s0, wts, C):
    B, L, _ = z.shape
    fixed = lambda b, c: (0, 0)
    full = lambda a: pl.BlockSpec(a.shape, fixed)
    names = ["mu", "w0", "w2", "a0", "a2", "g2", "k_k", "k_a", "r_k", "lnx_g", "lnx_b", "ones_bd"]
    return pl.pallas_call(
        functools.partial(_rwkv_kernel, C=C),
        grid=(B, L // C),
        in_specs=[pl.BlockSpec((None, C, RWKV_COLS), lambda b, c: (b, c, 0)),
                  pl.BlockSpec((None, 1, RWKV_COLS), lambda b, c: (b, 0, 0)),
                  pl.BlockSpec((None, RWKV_HEADS, RWKV_HEAD, RWKV_HEAD), lambda b, c: (b, 0, 0, 0))]
                 + [full(wts[n]) for n in names],
        out_specs=[pl.BlockSpec((None, C, RWKV_DIM), lambda b, c: (b, c, 0)),
                   pl.BlockSpec((None, RWKV_HEADS, RWKV_HEAD, RWKV_HEAD), lambda b, c: (b, 0, 0, 0))],
        out_shape=[jax.ShapeDtypeStruct((B, L, RWKV_DIM), BF16),
                   jax.ShapeDtypeStruct((B, RWKV_HEADS, RWKV_HEAD, RWKV_HEAD), F32)],
        scratch_shapes=[pltpu.VMEM((RWKV_HEADS, RWKV_HEAD, RWKV_HEAD), F32), pltpu.VMEM((1, RWKV_COLS), F32)],
        compiler_params=_cparams(("arbitrary", "arbitrary")),
        name="rwkv",
    )(z, shift0, s0, *[wts[n] for n in names])


def _merge_kernel(x_ref, oa_ref, ob_ref, gl_ref, woa_ref, wob_ref, fg_ref, wrh_ref, wrl_ref, br_ref,
                  x1_ref, h2_ref, ri_ref, rw_ref, cnt_ref, run_scr, *, tm):
    @pl.when(pl.program_id(0) == 0)
    def _():
        run_scr[...] = jnp.zeros_like(run_scr)

    pa = _dot(oa_ref[...], woa_ref[...])
    pb = _dot(ob_ref[...], wob_ref[...])
    gate = lambda n: _sigmoid(gl_ref[:, n * D_MODEL:(n + 1) * D_MODEL].astype(F32))
    x1 = x_ref[...] + (gate(0) * pa + gate(1) * pb)
    x1_ref[...] = x1
    h2 = x1 * _rms(x1) * fg_ref[...]
    _store_token_tiles(h2_ref, h2)
    hh, hl = _split(h2)
    lg = _dot(hh, wrh_ref[...]) + (_dot(hl, wrh_ref[...]) + _dot(hh, wrl_ref[...])) + br_ref[...]

    lane = lax.broadcasted_iota(I32, (tm, LANES), 1)
    lane_f = lane.astype(F32)
    first = lambda hit: jnp.min(jnp.where(hit, lane_f, float(LANES)), axis=-1, keepdims=True).astype(I32)
    is_group = jnp.logical_and(lane >= ROUTE_GROUP_LANE, lane < ROUTE_GROUP_LANE + N_GROUPS)
    gl = jnp.where(is_group, lg, MASK_VALUE)
    gmax = jnp.max(gl, axis=-1, keepdims=True)
    gidx = first(gl == gmax) - ROUTE_GROUP_LANE
    g_top = 1.0 / jnp.sum(jnp.where(is_group, jnp.exp(gl - gmax), 0.0), axis=-1, keepdims=True)
    in_group = jnp.logical_and(lane < N_EXPERTS, jnp.right_shift(lane, 3) == gidx)
    el = jnp.where(in_group, lg, MASK_VALUE)
    e1 = jnp.max(el, axis=-1, keepdims=True)
    i1 = first(el == e1)
    el2 = jnp.where(lane == i1, MASK_VALUE, el)
    e2 = jnp.max(el2, axis=-1, keepdims=True)
    i2 = first(el2 == e2)
    t = jnp.exp(e2 - e1)
    w1 = g_top / (1.0 + t)
    w2 = g_top * t / (1.0 + t)

    hit1 = lane == i1
    hit2 = lane == i2
    oh = jnp.where(jnp.logical_or(hit1, hit2), 1.0, 0.0)
    ti = lax.broadcasted_iota(I32, (tm, tm), 0)
    tj = lax.broadcasted_iota(I32, (tm, tm), 1)
    before = _dot(jnp.where(ti > tj, 1.0, 0.0).astype(BF16), oh.astype(BF16)) + run_scr[...]
    rank1 = jnp.sum(jnp.where(hit1, before, 0.0), axis=-1, keepdims=True)
    rank2 = jnp.sum(jnp.where(hit2, before, 0.0), axis=-1, keepdims=True)
    run = run_scr[...] + jnp.sum(oh, axis=0, keepdims=True)
    run_scr[...] = run
    cnt_ref[...] = run.astype(I32)

    sel = lambda vals: functools.reduce(lambda acc, kv: jnp.where(lane == kv[0], kv[1], acc), enumerate(vals), 0)
    ri_ref[...] = sel([i1, i2, rank1.astype(I32), rank2.astype(I32)])
    rw_ref[...] = sel([w1, w2])


def _merge(x, oa, ob, gl, wts, tm):
    T = x.shape[0]
    row = lambda i: (i, 0)
    full = lambda a: pl.BlockSpec(a.shape, lambda i: (0,) * a.ndim)
    names = ["w_out_a", "w_out_b", "ffn_g", "wr_hi", "wr_lo", "b_r"]
    return pl.pallas_call(
        functools.partial(_merge_kernel, tm=tm),
        grid=(T // tm,),
        in_specs=[pl.BlockSpec((tm, D_MODEL), row), pl.BlockSpec((tm, MLA_HEADS * HEAD_PAD), row),
                  pl.BlockSpec((tm, RWKV_DIM), row), pl.BlockSpec((tm, GATE_COLS), row)]
                 + [full(wts[n]) for n in names],
        out_specs=[pl.BlockSpec((tm, D_MODEL), row), pl.BlockSpec((tm * ROW_TILE, LANES), row),
                   pl.BlockSpec((tm, LANES), row), pl.BlockSpec((tm, LANES), row),
                   pl.BlockSpec((1, LANES), lambda i: (0, 0))],
        out_shape=[jax.ShapeDtypeStruct((T, D_MODEL), F32), jax.ShapeDtypeStruct((T * ROW_TILE, LANES), F32),
                   jax.ShapeDtypeStruct((T, LANES), I32), jax.ShapeDtypeStruct((T, LANES), F32),
                   jax.ShapeDtypeStruct((1, LANES), I32)],
        scratch_shapes=[pltpu.VMEM((1, LANES), F32)],
        compiler_params=_cparams(("arbitrary",)),
        name="merge",
    )(x, oa, ob, gl, *[wts[n] for n in names])


def _dispatch_kernel(pend_ref, cnt_ref, dest_ref, h_ref, xs_ref, zero_scr, sems, *, tm):
    i = pl.program_id(0)
    tile = lambda ref, slot, rows=1: ref.at[pl.ds(slot * ROW_TILE, rows * ROW_TILE), :]

    @pl.when(i == 0)
    def _():
        zero_scr[...] = jnp.zeros_like(zero_scr)
        fill = lambda e: pltpu.make_async_copy(zero_scr, tile(xs_ref, pend_ref[e] - FFN_ROWS, FFN_ROWS), sems.at[1])
        for e in range(N_EXPERTS):
            pl.when(cnt_ref[e] > 0)(lambda e=e: fill(e).start())
        for e in range(N_EXPERTS):
            pl.when(cnt_ref[e] > 0)(lambda e=e: fill(e).wait())
        dead = lambda b: pltpu.make_async_copy(zero_scr, tile(xs_ref, b * FFN_ROWS, FFN_ROWS), sems.at[1])
        first_dead = pend_ref[N_EXPERTS - 1] // FFN_ROWS
        n_blocks = xs_ref.shape[0] // (FFN_ROWS * ROW_TILE)
        lax.fori_loop(first_dead, n_blocks, lambda b, c: (dead(b).start(), c)[1], 0)
        lax.fori_loop(first_dead, n_blocks, lambda b, c: (dead(b).wait(), c)[1], 0)

    sem = sems.at[0]

    def issue(t, carry):
        for s in range(TOP_K):
            d = dest_ref[0, 0, TOP_K * t + s]
            pltpu.make_async_copy(tile(h_ref, t), tile(xs_ref, d), sem).start()
        return carry

    lax.fori_loop(0, tm, issue, 0, unroll=DMA_ISSUE_UNROLL)
    for s in range(TOP_K):
        pltpu.make_async_copy(h_ref, tile(xs_ref, 0, tm), sem).wait()


def _dispatch(h2t, dest, pend, counts, n_slots, tm):
    T = h2t.shape[0] // ROW_TILE
    dest3 = dest.reshape(T // tm, 1, TOP_K * tm)
    grid_spec = pltpu.PrefetchScalarGridSpec(
        num_scalar_prefetch=2,
        grid=(T // tm,),
        in_specs=[pl.BlockSpec((1, 1, TOP_K * tm), lambda i, pe, cn: (i, 0, 0), memory_space=pltpu.SMEM),
                  pl.BlockSpec((tm * ROW_TILE, LANES), lambda i, pe, cn: (i, 0))],
        out_specs=pl.BlockSpec(memory_space=pl.ANY),
        scratch_shapes=[pltpu.VMEM((FFN_ROWS * ROW_TILE, LANES), F32), pltpu.SemaphoreType.DMA((2,))],
    )
    return pl.pallas_call(
        functools.partial(_dispatch_kernel, tm=tm),
        grid_spec=grid_spec,
        out_shape=jax.ShapeDtypeStruct((n_slots * ROW_TILE, LANES), F32),
        compiler_params=_cparams(("arbitrary",)),
        name="dispatch",
    )(pend, counts, dest3, h2t)


def _ffn_kernel(be_ref, nv_ref, x_ref, wg_ref, wu_ref, wd_ref, y_ref):
    del be_ref
    live = pl.program_id(0) < nv_ref[0]

    @pl.when(live)
    def _():
        x = _load_token_tiles(x_ref, FFN_ROWS).astype(BF16)
        gate = _dot(x, wg_ref[...])
        up = _dot(x, wu_ref[...])
        _store_token_tiles(y_ref, _dot((gate * _sigmoid(gate) * up).astype(BF16), wd_ref[...]))

    @pl.when(jnp.logical_not(live))
    def _():
        y_ref[...] = jnp.zeros_like(y_ref)


def _ffn(xs, block_e, n_live, wts):
    nb = xs.shape[0] // (FFN_ROWS * ROW_TILE)
    live_block = lambda i, be, nv: (jnp.minimum(i, nv[0] - 1), 0)
    grid_spec = pltpu.PrefetchScalarGridSpec(
        num_scalar_prefetch=2,
        grid=(nb,),
        in_specs=[pl.BlockSpec((FFN_ROWS * ROW_TILE, LANES), live_block),
                  pl.BlockSpec((None, D_MODEL, EXPERT_FF), lambda i, be, nv: (be[i], 0, 0)),
                  pl.BlockSpec((None, D_MODEL, EXPERT_FF), lambda i, be, nv: (be[i], 0, 0)),
                  pl.BlockSpec((None, EXPERT_FF, D_MODEL), lambda i, be, nv: (be[i], 0, 0))],
        out_specs=pl.BlockSpec((FFN_ROWS * ROW_TILE, LANES), lambda i, be, nv: (i, 0)),
    )
    return pl.pallas_call(
        _ffn_kernel,
        grid_spec=grid_spec,
        out_shape=jax.ShapeDtypeStruct(xs.shape, F32),
        compiler_params=_cparams(("arbitrary",)),
        name="ffn",
    )(block_e, n_live, xs, wts["w_eg"], wts["w_eu"], wts["w_ed"])


def _combine_kernel(dest_ref, dest_next_ref, x1_ref, rw_ref, yb_ref, o_ref, y_scr, sems, *, tm):
    i = pl.program_id(0)
    n = pl.num_programs(0)
    slot = i % 2
    tile = lambda ref, t, rows=1: ref.at[pl.ds(t * ROW_TILE, rows * ROW_TILE), :]

    def gather(dref, buf):
        def issue(t, carry):
            for s in range(TOP_K):
                d = dref[0, 0, TOP_K * t + s]
                pltpu.make_async_copy(tile(yb_ref, d), tile(y_scr.at[buf, s], t), sems.at[buf]).start()
            return carry

        lax.fori_loop(0, tm, issue, 0, unroll=DMA_ISSUE_UNROLL)

    pl.when(i == 0)(lambda: gather(dest_ref, 0))
    pl.when(i + 1 < n)(lambda: gather(dest_next_ref, 1 - slot))
    for s in range(TOP_K):
        pltpu.make_async_copy(tile(yb_ref, 0, tm), y_scr.at[slot, s], sems.at[slot]).wait()
    rw = rw_ref[...]
    y = [_load_token_tiles(y_scr.at[slot, s], tm) for s in range(TOP_K)]
    o_ref[...] = x1_ref[...] + (rw[:, 0:1] * y[0] + rw[:, 1:2] * y[1])


def _combine(x1, rw, yb, dest, tm):
    T = x1.shape[0]
    nt = T // tm
    dest3 = dest.reshape(nt, 1, TOP_K * tm)
    dest_spec = lambda step: pl.BlockSpec((1, 1, TOP_K * tm), lambda i: (step(i), 0, 0), memory_space=pltpu.SMEM)
    return pl.pallas_call(
        functools.partial(_combine_kernel, tm=tm),
        grid=(nt,),
        in_specs=[dest_spec(lambda i: i), dest_spec(lambda i: jnp.minimum(i + 1, nt - 1)),
                  pl.BlockSpec((tm, D_MODEL), lambda i: (i, 0)),
                  pl.BlockSpec((tm, LANES), lambda i: (i, 0)),
                  pl.BlockSpec(memory_space=pl.ANY)],
        out_specs=pl.BlockSpec((tm, D_MODEL), lambda i: (i, 0)),
        out_shape=jax.ShapeDtypeStruct((T, D_MODEL), F32),
        scratch_shapes=[pltpu.VMEM((2, TOP_K, tm * ROW_TILE, LANES), F32), pltpu.SemaphoreType.DMA((2,))],
        compiler_params=_cparams(("arbitrary",)),
        name="combine",
    )(dest3, dest3, x1, rw, yb)


def _moe(x1, h2, ri, rw, cnt, wts, tm):
    T = x1.shape[0]
    ids = ri[:, 0:TOP_K]
    ranks = ri[:, TOP_K:2 * TOP_K]
    counts = cnt[0, :N_EXPERTS]
    padded = (counts + FFN_ROWS - 1) // FFN_ROWS * FFN_ROWS
    pend = jnp.cumsum(padded)
    pstart = pend - padded
    dest = (pstart[ids] + ranks).astype(I32)
    nb = -(-T * TOP_K // FFN_ROWS) + N_EXPERTS
    block_pos = jnp.arange(nb, dtype=I32) * FFN_ROWS
    block_e = jnp.minimum(jnp.sum(pend[None, :] <= block_pos[:, None], axis=1), N_EXPERTS - 1).astype(I32)
    n_live = (pend[-1:] // FFN_ROWS).astype(I32)
    xs = _dispatch(h2, dest, pend.astype(I32), counts.astype(I32), nb * FFN_ROWS, min(DISPATCH_TOKENS, T))
    yb = _ffn(xs, block_e, n_live, wts)
    return _combine(x1, rw, yb, dest, tm)


def _rope_tables(pos):
    inv = 1.0 / (ROPE_BASE ** (jnp.arange(HALF_ROPE, dtype=F32) * (2.0 / QK_ROPE)))
    ang = pos.astype(F32)[:, None] * inv[None, :]
    cos, sin = jnp.cos(ang), jnp.sin(ang)
    n = pos.shape[0]
    pad = jnp.zeros((n, HEAD_PAD - QK_DIM), F32)
    ct = jnp.concatenate([jnp.ones((n, QK_NOPE), F32), cos, cos, pad], axis=1)
    st = jnp.concatenate([jnp.zeros((n, QK_NOPE), F32), -sin, sin, pad], axis=1)
    kt = jnp.concatenate([cos, sin], axis=1)
    return ct, st, kt


def _score_bound(q_head_norm_g, k_head_norm_g):
    return (1.02 * QK_DIM * ATTN_SCALE * math.log2(math.e)) * (
        jnp.max(jnp.abs(q_head_norm_g)) * jnp.max(jnp.abs(k_head_norm_g)))


def _prep_weights(norm_mix_g, w_in, q_norm_g, w_uq, kv_norm_g, w_ukv, q_head_norm_g, k_head_norm_g,
                  rwkv_mu, rwkv_w0, rwkv_w2, rwkv_a0, rwkv_a2, rwkv_g2, rwkv_k_k, rwkv_k_a, rwkv_r_k,
                  rwkv_lnx_g, rwkv_lnx_b, w_out, norm_ffn_g, w_router_group, b_router_group,
                  w_router_expert, b_router_expert, w_expert_gate, w_expert_up, w_expert_down):
    row = lambda a: a.reshape(1, -1).astype(F32)
    head_pad = lambda a: jnp.pad(a, ((0, 0), (0, 0), (0, HEAD_PAD - a.shape[-1])))
    w = {}
    w["mix_g"] = row(norm_mix_g)
    w["w_in"] = jnp.concatenate([w_in[:, :MLA_COLS], jnp.zeros((D_MODEL, LANES - QK_ROPE), F32),
                                 w_in[:, MLA_COLS:]], axis=1).astype(BF16)
    w["q_g"] = row(q_norm_g)
    w["kv_g"] = row(kv_norm_g)
    w["wqa"] = head_pad(w_uq).reshape(Q_LORA, MLA_HEADS * HEAD_PAD).astype(BF16)
    partner = jnp.concatenate([jnp.zeros_like(w_uq[..., :QK_NOPE]), w_uq[..., QK_NOPE + HALF_ROPE:],
                               w_uq[..., QK_NOPE:QK_NOPE + HALF_ROPE]], axis=-1)
    w["wqb"] = head_pad(partner).reshape(Q_LORA, MLA_HEADS * HEAD_PAD).astype(BF16)
    pad_g = lambda g: jnp.pad(g, (0, HEAD_PAD - QK_DIM)).reshape(1, HEAD_PAD).astype(F32)
    w["qh_g"] = pad_g(q_head_norm_g) * (ATTN_SCALE * math.log2(math.e))
    w["kh_g"] = pad_g(k_head_norm_g)
    lane_one = lambda lane: (jnp.arange(HEAD_PAD) == lane).astype(F32).reshape(1, HEAD_PAD)
    w["score_bound"] = _score_bound(q_head_norm_g, k_head_norm_g)
    w["q_one"] = lane_one(QK_DIM)
    w["k_shift"] = -w["score_bound"] * lane_one(QK_DIM)
    w["v_one"] = jnp.tile(lane_one(V_DIM), (1, MLA_HEADS))
    w["wk"] = head_pad(w_ukv[..., :QK_NOPE]).reshape(KV_LORA, MLA_HEADS * HEAD_PAD).astype(BF16)
    w["wv"] = head_pad(w_ukv[..., QK_NOPE:]).reshape(KV_LORA, MLA_HEADS * HEAD_PAD).astype(BF16)
    w["place"] = jnp.pad(jnp.eye(QK_ROPE, dtype=F32), ((0, 0), (QK_NOPE, HEAD_PAD - QK_DIM))).astype(BF16)
    w["mu"] = row(rwkv_mu)
    w["w0"] = row(rwkv_w0)
    w["w2"] = jnp.concatenate([rwkv_w2, jnp.zeros((A_LORA, RWKV_DIM), F32)], axis=0).astype(BF16)
    w["a0"] = row(rwkv_a0)
    w["a2"] = jnp.concatenate([jnp.zeros((W_LORA, RWKV_DIM), F32), rwkv_a2], axis=0).astype(BF16)
    w["g2"] = rwkv_g2.astype(BF16)
    w["k_k"] = row(rwkv_k_k)
    w["k_a"] = row(rwkv_k_a)
    w["r_k"] = row(rwkv_r_k)
    w["lnx_g"] = row(rwkv_lnx_g)
    w["lnx_b"] = row(rwkv_lnx_b)
    head_of = jnp.arange(RWKV_DIM) // RWKV_HEAD
    w["ones_bd"] = (head_of[:, None] == head_of[None, :]).astype(BF16)
    w["w_out_a"] = jnp.pad(w_out[0].reshape(MLA_HEADS, V_DIM, D_MODEL), ((0, 0), (0, HEAD_PAD - V_DIM), (0, 0))
                           ).reshape(MLA_HEADS * HEAD_PAD, D_MODEL).astype(BF16)
    w["w_out_b"] = w_out[1].astype(BF16)
    w["ffn_g"] = row(norm_ffn_g)
    wr = jnp.concatenate([w_router_expert.reshape(D_MODEL, N_EXPERTS), w_router_group,
                          jnp.zeros((D_MODEL, LANES - N_EXPERTS - N_GROUPS), F32)], axis=1)
    w["wr_hi"] = wr.astype(BF16)
    w["wr_lo"] = (wr - w["wr_hi"].astype(F32)).astype(BF16)
    w["b_r"] = jnp.concatenate([b_router_expert.reshape(-1), b_router_group,
                                jnp.zeros((LANES - N_EXPERTS - N_GROUPS,), F32)]).reshape(1, LANES)
    w["w_eg"] = w_expert_gate.astype(BF16)
    w["w_eu"] = w_expert_up.astype(BF16)
    w["w_ed"] = w_expert_down.astype(BF16)
    return w


def _round_up(n, m):
    return -(-n // m) * m


def _layer(x, pos, lat_past, kpe_past, wkv0, shift0, wts):
    B, L, _ = x.shape
    T = B * L
    tm = min(256, T)
    ct, st, kt = _rope_tables(pos)
    if L % tm:
        reps = tm // L
        ct, st, kt = (jnp.tile(t, (reps, 1)) for t in (ct, st, kt))
    q, lat, kpe, z, gl = _inproj(x.reshape(T, D_MODEL), (ct, st, kt), wts, tm)
    lat3 = lat.reshape(B, L, KV_LORA)
    kpe3 = kpe.reshape(B, L, QK_ROPE)
    if lat_past is None:
        n_past, lk, lat_all, kpe_all = 0, L, lat3, kpe3
        tq, tk = min(ATTN_QUERY_TILE, L), min(ATTN_KEY_TILE, L)
        diag = True
    else:
        n_past = lat_past.shape[1]
        lk_valid = n_past + L
        lk = _round_up(lk_valid, LANES)
        padk = lambda a: jnp.pad(a, ((0, 0), (0, lk - lk_valid), (0, 0)))
        lat_all = padk(jnp.concatenate([lat_past, lat3], axis=1))
        kpe_all = padk(jnp.concatenate([kpe_past, kpe3], axis=1))
        tq, tk, diag = L, lk, False
    k, v = _kvprep(lat_all.reshape(B * lk, KV_LORA), kpe_all.reshape(B * lk, QK_ROPE), wts, min(512, B * lk))
    oa = _attention(q.reshape(B, L, -1), k.reshape(B, lk, -1), v.reshape(B, lk, -1),
                    wts["score_bound"] <= MAX_DIRECT_EXP2_SHIFT, tq=tq, tk=tk, n_past=n_past,
                    lk_valid=n_past + L, diag_tiles=diag)
    z3 = z.reshape(B, L, RWKV_COLS)
    ob, s_new = _rwkv(z3, shift0, wkv0, wts, min(SCAN_CHUNK, L))
    x1, h2, ri, rw, cnt = _merge(x.reshape(T, D_MODEL), oa.reshape(T, -1), ob.reshape(T, -1), gl, wts, tm)
    y = _moe(x1, h2, ri, rw, cnt, wts, tm)
    return y.reshape(B, L, D_MODEL), lat3, kpe3, s_new, z3[:, L - 1:, :]


def kernel(x_prompt, x_sample, cache_kv_latent, cache_k_rope, state_wkv, state_shift, norm_mix_g, w_in, q_norm_g, w_uq, kv_norm_g, w_ukv, q_head_norm_g, k_head_norm_g, rwkv_mu, rwkv_w0, rwkv_w2, rwkv_a0, rwkv_a2, rwkv_g2, rwkv_k_k, rwkv_k_a, rwkv_r_k, rwkv_lnx_g, rwkv_lnx_b, w_out, norm_ffn_g, w_router_group, b_router_group, w_router_expert, b_router_expert, w_expert_gate, w_expert_up, w_expert_down):
    weights = (norm_mix_g, w_in, q_norm_g, w_uq, kv_norm_g, w_ukv, q_head_norm_g, k_head_norm_g,
               rwkv_mu, rwkv_w0, rwkv_w2, rwkv_a0, rwkv_a2, rwkv_g2, rwkv_k_k, rwkv_k_a, rwkv_r_k,
               rwkv_lnx_g, rwkv_lnx_b, w_out, norm_ffn_g, w_router_group, b_router_group,
               w_router_expert, b_router_expert, w_expert_gate, w_expert_up, w_expert_down)
    depth = w_in.shape[0]
    bp, lp, _ = x_prompt.shape
    n_past = cache_kv_latent.shape[2]
    pos_p = jnp.arange(lp, dtype=I32)
    pos_s = n_past + jnp.arange(x_sample.shape[1], dtype=I32)
    wkv_zero = jnp.zeros((bp, RWKV_HEADS, RWKV_HEAD, RWKV_HEAD), F32)
    shift_zero = jnp.zeros((bp, 1, RWKV_COLS), F32)
    xp, xs = x_prompt, x_sample
    outs_p, outs_s = [], []
    for l in range(depth):
        wts = _prep_weights(*(wt[l] for wt in weights))
        xp, *rest_p = _layer(xp, pos_p, None, None, wkv_zero, shift_zero, wts)
        xs, *rest_s = _layer(xs, pos_s, cache_kv_latent[l], cache_k_rope[l], state_wkv[l], state_shift[l], wts)
        outs_p.append(rest_p)
        outs_s.append(rest_s)
    stack = lambda outs, i: jnp.stack([o[i] for o in outs], 0)
    return (xp, xs, stack(outs_p, 0), stack(outs_p, 1), stack(outs_p, 2), stack(outs_p, 3),
            stack(outs_s, 0), stack(outs_s, 1), stack(outs_s, 2), stack(outs_s, 3))
```

```python
import functools
import math

import jax
import jax.numpy as jnp
from jax import lax
from jax.experimental import pallas as pl
from jax.experimental.pallas import tpu as pltpu

F32 = jnp.float32
BF16 = jnp.bfloat16
I32 = jnp.int32

D_MODEL = 1024
CHUNK = 64
RMS_EPS = 1e-6
MASK_VALUE = -1e30
MLA_HEADS = 8
QK_NOPE = 64
QK_ROPE = 32
QK_DIM = QK_NOPE + QK_ROPE
V_DIM = 64
Q_LORA = 256
KV_LORA = 128
ROPE_BASE = 10000.0
ATTN_SCALE = QK_DIM ** -0.5
RWKV_HEADS = 8
RWKV_HEAD = 64
RWKV_DIM = RWKV_HEADS * RWKV_HEAD
W_LORA = 64
A_LORA = 64
G_LORA = 128
LN_X_EPS = 64e-5
MLA_COLS = Q_LORA + KV_LORA + QK_ROPE
RWKV_COLS = 3 * RWKV_DIM + W_LORA + A_LORA + G_LORA
GATE_COLS = 2 * D_MODEL
N_GROUPS = 4
EXPERTS_PER_GROUP = 8
N_EXPERTS = N_GROUPS * EXPERTS_PER_GROUP
EXPERT_FF = 256
TOP_K = 2

LANES = 128
ROW_TILE = D_MODEL // LANES
DMA_ISSUE_UNROLL = 4
DISPATCH_TOKENS = 512
DISPATCH_SLOTS = 3
HEAD_PAD = LANES
HALF_ROPE = QK_ROPE // 2
SEG_Q = 0
SEG_KV = SEG_Q + Q_LORA
SEG_PE = SEG_KV + KV_LORA
SEG_Z = SEG_PE + LANES
SEG_G = SEG_Z + RWKV_COLS
IN_COLS_PAD = SEG_G + GATE_COLS
ROUTE_GROUP_LANE = N_EXPERTS
SCAN_CHUNK = 128
SCAN_HEAD_GROUP = 8
FFN_ROWS = 256
ATTN_QUERY_TILE = 512
ATTN_KEY_TILE = 512
MAX_DIRECT_EXP2_SHIFT = 40.0
VMEM_LIMIT = 48 * 1024 * 1024


def _cparams(sem, vmem=VMEM_LIMIT):
    return pltpu.CompilerParams(dimension_semantics=sem, vmem_limit_bytes=vmem)


def _dot(a, b):
    return jnp.dot(a, b, preferred_element_type=F32)


def _dot_nt(a, b):
    return lax.dot_general(a, b, (((1,), (1,)), ((), ())), preferred_element_type=F32)


def _dot_tn(a, b):
    return lax.dot_general(a, b, (((0,), (0,)), ((), ())), preferred_element_type=F32)


def _split(x):
    hi = x.astype(BF16)
    lo = (x - hi.astype(F32)).astype(BF16)
    return hi, lo


def _dot2_exact_rhs(a, b_bf16):
    ah, al = _split(a)
    return _dot(ah, b_bf16) + _dot(al, b_bf16)


def _sigmoid(x):
    return 1.0 / (1.0 + jnp.exp(-x))


def _rms(x, n=None):
    n = x.shape[-1] if n is None else n
    return lax.rsqrt(jnp.sum(x * x, axis=-1, keepdims=True) * (1.0 / n) + RMS_EPS)


def _store_token_tiles(ref, x):
    n = x.shape[0]
    for c in range(ROW_TILE):
        ref[pl.ds(c, n, stride=ROW_TILE), :] = x[:, c * LANES:(c + 1) * LANES]


def _load_token_tiles(ref, n):
    return jnp.concatenate([ref[pl.ds(c, n, stride=ROW_TILE), :] for c in range(ROW_TILE)], axis=1)


def _inproj_kernel(x_ref, g_ref, w_ref, qg_ref, wqa_ref, wqb_ref, ct_ref, st_ref, kt_ref, kvg_ref, qhg_ref, qone_ref,
                   q_ref, lat_ref, kpe_ref, z_ref, gl_ref):
    x = x_ref[...]
    h = (x * _rms(x) * g_ref[...]).astype(BF16)
    cq = _dot(h, w_ref[:, SEG_Q:SEG_KV])
    ckv = _dot(h, w_ref[:, SEG_KV:SEG_PE])
    pe = _dot(h, w_ref[:, SEG_PE:SEG_Z])
    z_ref[...] = _dot(h, w_ref[:, SEG_Z:SEG_G])
    gl_ref[...] = _dot(h, w_ref[:, SEG_G:IN_COLS_PAD]).astype(gl_ref.dtype)
    lat_ref[...] = ckv * _rms(ckv) * kvg_ref[...]
    x1 = pe[:, :HALF_ROPE]
    x2 = pe[:, HALF_ROPE:QK_ROPE]
    c = kt_ref[:, :HALF_ROPE]
    s = kt_ref[:, HALF_ROPE:QK_ROPE]
    kpe_ref[...] = jnp.concatenate([x1 * c - x2 * s, x1 * s + x2 * c], axis=-1)
    cqn = (cq * _rms(cq) * qg_ref[...]).astype(BF16)
    ct = ct_ref[...]
    st = st_ref[...]
    qhg = qhg_ref[...]
    qone = qone_ref[...]
    for hd in range(MLA_HEADS):
        sl = slice(hd * HEAD_PAD, (hd + 1) * HEAD_PAD)
        xh = _dot(cqn, wqa_ref[:, sl]) * ct + _dot(cqn, wqb_ref[:, sl]) * st
        q_ref[:, sl] = (xh * _rms(xh, QK_DIM) * qhg + qone).astype(BF16)


def _inproj(x, tabs, wts, tm):
    T = x.shape[0]
    ct, st, kt = tabs
    nt = ct.shape[0] // tm
    row = lambda i: (i, 0)
    tab = lambda i: (i % nt, 0)
    fixed = lambda i: (0, 0)
    full = lambda a: pl.BlockSpec(a.shape, fixed)
    return pl.pallas_call(
        _inproj_kernel,
        grid=(T // tm,),
        in_specs=[pl.BlockSpec((tm, D_MODEL), row), full(wts["mix_g"]), full(wts["w_in"]), full(wts["q_g"]),
                  full(wts["wqa"]), full(wts["wqb"]),
                  pl.BlockSpec((tm, HEAD_PAD), tab), pl.BlockSpec((tm, HEAD_PAD), tab),
                  pl.BlockSpec((tm, QK_ROPE), tab), full(wts["kv_g"]), full(wts["qh_g"]), full(wts["q_one"])],
        out_specs=[pl.BlockSpec((tm, MLA_HEADS * HEAD_PAD), row), pl.BlockSpec((tm, KV_LORA), row),
                   pl.BlockSpec((tm, QK_ROPE), row), pl.BlockSpec((tm, RWKV_COLS), row),
                   pl.BlockSpec((tm, GATE_COLS), row)],
        out_shape=[jax.ShapeDtypeStruct((T, MLA_HEADS * HEAD_PAD), BF16), jax.ShapeDtypeStruct((T, KV_LORA), F32),
                   jax.ShapeDtypeStruct((T, QK_ROPE), F32), jax.ShapeDtypeStruct((T, RWKV_COLS), F32),
                   jax.ShapeDtypeStruct((T, GATE_COLS), BF16)],
        compiler_params=_cparams(("arbitrary",)),
        name="inproj",
    )(x, wts["mix_g"], wts["w_in"], wts["q_g"], wts["wqa"], wts["wqb"], ct, st, kt, wts["kv_g"], wts["qh_g"],
      wts["q_one"])


def _kvprep_kernel(lat_ref, kpe_ref, wk_ref, wv_ref, place_ref, kg_ref, kshift_ref, vone_ref, k_ref, v_ref):
    lat = lat_ref[...].astype(BF16)
    rot = _dot2_exact_rhs(kpe_ref[...], place_ref[...])
    kg = kg_ref[...]
    kshift = kshift_ref[...]
    for hd in range(MLA_HEADS):
        sl = slice(hd * HEAD_PAD, (hd + 1) * HEAD_PAD)
        kh = _dot(lat, wk_ref[:, sl]) + rot
        k_ref[:, sl] = (kh * _rms(kh, QK_DIM) * kg + kshift).astype(BF16)
    v_ref[...] = (_dot(lat, wv_ref[...]) + vone_ref[...]).astype(BF16)


def _kvprep(lat, kpe, wts, tm):
    T = lat.shape[0]
    row = lambda i: (i, 0)
    fixed = lambda i: (0, 0)
    full = lambda a: pl.BlockSpec(a.shape, fixed)
    return pl.pallas_call(
        _kvprep_kernel,
        grid=(T // tm,),
        in_specs=[pl.BlockSpec((tm, KV_LORA), row), pl.BlockSpec((tm, QK_ROPE), row), full(wts["wk"]),
                  full(wts["wv"]), full(wts["place"]), full(wts["kh_g"]), full(wts["k_shift"]),
                  full(wts["v_one"])],
        out_specs=[pl.BlockSpec((tm, MLA_HEADS * HEAD_PAD), row), pl.BlockSpec((tm, MLA_HEADS * HEAD_PAD), row)],
        out_shape=[jax.ShapeDtypeStruct((T, MLA_HEADS * HEAD_PAD), BF16),
                   jax.ShapeDtypeStruct((T, MLA_HEADS * HEAD_PAD), BF16)],
        compiler_params=_cparams(("arbitrary",)),
        name="kvprep",
    )(lat, kpe, wts["wk"], wts["wv"], wts["place"], wts["kh_g"], wts["k_shift"], wts["v_one"])


def _attn_kernel(q_ref, k_ref, v_ref, o_ref, *, tq, tk, n_past, lk_valid, diag_tiles, bounded):
    i = pl.program_id(2)
    n_full = i * (tq // tk) if diag_tiles else 0
    head_lanes = [slice(hh * HEAD_PAD, (hh + 1) * HEAD_PAD) for hh in range(2)]
    qs = [q_ref[:, hs] for hs in head_lanes]

    def scores(hh, j, masked):
        ks = pl.multiple_of(j * tk, tk)
        s = _dot_nt(qs[hh], k_ref[pl.ds(ks, tk), head_lanes[hh]])
        if masked:
            qpos = n_past + i * tq + lax.broadcasted_iota(I32, (tq, tk), 0)
            kpos = j * tk + lax.broadcasted_iota(I32, (tq, tk), 1)
            vis = jnp.logical_and(jnp.right_shift(kpos, 6) <= jnp.right_shift(qpos, 6), kpos < lk_valid)
            s = jnp.where(vis, s, MASK_VALUE)
        return s, v_ref[pl.ds(ks, tk), head_lanes[hh]]

    if bounded:
        def head_step(hh, j, acc, masked):
            s, v = scores(hh, j, masked)
            return acc + _dot(jnp.exp2(s).astype(BF16), v)

        init = jnp.zeros((tq, HEAD_PAD), F32)
    else:
        def head_step(hh, j, carry, masked):
            m, acc = carry
            s, v = scores(hh, j, masked)
            m_new = jnp.maximum(m, jnp.max(s, axis=-1, keepdims=True))
            return m_new, jnp.exp2(m - m_new) * acc + _dot(jnp.exp2(s - m_new).astype(BF16), v)

        init = (jnp.full((tq, 1), MASK_VALUE, F32), jnp.zeros((tq, HEAD_PAD), F32))

    step = lambda j, carry, masked: tuple(head_step(hh, j, carry[hh], masked) for hh in range(2))
    carry = (init, init)
    if diag_tiles:
        carry = lax.fori_loop(0, n_full, functools.partial(step, masked=False), carry)
    for d in range(tq // tk if diag_tiles else 1):
        carry = step(n_full + d, carry, True)
    for hh in range(2):
        acc = carry[hh] if bounded else carry[hh][1]
        o_ref[:, head_lanes[hh]] = (acc / acc[:, V_DIM:V_DIM + 1]).astype(o_ref.dtype)


def _attention(q, k, v, bound_ok, *, tq, tk, n_past, lk_valid, diag_tiles):
    B, L, _ = q.shape
    Lk = k.shape[1]

    def run(bounded):
        kern = functools.partial(_attn_kernel, tq=tq, tk=tk, n_past=n_past, lk_valid=lk_valid,
                                 diag_tiles=diag_tiles, bounded=bounded)
        return pl.pallas_call(
            kern,
            grid=(B, MLA_HEADS // 2, L // tq),
            in_specs=[pl.BlockSpec((None, tq, 2 * HEAD_PAD), lambda b, p, i: (b, i, p)),
                      pl.BlockSpec((None, Lk, 2 * HEAD_PAD), lambda b, p, i: (b, 0, p)),
                      pl.BlockSpec((None, Lk, 2 * HEAD_PAD), lambda b, p, i: (b, 0, p))],
            out_specs=pl.BlockSpec((None, tq, 2 * HEAD_PAD), lambda b, p, i: (b, i, p)),
            out_shape=jax.ShapeDtypeStruct((B, L, MLA_HEADS * HEAD_PAD), BF16),
            compiler_params=_cparams(("arbitrary", "arbitrary", "arbitrary")),
            name="attn_bounded" if bounded else "attn_running_max",
        )(q, k, v)

    return lax.cond(bound_ok, lambda: run(True), lambda: run(False))


def _rwkv_kernel(z_ref, shift_ref, s0_ref, mu_ref, w0_ref, w2_ref, a0_ref, a2_ref, g2_ref, kk_ref, ka_ref, rk_ref,
                 lng_ref, lnb_ref, ones_ref, o_ref, sout_ref, s_scr, prev_scr, *, C):
    c = pl.program_id(1)

    @pl.when(c == 0)
    def _():
        s_scr[...] = s0_ref[...]
        prev_scr[...] = shift_ref[...]

    zf = z_ref[...]
    row = lax.broadcasted_iota(I32, (C, 1), 0)
    pv = jnp.where(row == 0, prev_scr[...], pltpu.roll(zf, 1, axis=0))
    prev_scr[...] = zf[C - 1:C, :]
    zm = zf + (pv - zf) * mu_ref[...]
    H = RWKV_DIM
    r = zm[:, 0:H]
    k = zm[:, H:2 * H]
    v = zm[:, 2 * H:3 * H]
    xwa = zm[:, 3 * H:3 * H + W_LORA + A_LORA]
    xg = zm[:, 3 * H + W_LORA + A_LORA:]
    lw = (-math.exp(-0.5)) * _sigmoid(w0_ref[...] + _dot(jnp.tanh(xwa).astype(BF16), w2_ref[...]))
    a = _sigmoid(a0_ref[...] + _dot(xwa.astype(BF16), a2_ref[...]))
    g = _dot(_sigmoid(xg).astype(BF16), g2_ref[...])
    ones_bd = ones_ref[...]
    kk = k * kk_ref[...]
    kkn = kk * lax.rsqrt(jnp.maximum(_dot2_exact_rhs(kk * kk, ones_bd), 1e-24))
    k2 = k * (1.0 + (a - 1.0) * ka_ref[...])
    head_sum = lambda x: _dot(x.astype(BF16), ones_bd)
    bonus = head_sum(r * k2 * rk_ref[...]) * v

    ti = lax.broadcasted_iota(I32, (C, C), 0)
    tj = lax.broadcasted_iota(I32, (C, C), 1)
    incl = ti >= tj
    strict = ti > tj
    tri = jnp.where(incl, 1.0, 0.0).astype(BF16)
    p1 = lw.astype(BF16)
    r1 = lw - p1.astype(F32)
    p2 = r1.astype(BF16)
    p3 = (r1 - p2.astype(F32)).astype(BF16)
    cs = _dot(tri, p1) + (_dot(tri, p2) + _dot(tri, p3))
    w_incl = jnp.exp(cs)
    w_inv = jnp.exp(-cs)
    at = -kkn * jnp.exp(cs - lw)
    rt = r * w_incl
    bt = kkn * a * w_inv
    kt = k2 * w_inv
    wc = w_incl[C - 1:C, :]
    bh = bt * wc
    kh = kt * wc
    eye = jnp.where(ti == tj, 1.0, 0.0).astype(F32)

    per_head = lambda x: [x[:, hd * RWKV_HEAD:(hd + 1) * RWKV_HEAD] for hd in range(RWKV_HEADS)]
    rows = lambda xs: jnp.concatenate(xs, axis=0)
    side = lambda xs: jnp.concatenate(xs, axis=1)
    hi_lo = lambda x: tuple(per_head(p) for p in _split(x))
    cast = lambda x: per_head(x.astype(BF16))
    at_b, rt_b, bt_b, kt_b = cast(at), cast(rt), cast(bt), cast(kt)
    (v_hi, v_lo), (bh_hi, bh_lo), (kh_hi, kh_lo) = hi_lo(v), hi_lo(bh), hi_lo(kh)
    wc_h = per_head(wc)

    def dot_hl(a, b_hi, b_lo):
        ah, al = _split(a)
        return _dot(side([ah, al, ah]), rows([b_hi, b_hi, b_lo]))

    def head_group(heads):
        s_all = {h: s_scr[h] for h in heads}
        s_b = {h: s_all[h].astype(BF16) for h in heads}
        pr = {h: _dot_nt(rows([at_b[h], rt_b[h]]), rows([bt_b[h], kt_b[h]])) for h in heads}
        a_ab = {h: jnp.where(strict, pr[h][:C, :C], 0.0) for h in heads}
        a_ak = {h: jnp.where(strict, pr[h][:C, C:], 0.0).astype(BF16) for h in heads}
        a_rb = {h: jnp.where(incl, pr[h][C:, :C], 0.0).astype(BF16) for h in heads}
        a_rk = {h: jnp.where(incl, pr[h][C:, C:], 0.0).astype(BF16) for h in heads}
        pw = a_ab
        tinv = {h: eye + pw[h] for h in heads}
        n = 2
        while n < C:
            pw_b = {h: pw[h].astype(BF16) for h in heads}
            pw = {h: _dot(pw_b[h], pw_b[h]) for h in heads}
            tinv = {h: tinv[h] + _dot(tinv[h].astype(BF16), pw[h].astype(BF16)) for h in heads}
            n *= 2
        resid = {h: (eye - tinv[h]) + dot_hl(a_ab[h], *_split(tinv[h])) for h in heads}
        tinv = {h: tinv[h] + _dot(tinv[h].astype(BF16), resid[h].astype(BF16)) for h in heads}
        x = {h: _dot_nt(at_b[h], s_b[h]) + _dot(a_ak[h], v_hi[h]) for h in heads}
        u = {h: _split(_dot(tinv[h].astype(BF16), x[h].astype(BF16))) for h in heads}
        ys = [_dot_nt(rt_b[h], s_b[h]) + _dot(side([a_rb[h], a_rk[h]]), rows([u[h][0], v_hi[h]])) for h in heads]
        for h in heads:
            u_hi, u_lo = u[h]
            upd = _dot_tn(rows([u_hi, v_hi[h], u_lo, v_lo[h], u_hi, v_hi[h]]),
                          rows([bh_hi[h], kh_hi[h], bh_hi[h], kh_hi[h], bh_lo[h], kh_lo[h]]))
            s_scr[h] = s_all[h] * wc_h[h] + upd
        return ys

    ys = []
    for first in range(0, RWKV_HEADS, SCAN_HEAD_GROUP):
        ys += head_group(range(first, first + SCAN_HEAD_GROUP))
    y = jnp.concatenate(ys, axis=-1)
    inv_n = 1.0 / RWKV_HEAD
    mean = head_sum(y) * inv_n
    d = y - mean
    var = head_sum(d * d) * inv_n
    yn = d * lax.rsqrt(var + LN_X_EPS) * lng_ref[...] + lnb_ref[...]
    o_ref[...] = ((yn + bonus) * g).astype(o_ref.dtype)

    @pl.when(c == pl.num_programs(1) - 1)
    def _():
        sout_ref[...] = s_scr[...]


def _rwkv(z, shift0, s0, wts, C):
    B, L, _ = z.shape
    fixed = lambda b, c: (0, 0)
    full = lambda a: pl.BlockSpec(a.shape, fixed)
    names = ["mu", "w0", "w2", "a0", "a2", "g2", "k_k", "k_a", "r_k", "lnx_g", "lnx_b", "ones_bd"]
    return pl.pallas_call(
        functools.partial(_rwkv_kernel, C=C),
        grid=(B, L // C),
        in_specs=[pl.BlockSpec((None, C, RWKV_COLS), lambda b, c: (b, c, 0)),
                  pl.BlockSpec((None, 1, RWKV_COLS), lambda b, c: (b, 0, 0)),
                  pl.BlockSpec((None, RWKV_HEADS, RWKV_HEAD, RWKV_HEAD), lambda b, c: (b, 0, 0, 0))]
                 + [full(wts[n]) for n in names],
        out_specs=[pl.BlockSpec((None, C, RWKV_DIM), lambda b, c: (b, c, 0)),
                   pl.BlockSpec((None, RWKV_HEADS, RWKV_HEAD, RWKV_HEAD), lambda b, c: (b, 0, 0, 0))],
        out_shape=[jax.ShapeDtypeStruct((B, L, RWKV_DIM), BF16),
                   jax.ShapeDtypeStruct((B, RWKV_HEADS, RWKV_HEAD, RWKV_HEAD), F32)],
        scratch_shapes=[pltpu.VMEM((RWKV_HEADS, RWKV_HEAD, RWKV_HEAD), F32), pltpu.VMEM((1, RWKV_COLS), F32)],
        compiler_params=_cparams(("arbitrary", "arbitrary")),
        name="rwkv",
    )(z, shift0, s0, *[wts[n] for n in names])


def _merge_kernel(x_ref, oa_ref, ob_ref, gl_ref, woa_ref, wob_ref, fg_ref, wrh_ref, wrl_ref, br_ref,
                  x1_ref, h2_ref, ri_ref, rw_ref, cnt_ref, run_scr, *, tm):
    @pl.when(pl.program_id(0) == 0)
    def _():
        run_scr[...] = jnp.zeros_like(run_scr)

    pa = _dot(oa_ref[...], woa_ref[...])
    pb = _dot(ob_ref[...], wob_ref[...])
    gate = lambda n: _sigmoid(gl_ref[:, n * D_MODEL:(n + 1) * D_MODEL].astype(F32))
    x1 = x_ref[...] + (gate(0) * pa + gate(1) * pb)
    x1_ref[...] = x1
    h2 = x1 * _rms(x1) * fg_ref[...]
    _store_token_tiles(h2_ref, h2)
    hh, hl = _split(h2)
    lg = _dot(hh, wrh_ref[...]) + (_dot(hl, wrh_ref[...]) + _dot(hh, wrl_ref[...])) + br_ref[...]

    lane = lax.broadcasted_iota(I32, (tm, LANES), 1)
    lane_f = lane.astype(F32)
    first = lambda hit: jnp.min(jnp.where(hit, lane_f, float(LANES)), axis=-1, keepdims=True).astype(I32)
    is_group = jnp.logical_and(lane >= ROUTE_GROUP_LANE, lane < ROUTE_GROUP_LANE + N_GROUPS)
    gl = jnp.where(is_group, lg, MASK_VALUE)
    gmax = jnp.max(gl, axis=-1, keepdims=True)
    gidx = first(gl == gmax) - ROUTE_GROUP_LANE
    g_top = 1.0 / jnp.sum(jnp.where(is_group, jnp.exp(gl - gmax), 0.0), axis=-1, keepdims=True)
    in_group = jnp.logical_and(lane < N_EXPERTS, jnp.right_shift(lane, 3) == gidx)
    el = jnp.where(in_group, lg, MASK_VALUE)
    e1 = jnp.max(el, axis=-1, keepdims=True)
    i1 = first(el == e1)
    el2 = jnp.where(lane == i1, MASK_VALUE, el)
    e2 = jnp.max(el2, axis=-1, keepdims=True)
    i2 = first(el2 == e2)
    t = jnp.exp(e2 - e1)
    w1 = g_top / (1.0 + t)
    w2 = g_top * t / (1.0 + t)

    hit1 = lane == i1
    hit2 = lane == i2
    oh = jnp.where(jnp.logical_or(hit1, hit2), 1.0, 0.0)
    ti = lax.broadcasted_iota(I32, (tm, tm), 0)
    tj = lax.broadcasted_iota(I32, (tm, tm), 1)
    before = _dot(jnp.where(ti > tj, 1.0, 0.0).astype(BF16), oh.astype(BF16)) + run_scr[...]
    rank1 = jnp.sum(jnp.where(hit1, before, 0.0), axis=-1, keepdims=True)
    rank2 = jnp.sum(jnp.where(hit2, before, 0.0), axis=-1, keepdims=True)
    run = run_scr[...] + jnp.sum(oh, axis=0, keepdims=True)
    run_scr[...] = run
    cnt_ref[...] = run.astype(I32)

    sel = lambda vals: functools.reduce(lambda acc, kv: jnp.where(lane == kv[0], kv[1], acc), enumerate(vals), 0)
    ri_ref[...] = sel([i1, i2, rank1.astype(I32), rank2.astype(I32)])
    rw_ref[...] = sel([w1, w2])


def _merge(x, oa, ob, gl, wts, tm):
    T = x.shape[0]
    row = lambda i: (i, 0)
    full = lambda a: pl.BlockSpec(a.shape, lambda i: (0,) * a.ndim)
    names = ["w_out_a", "w_out_b", "ffn_g", "wr_hi", "wr_lo", "b_r"]
    return pl.pallas_call(
        functools.partial(_merge_kernel, tm=tm),
        grid=(T // tm,),
        in_specs=[pl.BlockSpec((tm, D_MODEL), row), pl.BlockSpec((tm, MLA_HEADS * HEAD_PAD), row),
                  pl.BlockSpec((tm, RWKV_DIM), row), pl.BlockSpec((tm, GATE_COLS), row)]
                 + [full(wts[n]) for n in names],
        out_specs=[pl.BlockSpec((tm, D_MODEL), row), pl.BlockSpec((tm * ROW_TILE, LANES), row),
                   pl.BlockSpec((tm, LANES), row), pl.BlockSpec((tm, LANES), row),
                   pl.BlockSpec((1, LANES), lambda i: (0, 0))],
        out_shape=[jax.ShapeDtypeStruct((T, D_MODEL), F32), jax.ShapeDtypeStruct((T * ROW_TILE, LANES), F32),
                   jax.ShapeDtypeStruct((T, LANES), I32), jax.ShapeDtypeStruct((T, LANES), F32),
                   jax.ShapeDtypeStruct((1, LANES), I32)],
        scratch_shapes=[pltpu.VMEM((1, LANES), F32)],
        compiler_params=_cparams(("arbitrary",)),
        name="merge",
    )(x, oa, ob, gl, *[wts[n] for n in names])


def _dispatch_kernel(pend_ref, cnt_ref, dest_ref, h_ref, xs_ref, zero_scr, buf, sem_fill, sem_in, sem_out, *,
                     tm, n_steps):
    i = pl.program_id(0)
    tile = lambda ref, slot, rows=1: ref.at[pl.ds(slot * ROW_TILE, rows * ROW_TILE), :]
    load = lambda step, slot: pltpu.make_async_copy(tile(h_ref, step * tm, tm), buf.at[slot], sem_in.at[slot])
    step_drain = lambda slot: pltpu.make_async_copy(buf.at[slot], tile(xs_ref, 0, tm), sem_out.at[slot])

    @pl.when(i == 0)
    def _():
        zero_scr[...] = jnp.zeros_like(zero_scr)
        fill = lambda e: pltpu.make_async_copy(zero_scr, tile(xs_ref, pend_ref[e] - FFN_ROWS, FFN_ROWS), sem_fill)
        for e in range(N_EXPERTS):
            pl.when(cnt_ref[e] > 0)(lambda e=e: fill(e).start())
        for e in range(N_EXPERTS):
            pl.when(cnt_ref[e] > 0)(lambda e=e: fill(e).wait())
        dead = lambda b: pltpu.make_async_copy(zero_scr, tile(xs_ref, b * FFN_ROWS, FFN_ROWS), sem_fill)
        first_dead = pend_ref[N_EXPERTS - 1] // FFN_ROWS
        n_blocks = xs_ref.shape[0] // (FFN_ROWS * ROW_TILE)
        lax.fori_loop(first_dead, n_blocks, lambda b, c: (dead(b).start(), c)[1], 0)
        lax.fori_loop(first_dead, n_blocks, lambda b, c: (dead(b).wait(), c)[1], 0)
        load(0, 0).start()

    slot = i % DISPATCH_SLOTS
    nxt = (i + 1) % DISPATCH_SLOTS

    @pl.when(i >= DISPATCH_SLOTS - 1)
    def _():
        for s in range(TOP_K):
            step_drain(nxt).wait()

    pl.when(i + 1 < n_steps)(lambda: load(i + 1, nxt).start())
    load(i, slot).wait()

    def issue(t, carry):
        for s in range(TOP_K):
            d = dest_ref[0, 0, TOP_K * t + s]
            pltpu.make_async_copy(tile(buf.at[slot], t), tile(xs_ref, d), sem_out.at[slot]).start()
        return carry

    lax.fori_loop(0, tm, issue, 0, unroll=DMA_ISSUE_UNROLL)

    @pl.when(i == n_steps - 1)
    def _():
        for back in range(min(DISPATCH_SLOTS - 1, n_steps)):
            for s in range(TOP_K):
                step_drain((i - back) % DISPATCH_SLOTS).wait()


def _dispatch(h2t, dest, pend, counts, n_slots, tm):
    T = h2t.shape[0] // ROW_TILE
    n_steps = T // tm
    dest3 = dest.reshape(n_steps, 1, TOP_K * tm)
    grid_spec = pltpu.PrefetchScalarGridSpec(
        num_scalar_prefetch=2,
        grid=(n_steps,),
        in_specs=[pl.BlockSpec((1, 1, TOP_K * tm), lambda i, pe, cn: (i, 0, 0), memory_space=pltpu.SMEM),
                  pl.BlockSpec(memory_space=pl.ANY)],
        out_specs=pl.BlockSpec(memory_space=pl.ANY),
        scratch_shapes=[pltpu.VMEM((FFN_ROWS * ROW_TILE, LANES), F32),
                        pltpu.VMEM((DISPATCH_SLOTS, tm * ROW_TILE, LANES), F32),
                        pltpu.SemaphoreType.DMA(()), pltpu.SemaphoreType.DMA((DISPATCH_SLOTS,)),
                        pltpu.SemaphoreType.DMA((DISPATCH_SLOTS,))],
    )
    return pl.pallas_call(
        functools.partial(_dispatch_kernel, tm=tm, n_steps=n_steps),
        grid_spec=grid_spec,
        out_shape=jax.ShapeDtypeStruct((n_slots * ROW_TILE, LANES), F32),
        compiler_params=_cparams(("arbitrary",)),
        name="dispatch",
    )(pend, counts, dest3, h2t)


def _ffn_kernel(be_ref, nv_ref, x_ref, wg_ref, wu_ref, wd_ref, y_ref):
    del be_ref
    live = pl.program_id(0) < nv_ref[0]

    @pl.when(live)
    def _():
        x = _load_token_tiles(x_ref, FFN_ROWS).astype(BF16)
        gate = _dot(x, wg_ref[...])
        up = _dot(x, wu_ref[...])
        _store_token_tiles(y_ref, _dot((gate * _sigmoid(gate) * up).astype(BF16), wd_ref[...]))

    @pl.when(jnp.logical_not(live))
    def _():
        y_ref[...] = jnp.zeros_like(y_ref)


def _ffn(xs, block_e, n_live, wts):
    nb = xs.shape[0] // (FFN_ROWS * ROW_TILE)
    live_block = lambda i, be, nv: (jnp.minimum(i, nv[0] - 1), 0)
    grid_spec = pltpu.PrefetchScalarGridSpec(
        num_scalar_prefetch=2,
        grid=(nb,),
        in_specs=[pl.BlockSpec((FFN_ROWS * ROW_TILE, LANES), live_block),
                  pl.BlockSpec((None, D_MODEL, EXPERT_FF), lambda i, be, nv: (be[i], 0, 0)),
                  pl.BlockSpec((None, D_MODEL, EXPERT_FF), lambda i, be, nv: (be[i], 0, 0)),
                  pl.BlockSpec((None, EXPERT_FF, D_MODEL), lambda i, be, nv: (be[i], 0, 0))],
        out_specs=pl.BlockSpec((FFN_ROWS * ROW_TILE, LANES), lambda i, be, nv: (i, 0)),
    )
    return pl.pallas_call(
        _ffn_kernel,
        grid_spec=grid_spec,
        out_shape=jax.ShapeDtypeStruct(xs.shape, F32),
        compiler_params=_cparams(("arbitrary",)),
        name="ffn",
    )(block_e, n_live, xs, wts["w_eg"], wts["w_eu"], wts["w_ed"])


def _combine_kernel(dest_ref, dest_next_ref, x1_ref, rw_ref, yb_ref, o_ref, y_scr, sems, *, tm):
    i = pl.program_id(0)
    n = pl.num_programs(0)
    slot = i % 2
    tile = lambda ref, t, rows=1: ref.at[pl.ds(t * ROW_TILE, rows * ROW_TILE), :]

    def gather(dref, buf):
        def issue(t, carry):
            for s in range(TOP_K):
                d = dref[0, 0, TOP_K * t + s]
                pltpu.make_async_copy(tile(yb_ref, d), tile(y_scr.at[buf, s], t), sems.at[buf]).start()
            return carry

        lax.fori_loop(0, tm, issue, 0, unroll=DMA_ISSUE_UNROLL)

    pl.when(i == 0)(lambda: gather(dest_ref, 0))
    pl.when(i + 1 < n)(lambda: gather(dest_next_ref, 1 - slot))
    for s in range(TOP_K):
        pltpu.make_async_copy(tile(yb_ref, 0, tm), y_scr.at[slot, s], sems.at[slot]).wait()
    rw = rw_ref[...]
    y = [_load_token_tiles(y_scr.at[slot, s], tm) for s in range(TOP_K)]
    o_ref[...] = x1_ref[...] + (rw[:, 0:1] * y[0] + rw[:, 1:2] * y[1])


def _combine(x1, rw, yb, dest, tm):
    T = x1.shape[0]
    nt = T // tm
    dest3 = dest.reshape(nt, 1, TOP_K * tm)
    dest_spec = lambda step: pl.BlockSpec((1, 1, TOP_K * tm), lambda i: (step(i), 0, 0), memory_space=pltpu.SMEM)
    return pl.pallas_call(
        functools.partial(_combine_kernel, tm=tm),
        grid=(nt,),
        in_specs=[dest_spec(lambda i: i), dest_spec(lambda i: jnp.minimum(i + 1, nt - 1)),
                  pl.BlockSpec((tm, D_MODEL), lambda i: (i, 0)),
                  pl.BlockSpec((tm, LANES), lambda i: (i, 0)),
                  pl.BlockSpec(memory_space=pl.ANY)],
        out_specs=pl.BlockSpec((tm, D_MODEL), lambda i: (i, 0)),
        out_shape=jax.ShapeDtypeStruct((T, D_MODEL), F32),
        scratch_shapes=[pltpu.VMEM((2, TOP_K, tm * ROW_TILE, LANES), F32), pltpu.SemaphoreType.DMA((2,))],
        compiler_params=_cparams(("arbitrary",)),
        name="combine",
    )(dest3, dest3, x1, rw, yb)


def _moe(x1, h2, ri, rw, cnt, wts, tm):
    T = x1.shape[0]
    ids = ri[:, 0:TOP_K]
    ranks = ri[:, TOP_K:2 * TOP_K]
    counts = cnt[0, :N_EXPERTS]
    padded = (counts + FFN_ROWS - 1) // FFN_ROWS * FFN_ROWS
    pend = jnp.cumsum(padded)
    pstart = pend - padded
    dest = (pstart[ids] + ranks).astype(I32)
    nb = -(-T * TOP_K // FFN_ROWS) + N_EXPERTS
    block_pos = jnp.arange(nb, dtype=I32) * FFN_ROWS
    block_e = jnp.minimum(jnp.sum(pend[None, :] <= block_pos[:, None], axis=1), N_EXPERTS - 1).astype(I32)
    n_live = (pend[-1:] // FFN_ROWS).astype(I32)
    xs = _dispatch(h2, dest, pend.astype(I32), counts.astype(I32), nb * FFN_ROWS, min(DISPATCH_TOKENS, T))
    yb = _ffn(xs, block_e, n_live, wts)
    return _combine(x1, rw, yb, dest, tm)


def _rope_tables(pos):
    inv = 1.0 / (ROPE_BASE ** (jnp.arange(HALF_ROPE, dtype=F32) * (2.0 / QK_ROPE)))
    ang = pos.astype(F32)[:, None] * inv[None, :]
    cos, sin = jnp.cos(ang), jnp.sin(ang)
    n = pos.shape[0]
    pad = jnp.zeros((n, HEAD_PAD - QK_DIM), F32)
    ct = jnp.concatenate([jnp.ones((n, QK_NOPE), F32), cos, cos, pad], axis=1)
    st = jnp.concatenate([jnp.zeros((n, QK_NOPE), F32), -sin, sin, pad], axis=1)
    kt = jnp.concatenate([cos, sin], axis=1)
    return ct, st, kt


def _score_bound(q_head_norm_g, k_head_norm_g):
    return (1.02 * QK_DIM * ATTN_SCALE * math.log2(math.e)) * (
        jnp.max(jnp.abs(q_head_norm_g)) * jnp.max(jnp.abs(k_head_norm_g)))


def _prep_weights(norm_mix_g, w_in, q_norm_g, w_uq, kv_norm_g, w_ukv, q_head_norm_g, k_head_norm_g,
                  rwkv_mu, rwkv_w0, rwkv_w2, rwkv_a0, rwkv_a2, rwkv_g2, rwkv_k_k, rwkv_k_a, rwkv_r_k,
                  rwkv_lnx_g, rwkv_lnx_b, w_out, norm_ffn_g, w_router_group, b_router_group,
                  w_router_expert, b_router_expert, w_expert_gate, w_expert_up, w_expert_down):
    row = lambda a: a.reshape(1, -1).astype(F32)
    head_pad = lambda a: jnp.pad(a, ((0, 0), (0, 0), (0, HEAD_PAD - a.shape[-1])))
    w = {}
    w["mix_g"] = row(norm_mix_g)
    w["w_in"] = jnp.concatenate([w_in[:, :MLA_COLS], jnp.zeros((D_MODEL, LANES - QK_ROPE), F32),
                                 w_in[:, MLA_COLS:]], axis=1).astype(BF16)
    w["q_g"] = row(q_norm_g)
    w["kv_g"] = row(kv_norm_g)
    w["wqa"] = head_pad(w_uq).reshape(Q_LORA, MLA_HEADS * HEAD_PAD).astype(BF16)
    partner = jnp.concatenate([jnp.zeros_like(w_uq[..., :QK_NOPE]), w_uq[..., QK_NOPE + HALF_ROPE:],
                               w_uq[..., QK_NOPE:QK_NOPE + HALF_ROPE]], axis=-1)
    w["wqb"] = head_pad(partner).reshape(Q_LORA, MLA_HEADS * HEAD_PAD).astype(BF16)
    pad_g = lambda g: jnp.pad(g, (0, HEAD_PAD - QK_DIM)).reshape(1, HEAD_PAD).astype(F32)
    w["qh_g"] = pad_g(q_head_norm_g) * (ATTN_SCALE * math.log2(math.e))
    w["kh_g"] = pad_g(k_head_norm_g)
    lane_one = lambda lane: (jnp.arange(HEAD_PAD) == lane).astype(F32).reshape(1, HEAD_PAD)
    w["score_bound"] = _score_bound(q_head_norm_g, k_head_norm_g)
    w["q_one"] = lane_one(QK_DIM)
    w["k_shift"] = -w["score_bound"] * lane_one(QK_DIM)
    w["v_one"] = jnp.tile(lane_one(V_DIM), (1, MLA_HEADS))
    w["wk"] = head_pad(w_ukv[..., :QK_NOPE]).reshape(KV_LORA, MLA_HEADS * HEAD_PAD).astype(BF16)
    w["wv"] = head_pad(w_ukv[..., QK_NOPE:]).reshape(KV_LORA, MLA_HEADS * HEAD_PAD).astype(BF16)
    w["place"] = jnp.pad(jnp.eye(QK_ROPE, dtype=F32), ((0, 0), (QK_NOPE, HEAD_PAD - QK_DIM))).astype(BF16)
    w["mu"] = row(rwkv_mu)
    w["w0"] = row(rwkv_w0)
    w["w2"] = jnp.concatenate([rwkv_w2, jnp.zeros((A_LORA, RWKV_DIM), F32)], axis=0).astype(BF16)
    w["a0"] = row(rwkv_a0)
    w["a2"] = jnp.concatenate([jnp.zeros((W_LORA, RWKV_DIM), F32), rwkv_a2], axis=0).astype(BF16)
    w["g2"] = rwkv_g2.astype(BF16)
    w["k_k"] = row(rwkv_k_k)
    w["k_a"] = row(rwkv_k_a)
    w["r_k"] = row(rwkv_r_k)
    w["lnx_g"] = row(rwkv_lnx_g)
    w["lnx_b"] = row(rwkv_lnx_b)
    head_of = jnp.arange(RWKV_DIM) // RWKV_HEAD
    w["ones_bd"] = (head_of[:, None] == head_of[None, :]).astype(BF16)
    w["w_out_a"] = jnp.pad(w_out[0].reshape(MLA_HEADS, V_DIM, D_MODEL), ((0, 0), (0, HEAD_PAD - V_DIM), (0, 0))
                           ).reshape(MLA_HEADS * HEAD_PAD, D_MODEL).astype(BF16)
    w["w_out_b"] = w_out[1].astype(BF16)
    w["ffn_g"] = row(norm_ffn_g)
    wr = jnp.concatenate([w_router_expert.reshape(D_MODEL, N_EXPERTS), w_router_group,
                          jnp.zeros((D_MODEL, LANES - N_EXPERTS - N_GROUPS), F32)], axis=1)
    w["wr_hi"] = wr.astype(BF16)
    w["wr_lo"] = (wr - w["wr_hi"].astype(F32)).astype(BF16)
    w["b_r"] = jnp.concatenate([b_router_expert.reshape(-1), b_router_group,
                                jnp.zeros((LANES - N_EXPERTS - N_GROUPS,), F32)]).reshape(1, LANES)
    w["w_eg"] = w_expert_gate.astype(BF16)
    w["w_eu"] = w_expert_up.astype(BF16)
    w["w_ed"] = w_expert_down.astype(BF16)
    return w


def _round_up(n, m):
    return -(-n // m) * m


def _layer(x, pos, lat_past, kpe_past, wkv0, shift0, wts):
    B, L, _ = x.shape
    T = B * L
    tm = min(256, T)
    ct, st, kt = _rope_tables(pos)
    if L % tm:
        reps = tm // L
        ct, st, kt = (jnp.tile(t, (reps, 1)) for t in (ct, st, kt))
    q, lat, kpe, z, gl = _inproj(x.reshape(T, D_MODEL), (ct, st, kt), wts, tm)
    lat3 = lat.reshape(B, L, KV_LORA)
    kpe3 = kpe.reshape(B, L, QK_ROPE)
    if lat_past is None:
        n_past, lk, lat_all, kpe_all = 0, L, lat3, kpe3
        tq, tk = min(ATTN_QUERY_TILE, L), min(ATTN_KEY_TILE, L)
        diag = True
    else:
        n_past = lat_past.shape[1]
        lk_valid = n_past + L
        lk = _round_up(lk_valid, LANES)
        padk = lambda a: jnp.pad(a, ((0, 0), (0, lk - lk_valid), (0, 0)))
        lat_all = padk(jnp.concatenate([lat_past, lat3], axis=1))
        kpe_all = padk(jnp.concatenate([kpe_past, kpe3], axis=1))
        tq, tk, diag = L, lk, False
    k, v = _kvprep(lat_all.reshape(B * lk, KV_LORA), kpe_all.reshape(B * lk, QK_ROPE), wts, min(512, B * lk))
    oa = _attention(q.reshape(B, L, -1), k.reshape(B, lk, -1), v.reshape(B, lk, -1),
                    wts["score_bound"] <= MAX_DIRECT_EXP2_SHIFT, tq=tq, tk=tk, n_past=n_past,
                    lk_valid=n_past + L, diag_tiles=diag)
    z3 = z.reshape(B, L, RWKV_COLS)
    ob, s_new = _rwkv(z3, shift0, wkv0, wts, min(SCAN_CHUNK, L))
    x1, h2, ri, rw, cnt = _merge(x.reshape(T, D_MODEL), oa.reshape(T, -1), ob.reshape(T, -1), gl, wts, tm)
    y = _moe(x1, h2, ri, rw, cnt, wts, tm)
    return y.reshape(B, L, D_MODEL), lat3, kpe3, s_new, z3[:, L - 1:, :]


def kernel(x_prompt, x_sample, cache_kv_latent, cache_k_rope, state_wkv, state_shift, norm_mix_g, w_in, q_norm_g, w_uq, kv_norm_g, w_ukv, q_head_norm_g, k_head_norm_g, rwkv_mu, rwkv_w0, rwkv_w2, rwkv_a0, rwkv_a2, rwkv_g2, rwkv_k_k, rwkv_k_a, rwkv_r_k, rwkv_lnx_g, rwkv_lnx_b, w_out, norm_ffn_g, w_router_group, b_router_group, w_router_expert, b_router_expert, w_expert_gate, w_expert_up, w_expert_down):
    weights = (norm_mix_g, w_in, q_norm_g, w_uq, kv_norm_g, w_ukv, q_head_norm_g, k_head_norm_g,
               rwkv_mu, rwkv_w0, rwkv_w2, rwkv_a0, rwkv_a2, rwkv_g2, rwkv_k_k, rwkv_k_a, rwkv_r_k,
               rwkv_lnx_g, rwkv_lnx_b, w_out, norm_ffn_g, w_router_group, b_router_group,
               w_router_expert, b_router_expert, w_expert_gate, w_expert_up, w_expert_down)
    depth = w_in.shape[0]
    bp, lp, _ = x_prompt.shape
    n_past = cache_kv_latent.shape[2]
    pos_p = jnp.arange(lp, dtype=I32)
    pos_s = n_past + jnp.arange(x_sample.shape[1], dtype=I32)
    wkv_zero = jnp.zeros((bp, RWKV_HEADS, RWKV_HEAD, RWKV_HEAD), F32)
    shift_zero = jnp.zeros((bp, 1, RWKV_COLS), F32)
    xp, xs = x_prompt, x_sample
    outs_p, outs_s = [], []
    for l in range(depth):
        wts = _prep_weights(*(wt[l] for wt in weights))
        xp, *rest_p = _layer(xp, pos_p, None, None, wkv_zero, shift_zero, wts)
        xs, *rest_s = _layer(xs, pos_s, cache_kv_latent[l], cache_k_rope[l], state_wkv[l], state_shift[l], wts)
        outs_p.append(rest_p)
        outs_s.append(rest_s)
    stack = lambda outs, i: jnp.stack([o[i] for o in outs], 0)
    return (xp, xs, stack(outs_p, 0), stack(outs_p, 1), stack(outs_p, 2), stack(outs_p, 3),
            stack(outs_s, 0), stack(outs_s, 1), stack(outs_s, 2), stack(outs_s, 3))
```

```python
import functools
import math

import jax
import jax.numpy as jnp
from jax import lax
from jax.experimental import pallas as pl
from jax.experimental.pallas import tpu as pltpu

F32 = jnp.float32
BF16 = jnp.bfloat16
I32 = jnp.int32

D_MODEL = 1024
CHUNK = 64
RMS_EPS = 1e-6
MASK_VALUE = -1e30
MLA_HEADS = 8
QK_NOPE = 64
QK_ROPE = 32
QK_DIM = QK_NOPE + QK_ROPE
V_DIM = 64
Q_LORA = 256
KV_LORA = 128
ROPE_BASE = 10000.0
ATTN_SCALE = QK_DIM ** -0.5
RWKV_HEADS = 8
RWKV_HEAD = 64
RWKV_DIM = RWKV_HEADS * RWKV_HEAD
W_LORA = 64
A_LORA = 64
G_LORA = 128
LN_X_EPS = 64e-5
MLA_COLS = Q_LORA + KV_LORA + QK_ROPE
RWKV_COLS = 3 * RWKV_DIM + W_LORA + A_LORA + G_LORA
GATE_COLS = 2 * D_MODEL
N_GROUPS = 4
EXPERTS_PER_GROUP = 8
N_EXPERTS = N_GROUPS * EXPERTS_PER_GROUP
EXPERT_FF = 256
TOP_K = 2

LANES = 128
ROW_TILE = D_MODEL // LANES
DMA_ISSUE_UNROLL = 4
DISPATCH_TOKENS = 512
DISPATCH_SLOTS = 3
HEAD_PAD = LANES
HALF_ROPE = QK_ROPE // 2
SEG_Q = 0
SEG_KV = SEG_Q + Q_LORA
SEG_PE = SEG_KV + KV_LORA
SEG_Z = SEG_PE + LANES
SEG_G = SEG_Z + RWKV_COLS
IN_COLS_PAD = SEG_G + GATE_COLS
ROUTE_GROUP_LANE = N_EXPERTS
ROUTE_FIELDS = 8
SCAN_CHUNK = 128
SCAN_HEAD_GROUP = 8
FFN_ROWS = 256
ATTN_QUERY_TILE = 512
ATTN_KEY_TILE = 512
MAX_DIRECT_EXP2_SHIFT = 40.0
VMEM_LIMIT = 48 * 1024 * 1024


def _cparams(sem, vmem=VMEM_LIMIT):
    return pltpu.CompilerParams(dimension_semantics=sem, vmem_limit_bytes=vmem)


def _dot(a, b):
    return jnp.dot(a, b, preferred_element_type=F32)


def _dot_nt(a, b):
    return lax.dot_general(a, b, (((1,), (1,)), ((), ())), preferred_element_type=F32)


def _dot_tn(a, b):
    return lax.dot_general(a, b, (((0,), (0,)), ((), ())), preferred_element_type=F32)


def _split(x):
    hi = x.astype(BF16)
    lo = (x - hi.astype(F32)).astype(BF16)
    return hi, lo


def _dot2_exact_rhs(a, b_bf16):
    ah, al = _split(a)
    return _dot(ah, b_bf16) + _dot(al, b_bf16)


def _sigmoid(x):
    return 1.0 / (1.0 + jnp.exp(-x))


def _rms(x, n=None):
    n = x.shape[-1] if n is None else n
    return lax.rsqrt(jnp.sum(x * x, axis=-1, keepdims=True) * (1.0 / n) + RMS_EPS)


def _store_token_tiles(ref, x):
    n = x.shape[0]
    for c in range(ROW_TILE):
        ref[pl.ds(c, n, stride=ROW_TILE), :] = x[:, c * LANES:(c + 1) * LANES]


def _load_token_tiles(ref, n):
    return jnp.concatenate([ref[pl.ds(c, n, stride=ROW_TILE), :] for c in range(ROW_TILE)], axis=1)


def _inproj_kernel(x_ref, g_ref, w_ref, qg_ref, wqa_ref, wqb_ref, ct_ref, st_ref, kt_ref, kvg_ref, qhg_ref, qone_ref,
                   q_ref, lat_ref, kpe_ref, z_ref, gl_ref):
    x = x_ref[...]
    h = (x * _rms(x) * g_ref[...]).astype(BF16)
    cq = _dot(h, w_ref[:, SEG_Q:SEG_KV])
    ckv = _dot(h, w_ref[:, SEG_KV:SEG_PE])
    pe = _dot(h, w_ref[:, SEG_PE:SEG_Z])
    z_ref[...] = _dot(h, w_ref[:, SEG_Z:SEG_G])
    gl_ref[...] = _dot(h, w_ref[:, SEG_G:IN_COLS_PAD]).astype(gl_ref.dtype)
    lat_ref[...] = ckv * _rms(ckv) * kvg_ref[...]
    x1 = pe[:, :HALF_ROPE]
    x2 = pe[:, HALF_ROPE:QK_ROPE]
    c = kt_ref[:, :HALF_ROPE]
    s = kt_ref[:, HALF_ROPE:QK_ROPE]
    kpe_ref[...] = jnp.concatenate([x1 * c - x2 * s, x1 * s + x2 * c], axis=-1)
    cqn = (cq * _rms(cq) * qg_ref[...]).astype(BF16)
    ct = ct_ref[...]
    st = st_ref[...]
    qhg = qhg_ref[...]
    qone = qone_ref[...]
    for hd in range(MLA_HEADS):
        sl = slice(hd * HEAD_PAD, (hd + 1) * HEAD_PAD)
        xh = _dot(cqn, wqa_ref[:, sl]) * ct + _dot(cqn, wqb_ref[:, sl]) * st
        q_ref[:, sl] = (xh * _rms(xh, QK_DIM) * qhg + qone).astype(BF16)


def _inproj(x, tabs, wts, tm):
    T = x.shape[0]
    ct, st, kt = tabs
    nt = ct.shape[0] // tm
    row = lambda i: (i, 0)
    tab = lambda i: (i % nt, 0)
    fixed = lambda i: (0, 0)
    full = lambda a: pl.BlockSpec(a.shape, fixed)
    return pl.pallas_call(
        _inproj_kernel,
        grid=(T // tm,),
        in_specs=[pl.BlockSpec((tm, D_MODEL), row), full(wts["mix_g"]), full(wts["w_in"]), full(wts["q_g"]),
                  full(wts["wqa"]), full(wts["wqb"]),
                  pl.BlockSpec((tm, HEAD_PAD), tab), pl.BlockSpec((tm, HEAD_PAD), tab),
                  pl.BlockSpec((tm, QK_ROPE), tab), full(wts["kv_g"]), full(wts["qh_g"]), full(wts["q_one"])],
        out_specs=[pl.BlockSpec((tm, MLA_HEADS * HEAD_PAD), row), pl.BlockSpec((tm, KV_LORA), row),
                   pl.BlockSpec((tm, QK_ROPE), row), pl.BlockSpec((tm, RWKV_COLS), row),
                   pl.BlockSpec((tm, GATE_COLS), row)],
        out_shape=[jax.ShapeDtypeStruct((T, MLA_HEADS * HEAD_PAD), BF16), jax.ShapeDtypeStruct((T, KV_LORA), F32),
                   jax.ShapeDtypeStruct((T, QK_ROPE), F32), jax.ShapeDtypeStruct((T, RWKV_COLS), F32),
                   jax.ShapeDtypeStruct((T, GATE_COLS), BF16)],
        compiler_params=_cparams(("arbitrary",)),
        name="inproj",
    )(x, wts["mix_g"], wts["w_in"], wts["q_g"], wts["wqa"], wts["wqb"], ct, st, kt, wts["kv_g"], wts["qh_g"],
      wts["q_one"])


def _kvprep_kernel(lat_ref, kpe_ref, wk_ref, wv_ref, place_ref, kg_ref, kshift_ref, vone_ref, k_ref, v_ref):
    lat = lat_ref[...].astype(BF16)
    rot = _dot2_exact_rhs(kpe_ref[...], place_ref[...])
    kg = kg_ref[...]
    kshift = kshift_ref[...]
    for hd in range(MLA_HEADS):
        sl = slice(hd * HEAD_PAD, (hd + 1) * HEAD_PAD)
        kh = _dot(lat, wk_ref[:, sl]) + rot
        k_ref[:, sl] = (kh * _rms(kh, QK_DIM) * kg + kshift).astype(BF16)
    v_ref[...] = (_dot(lat, wv_ref[...]) + vone_ref[...]).astype(BF16)


def _kvprep(lat, kpe, wts, tm):
    T = lat.shape[0]
    row = lambda i: (i, 0)
    fixed = lambda i: (0, 0)
    full = lambda a: pl.BlockSpec(a.shape, fixed)
    return pl.pallas_call(
        _kvprep_kernel,
        grid=(T // tm,),
        in_specs=[pl.BlockSpec((tm, KV_LORA), row), pl.BlockSpec((tm, QK_ROPE), row), full(wts["wk"]),
                  full(wts["wv"]), full(wts["place"]), full(wts["kh_g"]), full(wts["k_shift"]),
                  full(wts["v_one"])],
        out_specs=[pl.BlockSpec((tm, MLA_HEADS * HEAD_PAD), row), pl.BlockSpec((tm, MLA_HEADS * HEAD_PAD), row)],
        out_shape=[jax.ShapeDtypeStruct((T, MLA_HEADS * HEAD_PAD), BF16),
                   jax.ShapeDtypeStruct((T, MLA_HEADS * HEAD_PAD), BF16)],
        compiler_params=_cparams(("arbitrary",)),
        name="kvprep",
    )(lat, kpe, wts["wk"], wts["wv"], wts["place"], wts["kh_g"], wts["k_shift"], wts["v_one"])


def _attn_kernel(q_ref, k_ref, v_ref, o_ref, *, tq, tk, n_past, lk_valid, diag_tiles, bounded):
    i = pl.program_id(2)
    n_full = i * (tq // tk) if diag_tiles else 0
    head_lanes = [slice(hh * HEAD_PAD, (hh + 1) * HEAD_PAD) for hh in range(2)]
    qs = [q_ref[:, hs] for hs in head_lanes]

    def scores(hh, j, masked):
        ks = pl.multiple_of(j * tk, tk)
        s = _dot_nt(qs[hh], k_ref[pl.ds(ks, tk), head_lanes[hh]])
        if masked:
            qpos = n_past + i * tq + lax.broadcasted_iota(I32, (tq, tk), 0)
            kpos = j * tk + lax.broadcasted_iota(I32, (tq, tk), 1)
            vis = jnp.logical_and(jnp.right_shift(kpos, 6) <= jnp.right_shift(qpos, 6), kpos < lk_valid)
            s = jnp.where(vis, s, MASK_VALUE)
        return s, v_ref[pl.ds(ks, tk), head_lanes[hh]]

    if bounded:
        def head_step(hh, j, acc, masked):
            s, v = scores(hh, j, masked)
            return acc + _dot(jnp.exp2(s).astype(BF16), v)

        init = jnp.zeros((tq, HEAD_PAD), F32)
    else:
        def head_step(hh, j, carry, masked):
            m, acc = carry
            s, v = scores(hh, j, masked)
            m_new = jnp.maximum(m, jnp.max(s, axis=-1, keepdims=True))
            return m_new, jnp.exp2(m - m_new) * acc + _dot(jnp.exp2(s - m_new).astype(BF16), v)

        init = (jnp.full((tq, 1), MASK_VALUE, F32), jnp.zeros((tq, HEAD_PAD), F32))

    step = lambda j, carry, masked: tuple(head_step(hh, j, carry[hh], masked) for hh in range(2))
    carry = (init, init)
    if diag_tiles:
        carry = lax.fori_loop(0, n_full, functools.partial(step, masked=False), carry)
    for d in range(tq // tk if diag_tiles else 1):
        carry = step(n_full + d, carry, True)
    for hh in range(2):
        acc = carry[hh] if bounded else carry[hh][1]
        o_ref[:, head_lanes[hh]] = (acc / acc[:, V_DIM:V_DIM + 1]).astype(o_ref.dtype)


def _attention(q, k, v, bound_ok, *, tq, tk, n_past, lk_valid, diag_tiles):
    B, L, _ = q.shape
    Lk = k.shape[1]

    def run(bounded):
        kern = functools.partial(_attn_kernel, tq=tq, tk=tk, n_past=n_past, lk_valid=lk_valid,
                                 diag_tiles=diag_tiles, bounded=bounded)
        return pl.pallas_call(
            kern,
            grid=(B, MLA_HEADS // 2, L // tq),
            in_specs=[pl.BlockSpec((None, tq, 2 * HEAD_PAD), lambda b, p, i: (b, i, p)),
                      pl.BlockSpec((None, Lk, 2 * HEAD_PAD), lambda b, p, i: (b, 0, p)),
                      pl.BlockSpec((None, Lk, 2 * HEAD_PAD), lambda b, p, i: (b, 0, p))],
            out_specs=pl.BlockSpec((None, tq, 2 * HEAD_PAD), lambda b, p, i: (b, i, p)),
            out_shape=jax.ShapeDtypeStruct((B, L, MLA_HEADS * HEAD_PAD), BF16),
            compiler_params=_cparams(("arbitrary", "arbitrary", "arbitrary")),
            name="attn_bounded" if bounded else "attn_running_max",
        )(q, k, v)

    return lax.cond(bound_ok, lambda: run(True), lambda: run(False))


def _rwkv_kernel(z_ref, shift_ref, s0_ref, mu_ref, w0_ref, w2_ref, a0_ref, a2_ref, g2_ref, kk_ref, ka_ref, rk_ref,
                 lng_ref, lnb_ref, ones_ref, o_ref, sout_ref, s_scr, prev_scr, *, C):
    c = pl.program_id(1)

    @pl.when(c == 0)
    def _():
        s_scr[...] = s0_ref[...]
        prev_scr[...] = shift_ref[...]

    zf = z_ref[...]
    row = lax.broadcasted_iota(I32, (C, 1), 0)
    pv = jnp.where(row == 0, prev_scr[...], pltpu.roll(zf, 1, axis=0))
    prev_scr[...] = zf[C - 1:C, :]
    zm = zf + (pv - zf) * mu_ref[...]
    H = RWKV_DIM
    r = zm[:, 0:H]
    k = zm[:, H:2 * H]
    v = zm[:, 2 * H:3 * H]
    xwa = zm[:, 3 * H:3 * H + W_LORA + A_LORA]
    xg = zm[:, 3 * H + W_LORA + A_LORA:]
    lw = (-math.exp(-0.5)) * _sigmoid(w0_ref[...] + _dot(jnp.tanh(xwa).astype(BF16), w2_ref[...]))
    a = _sigmoid(a0_ref[...] + _dot(xwa.astype(BF16), a2_ref[...]))
    g = _dot(_sigmoid(xg).astype(BF16), g2_ref[...])
    ones_bd = ones_ref[...]
    kk = k * kk_ref[...]
    kkn = kk * lax.rsqrt(jnp.maximum(_dot2_exact_rhs(kk * kk, ones_bd), 1e-24))
    k2 = k * (1.0 + (a - 1.0) * ka_ref[...])
    head_sum = lambda x: _dot(x.astype(BF16), ones_bd)
    bonus = head_sum(r * k2 * rk_ref[...]) * v

    ti = lax.broadcasted_iota(I32, (C, C), 0)
    tj = lax.broadcasted_iota(I32, (C, C), 1)
    incl = ti >= tj
    strict = ti > tj
    tri = jnp.where(incl, 1.0, 0.0).astype(BF16)
    p1 = lw.astype(BF16)
    r1 = lw - p1.astype(F32)
    p2 = r1.astype(BF16)
    p3 = (r1 - p2.astype(F32)).astype(BF16)
    cs = _dot(tri, p1) + (_dot(tri, p2) + _dot(tri, p3))
    w_incl = jnp.exp(cs)
    w_inv = jnp.exp(-cs)
    at = -kkn * jnp.exp(cs - lw)
    rt = r * w_incl
    bt = kkn * a * w_inv
    kt = k2 * w_inv
    wc = w_incl[C - 1:C, :]
    bh = bt * wc
    kh = kt * wc
    eye = jnp.where(ti == tj, 1.0, 0.0).astype(F32)

    per_head = lambda x: [x[:, hd * RWKV_HEAD:(hd + 1) * RWKV_HEAD] for hd in range(RWKV_HEADS)]
    rows = lambda xs: jnp.concatenate(xs, axis=0)
    side = lambda xs: jnp.concatenate(xs, axis=1)
    hi_lo = lambda x: tuple(per_head(p) for p in _split(x))
    cast = lambda x: per_head(x.astype(BF16))
    at_b, rt_b, bt_b, kt_b = cast(at), cast(rt), cast(bt), cast(kt)
    (v_hi, v_lo), (bh_hi, bh_lo), (kh_hi, kh_lo) = hi_lo(v), hi_lo(bh), hi_lo(kh)
    wc_h = per_head(wc)

    def dot_hl(a, b_hi, b_lo):
        ah, al = _split(a)
        return _dot(side([ah, al, ah]), rows([b_hi, b_hi, b_lo]))

    def head_group(heads):
        s_all = {h: s_scr[h] for h in heads}
        s_b = {h: s_all[h].astype(BF16) for h in heads}
        pr = {h: _dot_nt(rows([at_b[h], rt_b[h]]), rows([bt_b[h], kt_b[h]])) for h in heads}
        a_ab = {h: jnp.where(strict, pr[h][:C, :C], 0.0) for h in heads}
        a_ak = {h: jnp.where(strict, pr[h][:C, C:], 0.0).astype(BF16) for h in heads}
        a_rb = {h: jnp.where(incl, pr[h][C:, :C], 0.0).astype(BF16) for h in heads}
        a_rk = {h: jnp.where(incl, pr[h][C:, C:], 0.0).astype(BF16) for h in heads}
        pw = a_ab
        tinv = {h: eye + pw[h] for h in heads}
        n = 2
        while n < C:
            pw_b = {h: pw[h].astype(BF16) for h in heads}
            pw = {h: _dot(pw_b[h], pw_b[h]) for h in heads}
            tinv = {h: tinv[h] + _dot(tinv[h].astype(BF16), pw[h].astype(BF16)) for h in heads}
            n *= 2
        resid = {h: (eye - tinv[h]) + dot_hl(a_ab[h], *_split(tinv[h])) for h in heads}
        tinv = {h: tinv[h] + _dot(tinv[h].astype(BF16), resid[h].astype(BF16)) for h in heads}
        x = {h: _dot_nt(at_b[h], s_b[h]) + _dot(a_ak[h], v_hi[h]) for h in heads}
        u = {h: _split(_dot(tinv[h].astype(BF16), x[h].astype(BF16))) for h in heads}
        ys = [_dot_nt(rt_b[h], s_b[h]) + _dot(side([a_rb[h], a_rk[h]]), rows([u[h][0], v_hi[h]])) for h in heads]
        for h in heads:
            u_hi, u_lo = u[h]
            upd = _dot_tn(rows([u_hi, v_hi[h], u_lo, v_lo[h], u_hi, v_hi[h]]),
                          rows([bh_hi[h], kh_hi[h], bh_hi[h], kh_hi[h], bh_lo[h], kh_lo[h]]))
            s_scr[h] = s_all[h] * wc_h[h] + upd
        return ys

    ys = []
    for first in range(0, RWKV_HEADS, SCAN_HEAD_GROUP):
        ys += head_group(range(first, first + SCAN_HEAD_GROUP))
    y = jnp.concatenate(ys, axis=-1)
    inv_n = 1.0 / RWKV_HEAD
    mean = head_sum(y) * inv_n
    d = y - mean
    var = head_sum(d * d) * inv_n
    yn = d * lax.rsqrt(var + LN_X_EPS) * lng_ref[...] + lnb_ref[...]
    o_ref[...] = ((yn + bonus) * g).astype(o_ref.dtype)

    @pl.when(c == pl.num_programs(1) - 1)
    def _():
        sout_ref[...] = s_scr[...]


def _rwkv(z, shift0, s0, wts, C):
    B, L, _ = z.shape
    fixed = lambda b, c: (0, 0)
    full = lambda a: pl.BlockSpec(a.shape, fixed)
    names = ["mu", "w0", "w2", "a0", "a2", "g2", "k_k", "k_a", "r_k", "lnx_g", "lnx_b", "ones_bd"]
    return pl.pallas_call(
        functools.partial(_rwkv_kernel, C=C),
        grid=(B, L // C),
        in_specs=[pl.BlockSpec((None, C, RWKV_COLS), lambda b, c: (b, c, 0)),
                  pl.BlockSpec((None, 1, RWKV_COLS), lambda b, c: (b, 0, 0)),
                  pl.BlockSpec((None, RWKV_HEADS, RWKV_HEAD, RWKV_HEAD), lambda b, c: (b, 0, 0, 0))]
                 + [full(wts[n]) for n in names],
        out_specs=[pl.BlockSpec((None, C, RWKV_DIM), lambda b, c: (b, c, 0)),
                   pl.BlockSpec((None, RWKV_HEADS, RWKV_HEAD, RWKV_HEAD), lambda b, c: (b, 0, 0, 0))],
        out_shape=[jax.ShapeDtypeStruct((B, L, RWKV_DIM), BF16),
                   jax.ShapeDtypeStruct((B, RWKV_HEADS, RWKV_HEAD, RWKV_HEAD), F32)],
        scratch_shapes=[pltpu.VMEM((RWKV_HEADS, RWKV_HEAD, RWKV_HEAD), F32), pltpu.VMEM((1, RWKV_COLS), F32)],
        compiler_params=_cparams(("arbitrary", "arbitrary")),
        name="rwkv",
    )(z, shift0, s0, *[wts[n] for n in names])


def _merge_kernel(x_ref, oa_ref, ob_ref, gl_ref, woa_ref, wob_ref, fg_ref, wrh_ref, wrl_ref, br_ref,
                  x1_ref, h2_ref, ri_ref, rw_ref, cnt_ref, run_scr, *, tm):
    @pl.when(pl.program_id(0) == 0)
    def _():
        run_scr[...] = jnp.zeros_like(run_scr)

    pa = _dot(oa_ref[...], woa_ref[...])
    pb = _dot(ob_ref[...], wob_ref[...])
    gate = lambda n: _sigmoid(gl_ref[:, n * D_MODEL:(n + 1) * D_MODEL].astype(F32))
    x1 = x_ref[...] + (gate(0) * pa + gate(1) * pb)
    x1_ref[...] = x1
    h2 = x1 * _rms(x1) * fg_ref[...]
    _store_token_tiles(h2_ref, h2)
    hh, hl = _split(h2)
    lg = _dot(hh, wrh_ref[...]) + (_dot(hl, wrh_ref[...]) + _dot(hh, wrl_ref[...])) + br_ref[...]

    lane = lax.broadcasted_iota(I32, (tm, LANES), 1)
    lane_f = lane.astype(F32)
    first = lambda hit: jnp.min(jnp.where(hit, lane_f, float(LANES)), axis=-1, keepdims=True).astype(I32)
    is_group = jnp.logical_and(lane >= ROUTE_GROUP_LANE, lane < ROUTE_GROUP_LANE + N_GROUPS)
    gl = jnp.where(is_group, lg, MASK_VALUE)
    gmax = jnp.max(gl, axis=-1, keepdims=True)
    gidx = first(gl == gmax) - ROUTE_GROUP_LANE
    g_top = 1.0 / jnp.sum(jnp.where(is_group, jnp.exp(gl - gmax), 0.0), axis=-1, keepdims=True)
    in_group = jnp.logical_and(lane < N_EXPERTS, jnp.right_shift(lane, 3) == gidx)
    el = jnp.where(in_group, lg, MASK_VALUE)
    e1 = jnp.max(el, axis=-1, keepdims=True)
    i1 = first(el == e1)
    el2 = jnp.where(lane == i1, MASK_VALUE, el)
    e2 = jnp.max(el2, axis=-1, keepdims=True)
    i2 = first(el2 == e2)
    t = jnp.exp(e2 - e1)
    w1 = g_top / (1.0 + t)
    w2 = g_top * t / (1.0 + t)

    hit1 = lane == i1
    hit2 = lane == i2
    oh = jnp.where(jnp.logical_or(hit1, hit2), 1.0, 0.0)
    ti = lax.broadcasted_iota(I32, (tm, tm), 0)
    tj = lax.broadcasted_iota(I32, (tm, tm), 1)
    before = _dot(jnp.where(ti > tj, 1.0, 0.0).astype(BF16), oh.astype(BF16)) + run_scr[...]
    rank1 = jnp.sum(jnp.where(hit1, before, 0.0), axis=-1, keepdims=True)
    rank2 = jnp.sum(jnp.where(hit2, before, 0.0), axis=-1, keepdims=True)
    run = run_scr[...] + jnp.sum(oh, axis=0, keepdims=True)
    run_scr[...] = run
    cnt_ref[...] = run.astype(I32)

    sel = lambda vals: functools.reduce(lambda acc, kv: jnp.where(lane == kv[0], kv[1], acc), enumerate(vals), 0)
    fields = sel([i1.astype(F32), i2.astype(F32), rank1, rank2])
    ri_ref[...] = jnp.transpose(fields)[:ROUTE_FIELDS, :].astype(I32)
    rw_ref[...] = sel([w1, w2])


def _merge(x, oa, ob, gl, wts, tm):
    T = x.shape[0]
    row = lambda i: (i, 0)
    full = lambda a: pl.BlockSpec(a.shape, lambda i: (0,) * a.ndim)
    names = ["w_out_a", "w_out_b", "ffn_g", "wr_hi", "wr_lo", "b_r"]
    return pl.pallas_call(
        functools.partial(_merge_kernel, tm=tm),
        grid=(T // tm,),
        in_specs=[pl.BlockSpec((tm, D_MODEL), row), pl.BlockSpec((tm, MLA_HEADS * HEAD_PAD), row),
                  pl.BlockSpec((tm, RWKV_DIM), row), pl.BlockSpec((tm, GATE_COLS), row)]
                 + [full(wts[n]) for n in names],
        out_specs=[pl.BlockSpec((tm, D_MODEL), row), pl.BlockSpec((tm * ROW_TILE, LANES), row),
                   pl.BlockSpec((ROUTE_FIELDS, tm), lambda i: (0, i)), pl.BlockSpec((tm, LANES), row),
                   pl.BlockSpec((1, LANES), lambda i: (0, 0))],
        out_shape=[jax.ShapeDtypeStruct((T, D_MODEL), F32), jax.ShapeDtypeStruct((T * ROW_TILE, LANES), F32),
                   jax.ShapeDtypeStruct((ROUTE_FIELDS, T), I32), jax.ShapeDtypeStruct((T, LANES), F32),
                   jax.ShapeDtypeStruct((1, LANES), I32)],
        scratch_shapes=[pltpu.VMEM((1, LANES), F32)],
        compiler_params=_cparams(("arbitrary",)),
        name="merge",
    )(x, oa, ob, gl, *[wts[n] for n in names])


def _dispatch_kernel(pend_ref, cnt_ref, dest_ref, h_ref, xs_ref, zero_scr, buf, sem_fill, sem_in, sem_out, *,
                     tm, n_steps):
    i = pl.program_id(0)
    tile = lambda ref, slot, rows=1: ref.at[pl.ds(slot * ROW_TILE, rows * ROW_TILE), :]
    load = lambda step, slot: pltpu.make_async_copy(tile(h_ref, step * tm, tm), buf.at[slot], sem_in.at[slot])
    step_drain = lambda slot: pltpu.make_async_copy(buf.at[slot], tile(xs_ref, 0, tm), sem_out.at[slot])

    @pl.when(i == 0)
    def _():
        zero_scr[...] = jnp.zeros_like(zero_scr)
        fill = lambda e: pltpu.make_async_copy(zero_scr, tile(xs_ref, pend_ref[e] - FFN_ROWS, FFN_ROWS), sem_fill)
        for e in range(N_EXPERTS):
            pl.when(cnt_ref[e] > 0)(lambda e=e: fill(e).start())
        for e in range(N_EXPERTS):
            pl.when(cnt_ref[e] > 0)(lambda e=e: fill(e).wait())
        dead = lambda b: pltpu.make_async_copy(zero_scr, tile(xs_ref, b * FFN_ROWS, FFN_ROWS), sem_fill)
        first_dead = pend_ref[N_EXPERTS - 1] // FFN_ROWS
        n_blocks = xs_ref.shape[0] // (FFN_ROWS * ROW_TILE)
        lax.fori_loop(first_dead, n_blocks, lambda b, c: (dead(b).start(), c)[1], 0)
        lax.fori_loop(first_dead, n_blocks, lambda b, c: (dead(b).wait(), c)[1], 0)
        load(0, 0).start()

    slot = i % DISPATCH_SLOTS
    nxt = (i + 1) % DISPATCH_SLOTS

    @pl.when(i >= DISPATCH_SLOTS - 1)
    def _():
        for s in range(TOP_K):
            step_drain(nxt).wait()

    pl.when(i + 1 < n_steps)(lambda: load(i + 1, nxt).start())
    load(i, slot).wait()

    def issue(t, carry):
        for s in range(TOP_K):
            d = dest_ref[0, 0, s * tm + t]
            pltpu.make_async_copy(tile(buf.at[slot], t), tile(xs_ref, d), sem_out.at[slot]).start()
        return carry

    lax.fori_loop(0, tm, issue, 0, unroll=DMA_ISSUE_UNROLL)

    @pl.when(i == n_steps - 1)
    def _():
        for back in range(min(DISPATCH_SLOTS - 1, n_steps)):
            for s in range(TOP_K):
                step_drain((i - back) % DISPATCH_SLOTS).wait()


def _dest_blocks(dest, tm):
    T = dest.shape[1]
    return dest.reshape(TOP_K, T // tm, tm).transpose(1, 0, 2).reshape(T // tm, 1, TOP_K * tm)


def _dispatch(h2t, dest, pend, counts, n_slots, tm):
    T = h2t.shape[0] // ROW_TILE
    n_steps = T // tm
    dest3 = _dest_blocks(dest, tm)
    grid_spec = pltpu.PrefetchScalarGridSpec(
        num_scalar_prefetch=2,
        grid=(n_steps,),
        in_specs=[pl.BlockSpec((1, 1, TOP_K * tm), lambda i, pe, cn: (i, 0, 0), memory_space=pltpu.SMEM),
                  pl.BlockSpec(memory_space=pl.ANY)],
        out_specs=pl.BlockSpec(memory_space=pl.ANY),
        scratch_shapes=[pltpu.VMEM((FFN_ROWS * ROW_TILE, LANES), F32),
                        pltpu.VMEM((DISPATCH_SLOTS, tm * ROW_TILE, LANES), F32),
                        pltpu.SemaphoreType.DMA(()), pltpu.SemaphoreType.DMA((DISPATCH_SLOTS,)),
                        pltpu.SemaphoreType.DMA((DISPATCH_SLOTS,))],
    )
    return pl.pallas_call(
        functools.partial(_dispatch_kernel, tm=tm, n_steps=n_steps),
        grid_spec=grid_spec,
        out_shape=jax.ShapeDtypeStruct((n_slots * ROW_TILE, LANES), F32),
        compiler_params=_cparams(("arbitrary",)),
        name="dispatch",
    )(pend, counts, dest3, h2t)


def _ffn_kernel(be_ref, nv_ref, x_ref, wg_ref, wu_ref, wd_ref, y_ref):
    del be_ref
    live = pl.program_id(0) < nv_ref[0]

    @pl.when(live)
    def _():
        x = _load_token_tiles(x_ref, FFN_ROWS).astype(BF16)
        gate = _dot(x, wg_ref[...])
        up = _dot(x, wu_ref[...])
        _store_token_tiles(y_ref, _dot((gate * _sigmoid(gate) * up).astype(BF16), wd_ref[...]))

    @pl.when(jnp.logical_not(live))
    def _():
        y_ref[...] = jnp.zeros_like(y_ref)


def _ffn(xs, block_e, n_live, wts):
    nb = xs.shape[0] // (FFN_ROWS * ROW_TILE)
    live_block = lambda i, be, nv: (jnp.minimum(i, nv[0] - 1), 0)
    grid_spec = pltpu.PrefetchScalarGridSpec(
        num_scalar_prefetch=2,
        grid=(nb,),
        in_specs=[pl.BlockSpec((FFN_ROWS * ROW_TILE, LANES), live_block),
                  pl.BlockSpec((None, D_MODEL, EXPERT_FF), lambda i, be, nv: (be[i], 0, 0)),
                  pl.BlockSpec((None, D_MODEL, EXPERT_FF), lambda i, be, nv: (be[i], 0, 0)),
                  pl.BlockSpec((None, EXPERT_FF, D_MODEL), lambda i, be, nv: (be[i], 0, 0))],
        out_specs=pl.BlockSpec((FFN_ROWS * ROW_TILE, LANES), lambda i, be, nv: (i, 0)),
    )
    return pl.pallas_call(
        _ffn_kernel,
        grid_spec=grid_spec,
        out_shape=jax.ShapeDtypeStruct(xs.shape, F32),
        compiler_params=_cparams(("arbitrary",)),
        name="ffn",
    )(block_e, n_live, xs, wts["w_eg"], wts["w_eu"], wts["w_ed"])


def _combine_kernel(dest_ref, dest_next_ref, x1_ref, rw_ref, yb_ref, o_ref, y_scr, sems, *, tm):
    i = pl.program_id(0)
    n = pl.num_programs(0)
    slot = i % 2
    tile = lambda ref, t, rows=1: ref.at[pl.ds(t * ROW_TILE, rows * ROW_TILE), :]

    def gather(dref, buf):
        def issue(t, carry):
            for s in range(TOP_K):
                d = dref[0, 0, s * tm + t]
                pltpu.make_async_copy(tile(yb_ref, d), tile(y_scr.at[buf, s], t), sems.at[buf]).start()
            return carry

        lax.fori_loop(0, tm, issue, 0, unroll=DMA_ISSUE_UNROLL)

    pl.when(i == 0)(lambda: gather(dest_ref, 0))
    pl.when(i + 1 < n)(lambda: gather(dest_next_ref, 1 - slot))
    for s in range(TOP_K):
        pltpu.make_async_copy(tile(yb_ref, 0, tm), y_scr.at[slot, s], sems.at[slot]).wait()
    rw = rw_ref[...]
    y = [_load_token_tiles(y_scr.at[slot, s], tm) for s in range(TOP_K)]
    o_ref[...] = x1_ref[...] + (rw[:, 0:1] * y[0] + rw[:, 1:2] * y[1])


def _combine(x1, rw, yb, dest, tm):
    T = x1.shape[0]
    nt = T // tm
    dest3 = _dest_blocks(dest, tm)
    dest_spec = lambda step: pl.BlockSpec((1, 1, TOP_K * tm), lambda i: (step(i), 0, 0), memory_space=pltpu.SMEM)
    return pl.pallas_call(
        functools.partial(_combine_kernel, tm=tm),
        grid=(nt,),
        in_specs=[dest_spec(lambda i: i), dest_spec(lambda i: jnp.minimum(i + 1, nt - 1)),
                  pl.BlockSpec((tm, D_MODEL), lambda i: (i, 0)),
                  pl.BlockSpec((tm, LANES), lambda i: (i, 0)),
                  pl.BlockSpec(memory_space=pl.ANY)],
        out_specs=pl.BlockSpec((tm, D_MODEL), lambda i: (i, 0)),
        out_shape=jax.ShapeDtypeStruct((T, D_MODEL), F32),
        scratch_shapes=[pltpu.VMEM((2, TOP_K, tm * ROW_TILE, LANES), F32), pltpu.SemaphoreType.DMA((2,))],
        compiler_params=_cparams(("arbitrary",)),
        name="combine",
    )(dest3, dest3, x1, rw, yb)


def _moe(x1, h2, ri, rw, cnt, wts, tm):
    T = x1.shape[0]
    ids = ri[0:TOP_K]
    ranks = ri[TOP_K:2 * TOP_K]
    counts = cnt[0, :N_EXPERTS]
    padded = (counts + FFN_ROWS - 1) // FFN_ROWS * FFN_ROWS
    pend = jnp.cumsum(padded)
    pstart = pend - padded
    dest = (pstart[ids] + ranks).astype(I32)
    nb = -(-T * TOP_K // FFN_ROWS) + N_EXPERTS
    block_pos = jnp.arange(nb, dtype=I32) * FFN_ROWS
    block_e = jnp.minimum(jnp.sum(pend[None, :] <= block_pos[:, None], axis=1), N_EXPERTS - 1).astype(I32)
    n_live = (pend[-1:] // FFN_ROWS).astype(I32)
    xs = _dispatch(h2, dest, pend.astype(I32), counts.astype(I32), nb * FFN_ROWS, min(DISPATCH_TOKENS, T))
    yb = _ffn(xs, block_e, n_live, wts)
    return _combine(x1, rw, yb, dest, tm)


def _rope_tables(pos):
    inv = 1.0 / (ROPE_BASE ** (jnp.arange(HALF_ROPE, dtype=F32) * (2.0 / QK_ROPE)))
    ang = pos.astype(F32)[:, None] * inv[None, :]
    cos, sin = jnp.cos(ang), jnp.sin(ang)
    n = pos.shape[0]
    pad = jnp.zeros((n, HEAD_PAD - QK_DIM), F32)
    ct = jnp.concatenate([jnp.ones((n, QK_NOPE), F32), cos, cos, pad], axis=1)
    st = jnp.concatenate([jnp.zeros((n, QK_NOPE), F32), -sin, sin, pad], axis=1)
    kt = jnp.concatenate([cos, sin], axis=1)
    return ct, st, kt


def _score_bound(q_head_norm_g, k_head_norm_g):
    return (1.02 * QK_DIM * ATTN_SCALE * math.log2(math.e)) * (
        jnp.max(jnp.abs(q_head_norm_g)) * jnp.max(jnp.abs(k_head_norm_g)))


def _prep_weights(norm_mix_g, w_in, q_norm_g, w_uq, kv_norm_g, w_ukv, q_head_norm_g, k_head_norm_g,
                  rwkv_mu, rwkv_w0, rwkv_w2, rwkv_a0, rwkv_a2, rwkv_g2, rwkv_k_k, rwkv_k_a, rwkv_r_k,
                  rwkv_lnx_g, rwkv_lnx_b, w_out, norm_ffn_g, w_router_group, b_router_group,
                  w_router_expert, b_router_expert, w_expert_gate, w_expert_up, w_expert_down):
    row = lambda a: a.reshape(1, -1).astype(F32)
    head_pad = lambda a: jnp.pad(a, ((0, 0), (0, 0), (0, HEAD_PAD - a.shape[-1])))
    w = {}
    w["mix_g"] = row(norm_mix_g)
    w["w_in"] = jnp.concatenate([w_in[:, :MLA_COLS], jnp.zeros((D_MODEL, LANES - QK_ROPE), F32),
                                 w_in[:, MLA_COLS:]], axis=1).astype(BF16)
    w["q_g"] = row(q_norm_g)
    w["kv_g"] = row(kv_norm_g)
    w["wqa"] = head_pad(w_uq).reshape(Q_LORA, MLA_HEADS * HEAD_PAD).astype(BF16)
    partner = jnp.concatenate([jnp.zeros_like(w_uq[..., :QK_NOPE]), w_uq[..., QK_NOPE + HALF_ROPE:],
                               w_uq[..., QK_NOPE:QK_NOPE + HALF_ROPE]], axis=-1)
    w["wqb"] = head_pad(partner).reshape(Q_LORA, MLA_HEADS * HEAD_PAD).astype(BF16)
    pad_g = lambda g: jnp.pad(g, (0, HEAD_PAD - QK_DIM)).reshape(1, HEAD_PAD).astype(F32)
    w["qh_g"] = pad_g(q_head_norm_g) * (ATTN_SCALE * math.log2(math.e))
    w["kh_g"] = pad_g(k_head_norm_g)
    lane_one = lambda lane: (jnp.arange(HEAD_PAD) == lane).astype(F32).reshape(1, HEAD_PAD)
    w["score_bound"] = _score_bound(q_head_norm_g, k_head_norm_g)
    w["q_one"] = lane_one(QK_DIM)
    w["k_shift"] = -w["score_bound"] * lane_one(QK_DIM)
    w["v_one"] = jnp.tile(lane_one(V_DIM), (1, MLA_HEADS))
    w["wk"] = head_pad(w_ukv[..., :QK_NOPE]).reshape(KV_LORA, MLA_HEADS * HEAD_PAD).astype(BF16)
    w["wv"] = head_pad(w_ukv[..., QK_NOPE:]).reshape(KV_LORA, MLA_HEADS * HEAD_PAD).astype(BF16)
    w["place"] = jnp.pad(jnp.eye(QK_ROPE, dtype=F32), ((0, 0), (QK_NOPE, HEAD_PAD - QK_DIM))).astype(BF16)
    w["mu"] = row(rwkv_mu)
    w["w0"] = row(rwkv_w0)
    w["w2"] = jnp.concatenate([rwkv_w2, jnp.zeros((A_LORA, RWKV_DIM), F32)], axis=0).astype(BF16)
    w["a0"] = row(rwkv_a0)
    w["a2"] = jnp.concatenate([jnp.zeros((W_LORA, RWKV_DIM), F32), rwkv_a2], axis=0).astype(BF16)
    w["g2"] = rwkv_g2.astype(BF16)
    w["k_k"] = row(rwkv_k_k)
    w["k_a"] = row(rwkv_k_a)
    w["r_k"] = row(rwkv_r_k)
    w["lnx_g"] = row(rwkv_lnx_g)
    w["lnx_b"] = row(rwkv_lnx_b)
    head_of = jnp.arange(RWKV_DIM) // RWKV_HEAD
    w["ones_bd"] = (head_of[:, None] == head_of[None, :]).astype(BF16)
    w["w_out_a"] = jnp.pad(w_out[0].reshape(MLA_HEADS, V_DIM, D_MODEL), ((0, 0), (0, HEAD_PAD - V_DIM), (0, 0))
                           ).reshape(MLA_HEADS * HEAD_PAD, D_MODEL).astype(BF16)
    w["w_out_b"] = w_out[1].astype(BF16)
    w["ffn_g"] = row(norm_ffn_g)
    wr = jnp.concatenate([w_router_expert.reshape(D_MODEL, N_EXPERTS), w_router_group,
                          jnp.zeros((D_MODEL, LANES - N_EXPERTS - N_GROUPS), F32)], axis=1)
    w["wr_hi"] = wr.astype(BF16)
    w["wr_lo"] = (wr - w["wr_hi"].astype(F32)).astype(BF16)
    w["b_r"] = jnp.concatenate([b_router_expert.reshape(-1), b_router_group,
                                jnp.zeros((LANES - N_EXPERTS - N_GROUPS,), F32)]).reshape(1, LANES)
    w["w_eg"] = w_expert_gate.astype(BF16)
    w["w_eu"] = w_expert_up.astype(BF16)
    w["w_ed"] = w_expert_down.astype(BF16)
    return w


def _round_up(n, m):
    return -(-n // m) * m


def _layer(x, pos, lat_past, kpe_past, wkv0, shift0, wts):
    B, L, _ = x.shape
    T = B * L
    tm = min(256, T)
    ct, st, kt = _rope_tables(pos)
    if L % tm:
        reps = tm // L
        ct, st, kt = (jnp.tile(t, (reps, 1)) for t in (ct, st, kt))
    q, lat, kpe, z, gl = _inproj(x.reshape(T, D_MODEL), (ct, st, kt), wts, tm)
    lat3 = lat.reshape(B, L, KV_LORA)
    kpe3 = kpe.reshape(B, L, QK_ROPE)
    if lat_past is None:
        n_past, lk, lat_all, kpe_all = 0, L, lat3, kpe3
        tq, tk = min(ATTN_QUERY_TILE, L), min(ATTN_KEY_TILE, L)
        diag = True
    else:
        n_past = lat_past.shape[1]
        lk_valid = n_past + L
        lk = _round_up(lk_valid, LANES)
        padk = lambda a: jnp.pad(a, ((0, 0), (0, lk - lk_valid), (0, 0)))
        lat_all = padk(jnp.concatenate([lat_past, lat3], axis=1))
        kpe_all = padk(jnp.concatenate([kpe_past, kpe3], axis=1))
        tq, tk, diag = L, lk, False
    k, v = _kvprep(lat_all.reshape(B * lk, KV_LORA), kpe_all.reshape(B * lk, QK_ROPE), wts, min(512, B * lk))
    oa = _attention(q.reshape(B, L, -1), k.reshape(B, lk, -1), v.reshape(B, lk, -1),
                    wts["score_bound"] <= MAX_DIRECT_EXP2_SHIFT, tq=tq, tk=tk, n_past=n_past,
                    lk_valid=n_past + L, diag_tiles=diag)
    z3 = z.reshape(B, L, RWKV_COLS)
    ob, s_new = _rwkv(z3, shift0, wkv0, wts, min(SCAN_CHUNK, L))
    x1, h2, ri, rw, cnt = _merge(x.reshape(T, D_MODEL), oa.reshape(T, -1), ob.reshape(T, -1), gl, wts, tm)
    y = _moe(x1, h2, ri, rw, cnt, wts, tm)
    return y.reshape(B, L, D_MODEL), lat3, kpe3, s_new, z3[:, L - 1:, :]


def kernel(x_prompt, x_sample, cache_kv_latent, cache_k_rope, state_wkv, state_shift, norm_mix_g, w_in, q_norm_g, w_uq, kv_norm_g, w_ukv, q_head_norm_g, k_head_norm_g, rwkv_mu, rwkv_w0, rwkv_w2, rwkv_a0, rwkv_a2, rwkv_g2, rwkv_k_k, rwkv_k_a, rwkv_r_k, rwkv_lnx_g, rwkv_lnx_b, w_out, norm_ffn_g, w_router_group, b_router_group, w_router_expert, b_router_expert, w_expert_gate, w_expert_up, w_expert_down):
    weights = (norm_mix_g, w_in, q_norm_g, w_uq, kv_norm_g, w_ukv, q_head_norm_g, k_head_norm_g,
               rwkv_mu, rwkv_w0, rwkv_w2, rwkv_a0, rwkv_a2, rwkv_g2, rwkv_k_k, rwkv_k_a, rwkv_r_k,
               rwkv_lnx_g, rwkv_lnx_b, w_out, norm_ffn_g, w_router_group, b_router_group,
               w_router_expert, b_router_expert, w_expert_gate, w_expert_up, w_expert_down)
    depth = w_in.shape[0]
    bp, lp, _ = x_prompt.shape
    n_past = cache_kv_latent.shape[2]
    pos_p = jnp.arange(lp, dtype=I32)
    pos_s = n_past + jnp.arange(x_sample.shape[1], dtype=I32)
    wkv_zero = jnp.zeros((bp, RWKV_HEADS, RWKV_HEAD, RWKV_HEAD), F32)
    shift_zero = jnp.zeros((bp, 1, RWKV_COLS), F32)
    xp, xs = x_prompt, x_sample
    outs_p, outs_s = [], []
    for l in range(depth):
        wts = _prep_weights(*(wt[l] for wt in weights))
        xp, *rest_p = _layer(xp, pos_p, None, None, wkv_zero, shift_zero, wts)
        xs, *rest_s = _layer(xs, pos_s, cache_kv_latent[l], cache_k_rope[l], state_wkv[l], state_shift[l], wts)
        outs_p.append(rest_p)
        outs_s.append(rest_s)
    stack = lambda outs, i: jnp.stack([o[i] for o in outs], 0)
    return (xp, xs, stack(outs_p, 0), stack(outs_p, 1), stack(outs_p, 2), stack(outs_p, 3),
            stack(outs_s, 0), stack(outs_s, 1), stack(outs_s, 2), stack(outs_s, 3))
```

```python
import functools
import math

import jax
import jax.numpy as jnp
from jax import lax
from jax.experimental import pallas as pl
from jax.experimental.pallas import tpu as pltpu

F32 = jnp.float32
BF16 = jnp.bfloat16
I32 = jnp.int32

D_MODEL = 1024
CHUNK = 64
RMS_EPS = 1e-6
MASK_VALUE = -1e30
MLA_HEADS = 8
QK_NOPE = 64
QK_ROPE = 32
QK_DIM = QK_NOPE + QK_ROPE
V_DIM = 64
Q_LORA = 256
KV_LORA = 128
ROPE_BASE = 10000.0
ATTN_SCALE = QK_DIM ** -0.5
RWKV_HEADS = 8
RWKV_HEAD = 64
RWKV_DIM = RWKV_HEADS * RWKV_HEAD
W_LORA = 64
A_LORA = 64
G_LORA = 128
LN_X_EPS = 64e-5
MLA_COLS = Q_LORA + KV_LORA + QK_ROPE
RWKV_COLS = 3 * RWKV_DIM + W_LORA + A_LORA + G_LORA
GATE_COLS = 2 * D_MODEL
N_GROUPS = 4
EXPERTS_PER_GROUP = 8
N_EXPERTS = N_GROUPS * EXPERTS_PER_GROUP
EXPERT_FF = 256
TOP_K = 2

LANES = 128
ROW_TILE = D_MODEL // LANES
DMA_ISSUE_UNROLL = 4
DISPATCH_TOKENS = 512
DISPATCH_SLOTS = 3
HEAD_PAD = LANES
HALF_ROPE = QK_ROPE // 2
SEG_Q = 0
SEG_KV = SEG_Q + Q_LORA
SEG_PE = SEG_KV + KV_LORA
SEG_Z = SEG_PE + LANES
SEG_G = SEG_Z + RWKV_COLS
IN_COLS_PAD = SEG_G + GATE_COLS
ROUTE_GROUP_LANE = N_EXPERTS
ROUTE_FIELDS = 8
SCAN_CHUNK = 128
SCAN_HEAD_GROUP = 8
FFN_ROWS = 256
ATTN_QUERY_TILE = 512
ATTN_KEY_TILE = 512
MAX_DIRECT_EXP2_SHIFT = 40.0
VMEM_LIMIT = 48 * 1024 * 1024


def _cparams(sem, vmem=VMEM_LIMIT):
    return pltpu.CompilerParams(dimension_semantics=sem, vmem_limit_bytes=vmem)


def _dot(a, b):
    return jnp.dot(a, b, preferred_element_type=F32)


def _dot_nt(a, b):
    return lax.dot_general(a, b, (((1,), (1,)), ((), ())), preferred_element_type=F32)


def _dot_tn(a, b):
    return lax.dot_general(a, b, (((0,), (0,)), ((), ())), preferred_element_type=F32)


def _split(x):
    hi = x.astype(BF16)
    lo = (x - hi.astype(F32)).astype(BF16)
    return hi, lo


def _dot2_exact_rhs(a, b_bf16):
    ah, al = _split(a)
    return _dot(ah, b_bf16) + _dot(al, b_bf16)


def _sigmoid(x):
    return 1.0 / (1.0 + jnp.exp(-x))


def _rms(x, n=None):
    n = x.shape[-1] if n is None else n
    return lax.rsqrt(jnp.sum(x * x, axis=-1, keepdims=True) * (1.0 / n) + RMS_EPS)


def _store_token_tiles(ref, x):
    n = x.shape[0]
    for c in range(ROW_TILE):
        ref[pl.ds(c, n, stride=ROW_TILE), :] = x[:, c * LANES:(c + 1) * LANES]


def _load_token_tiles(ref, n):
    return jnp.concatenate([ref[pl.ds(c, n, stride=ROW_TILE), :] for c in range(ROW_TILE)], axis=1)


def _inproj_kernel(x_ref, g_ref, w_ref, qg_ref, wqa_ref, wqb_ref, ct_ref, st_ref, kt_ref, kvg_ref, qhg_ref, qone_ref,
                   q_ref, lat_ref, kpe_ref, z_ref, gl_ref):
    x = x_ref[...]
    h = (x * _rms(x) * g_ref[...]).astype(BF16)
    cq = _dot(h, w_ref[:, SEG_Q:SEG_KV])
    ckv = _dot(h, w_ref[:, SEG_KV:SEG_PE])
    pe = _dot(h, w_ref[:, SEG_PE:SEG_Z])
    z_ref[...] = _dot(h, w_ref[:, SEG_Z:SEG_G])
    gl_ref[...] = _dot(h, w_ref[:, SEG_G:IN_COLS_PAD]).astype(gl_ref.dtype)
    lat_ref[...] = ckv * _rms(ckv) * kvg_ref[...]
    x1 = pe[:, :HALF_ROPE]
    x2 = pe[:, HALF_ROPE:QK_ROPE]
    c = kt_ref[:, :HALF_ROPE]
    s = kt_ref[:, HALF_ROPE:QK_ROPE]
    kpe_ref[...] = jnp.concatenate([x1 * c - x2 * s, x1 * s + x2 * c], axis=-1)
    cqn = (cq * _rms(cq) * qg_ref[...]).astype(BF16)
    ct = ct_ref[...]
    st = st_ref[...]
    qhg = qhg_ref[...]
    qone = qone_ref[...]
    for hd in range(MLA_HEADS):
        sl = slice(hd * HEAD_PAD, (hd + 1) * HEAD_PAD)
        xh = _dot(cqn, wqa_ref[:, sl]) * ct + _dot(cqn, wqb_ref[:, sl]) * st
        q_ref[:, sl] = (xh * _rms(xh, QK_DIM) * qhg + qone).astype(BF16)


def _inproj(x, tabs, wts, tm):
    T = x.shape[0]
    ct, st, kt = tabs
    nt = ct.shape[0] // tm
    row = lambda i: (i, 0)
    tab = lambda i: (i % nt, 0)
    fixed = lambda i: (0, 0)
    full = lambda a: pl.BlockSpec(a.shape, fixed)
    return pl.pallas_call(
        _inproj_kernel,
        grid=(T // tm,),
        in_specs=[pl.BlockSpec((tm, D_MODEL), row), full(wts["mix_g"]), full(wts["w_in"]), full(wts["q_g"]),
                  full(wts["wqa"]), full(wts["wqb"]),
                  pl.BlockSpec((tm, HEAD_PAD), tab), pl.BlockSpec((tm, HEAD_PAD), tab),
                  pl.BlockSpec((tm, QK_ROPE), tab), full(wts["kv_g"]), full(wts["qh_g"]), full(wts["q_one"])],
        out_specs=[pl.BlockSpec((tm, MLA_HEADS * HEAD_PAD), row), pl.BlockSpec((tm, KV_LORA), row),
                   pl.BlockSpec((tm, QK_ROPE), row), pl.BlockSpec((tm, RWKV_COLS), row),
                   pl.BlockSpec((tm, GATE_COLS), row)],
        out_shape=[jax.ShapeDtypeStruct((T, MLA_HEADS * HEAD_PAD), BF16), jax.ShapeDtypeStruct((T, KV_LORA), F32),
                   jax.ShapeDtypeStruct((T, QK_ROPE), F32), jax.ShapeDtypeStruct((T, RWKV_COLS), F32),
                   jax.ShapeDtypeStruct((T, GATE_COLS), BF16)],
        compiler_params=_cparams(("arbitrary",)),
        name="inproj",
    )(x, wts["mix_g"], wts["w_in"], wts["q_g"], wts["wqa"], wts["wqb"], ct, st, kt, wts["kv_g"], wts["qh_g"],
      wts["q_one"])


def _kvprep_kernel(lat_ref, kpe_ref, wk_ref, wv_ref, place_ref, kg_ref, kshift_ref, vone_ref, k_ref, v_ref):
    lat = lat_ref[...].astype(BF16)
    rot = _dot2_exact_rhs(kpe_ref[...], place_ref[...])
    kg = kg_ref[...]
    kshift = kshift_ref[...]
    for hd in range(MLA_HEADS):
        sl = slice(hd * HEAD_PAD, (hd + 1) * HEAD_PAD)
        kh = _dot(lat, wk_ref[:, sl]) + rot
        k_ref[:, sl] = (kh * _rms(kh, QK_DIM) * kg + kshift).astype(BF16)
    v_ref[...] = (_dot(lat, wv_ref[...]) + vone_ref[...]).astype(BF16)


def _kvprep(lat, kpe, wts, tm):
    T = lat.shape[0]
    row = lambda i: (i, 0)
    fixed = lambda i: (0, 0)
    full = lambda a: pl.BlockSpec(a.shape, fixed)
    return pl.pallas_call(
        _kvprep_kernel,
        grid=(T // tm,),
        in_specs=[pl.BlockSpec((tm, KV_LORA), row), pl.BlockSpec((tm, QK_ROPE), row), full(wts["wk"]),
                  full(wts["wv"]), full(wts["place"]), full(wts["kh_g"]), full(wts["k_shift"]),
                  full(wts["v_one"])],
        out_specs=[pl.BlockSpec((tm, MLA_HEADS * HEAD_PAD), row), pl.BlockSpec((tm, MLA_HEADS * HEAD_PAD), row)],
        out_shape=[jax.ShapeDtypeStruct((T, MLA_HEADS * HEAD_PAD), BF16),
                   jax.ShapeDtypeStruct((T, MLA_HEADS * HEAD_PAD), BF16)],
        compiler_params=_cparams(("arbitrary",)),
        name="kvprep",
    )(lat, kpe, wts["wk"], wts["wv"], wts["place"], wts["kh_g"], wts["k_shift"], wts["v_one"])


def _attn_kernel(q_ref, k_ref, v_ref, o_ref, *, tq, tk, n_past, lk_valid, diag_tiles, bounded):
    i = pl.program_id(2)
    n_full = i * (tq // tk) if diag_tiles else 0
    head_lanes = [slice(hh * HEAD_PAD, (hh + 1) * HEAD_PAD) for hh in range(2)]
    qs = [q_ref[:, hs] for hs in head_lanes]

    def scores(hh, j, masked):
        ks = pl.multiple_of(j * tk, tk)
        s = _dot_nt(qs[hh], k_ref[pl.ds(ks, tk), head_lanes[hh]])
        if masked:
            qpos = n_past + i * tq + lax.broadcasted_iota(I32, (tq, tk), 0)
            kpos = j * tk + lax.broadcasted_iota(I32, (tq, tk), 1)
            vis = jnp.logical_and(jnp.right_shift(kpos, 6) <= jnp.right_shift(qpos, 6), kpos < lk_valid)
            s = jnp.where(vis, s, MASK_VALUE)
        return s, v_ref[pl.ds(ks, tk), head_lanes[hh]]

    if bounded:
        def head_step(hh, j, acc, masked):
            s, v = scores(hh, j, masked)
            return acc + _dot(jnp.exp2(s).astype(BF16), v)

        init = jnp.zeros((tq, HEAD_PAD), F32)
    else:
        def head_step(hh, j, carry, masked):
            m, acc = carry
            s, v = scores(hh, j, masked)
            m_new = jnp.maximum(m, jnp.max(s, axis=-1, keepdims=True))
            return m_new, jnp.exp2(m - m_new) * acc + _dot(jnp.exp2(s - m_new).astype(BF16), v)

        init = (jnp.full((tq, 1), MASK_VALUE, F32), jnp.zeros((tq, HEAD_PAD), F32))

    step = lambda j, carry, masked: tuple(head_step(hh, j, carry[hh], masked) for hh in range(2))
    carry = (init, init)
    if diag_tiles:
        carry = lax.fori_loop(0, n_full, functools.partial(step, masked=False), carry)
    for d in range(tq // tk if diag_tiles else 1):
        carry = step(n_full + d, carry, True)
    for hh in range(2):
        acc = carry[hh] if bounded else carry[hh][1]
        o_ref[:, head_lanes[hh]] = (acc / acc[:, V_DIM:V_DIM + 1]).astype(o_ref.dtype)


def _attention(q, k, v, bound_ok, *, tq, tk, n_past, lk_valid, diag_tiles):
    B, L, _ = q.shape
    Lk = k.shape[1]

    def run(bounded):
        kern = functools.partial(_attn_kernel, tq=tq, tk=tk, n_past=n_past, lk_valid=lk_valid,
                                 diag_tiles=diag_tiles, bounded=bounded)
        return pl.pallas_call(
            kern,
            grid=(B, MLA_HEADS // 2, L // tq),
            in_specs=[pl.BlockSpec((None, tq, 2 * HEAD_PAD), lambda b, p, i: (b, i, p)),
                      pl.BlockSpec((None, Lk, 2 * HEAD_PAD), lambda b, p, i: (b, 0, p)),
                      pl.BlockSpec((None, Lk, 2 * HEAD_PAD), lambda b, p, i: (b, 0, p))],
            out_specs=pl.BlockSpec((None, tq, 2 * HEAD_PAD), lambda b, p, i: (b, i, p)),
            out_shape=jax.ShapeDtypeStruct((B, L, MLA_HEADS * HEAD_PAD), BF16),
            compiler_params=_cparams(("arbitrary", "arbitrary", "arbitrary")),
            name="attn_bounded" if bounded else "attn_running_max",
        )(q, k, v)

    return lax.cond(bound_ok, lambda: run(True), lambda: run(False))


def _rwkv_kernel(z_ref, shift_ref, s0_ref, mu_ref, w0_ref, w2_ref, a0_ref, a2_ref, g2_ref, kk_ref, ka_ref, rk_ref,
                 lng_ref, lnb_ref, ones_ref, o_ref, sout_ref, s_scr, prev_scr, *, C):
    c = pl.program_id(1)

    @pl.when(c == 0)
    def _():
        s_scr[...] = s0_ref[...]
        prev_scr[...] = shift_ref[...]

    zf = z_ref[...]
    row = lax.broadcasted_iota(I32, (C, 1), 0)
    pv = jnp.where(row == 0, prev_scr[...], pltpu.roll(zf, 1, axis=0))
    prev_scr[...] = zf[C - 1:C, :]
    zm = zf + (pv - zf) * mu_ref[...]
    H = RWKV_DIM
    r = zm[:, 0:H]
    k = zm[:, H:2 * H]
    v = zm[:, 2 * H:3 * H]
    xwa = zm[:, 3 * H:3 * H + W_LORA + A_LORA]
    xg = zm[:, 3 * H + W_LORA + A_LORA:]
    lw = (-math.exp(-0.5)) * _sigmoid(w0_ref[...] + _dot(jnp.tanh(xwa).astype(BF16), w2_ref[...]))
    a = _sigmoid(a0_ref[...] + _dot(xwa.astype(BF16), a2_ref[...]))
    g = _dot(_sigmoid(xg).astype(BF16), g2_ref[...])
    ones_bd = ones_ref[...]
    kk = k * kk_ref[...]
    kkn = kk * lax.rsqrt(jnp.maximum(_dot2_exact_rhs(kk * kk, ones_bd), 1e-24))
    k2 = k * (1.0 + (a - 1.0) * ka_ref[...])
    head_sum = lambda x: _dot(x.astype(BF16), ones_bd)
    bonus = head_sum(r * k2 * rk_ref[...]) * v

    ti = lax.broadcasted_iota(I32, (C, C), 0)
    tj = lax.broadcasted_iota(I32, (C, C), 1)
    incl = ti >= tj
    strict = ti > tj
    tri = jnp.where(incl, 1.0, 0.0).astype(BF16)
    p1 = lw.astype(BF16)
    r1 = lw - p1.astype(F32)
    p2 = r1.astype(BF16)
    p3 = (r1 - p2.astype(F32)).astype(BF16)
    cs = _dot(tri, p1) + (_dot(tri, p2) + _dot(tri, p3))
    w_incl = jnp.exp(cs)
    w_inv = jnp.exp(-cs)
    at = -kkn * jnp.exp(cs - lw)
    rt = r * w_incl
    bt = kkn * a * w_inv
    kt = k2 * w_inv
    wc = w_incl[C - 1:C, :]
    bh = bt * wc
    kh = kt * wc
    eye = jnp.where(ti == tj, 1.0, 0.0).astype(F32)

    per_head = lambda x: [x[:, hd * RWKV_HEAD:(hd + 1) * RWKV_HEAD] for hd in range(RWKV_HEADS)]
    rows = lambda xs: jnp.concatenate(xs, axis=0)
    side = lambda xs: jnp.concatenate(xs, axis=1)
    hi_lo = lambda x: tuple(per_head(p) for p in _split(x))
    cast = lambda x: per_head(x.astype(BF16))
    at_b, rt_b, bt_b, kt_b = cast(at), cast(rt), cast(bt), cast(kt)
    (v_hi, v_lo), (bh_hi, bh_lo), (kh_hi, kh_lo) = hi_lo(v), hi_lo(bh), hi_lo(kh)
    wc_h = per_head(wc)

    def dot_hl(a, b_hi, b_lo):
        ah, al = _split(a)
        return _dot(side([ah, al, ah]), rows([b_hi, b_hi, b_lo]))

    def head_group(heads):
        s_all = {h: s_scr[h] for h in heads}
        s_b = {h: s_all[h].astype(BF16) for h in heads}
        pr = {h: _dot_nt(rows([at_b[h], rt_b[h]]), rows([bt_b[h], kt_b[h]])) for h in heads}
        a_ab = {h: jnp.where(strict, pr[h][:C, :C], 0.0) for h in heads}
        a_ak = {h: jnp.where(strict, pr[h][:C, C:], 0.0).astype(BF16) for h in heads}
        a_rb = {h: jnp.where(incl, pr[h][C:, :C], 0.0).astype(BF16) for h in heads}
        a_rk = {h: jnp.where(incl, pr[h][C:, C:], 0.0).astype(BF16) for h in heads}
        pw = a_ab
        tinv = {h: eye + pw[h] for h in heads}
        n = 2
        while n < C:
            pw_b = {h: pw[h].astype(BF16) for h in heads}
            pw = {h: _dot(pw_b[h], pw_b[h]) for h in heads}
            tinv = {h: tinv[h] + _dot(tinv[h].astype(BF16), pw[h].astype(BF16)) for h in heads}
            n *= 2
        resid = {h: (eye - tinv[h]) + dot_hl(a_ab[h], *_split(tinv[h])) for h in heads}
        tinv = {h: tinv[h] + _dot(tinv[h].astype(BF16), resid[h].astype(BF16)) for h in heads}
        x = {h: _dot_nt(at_b[h], s_b[h]) + _dot(a_ak[h], v_hi[h]) for h in heads}
        u = {h: _split(_dot(tinv[h].astype(BF16), x[h].astype(BF16))) for h in heads}
        ys = [_dot_nt(rt_b[h], s_b[h]) + _dot(side([a_rb[h], a_rk[h]]), rows([u[h][0], v_hi[h]])) for h in heads]
        for h in heads:
            u_hi, u_lo = u[h]
            upd = _dot_tn(rows([u_hi, v_hi[h], u_lo, v_lo[h], u_hi, v_hi[h]]),
                          rows([bh_hi[h], kh_hi[h], bh_hi[h], kh_hi[h], bh_lo[h], kh_lo[h]]))
            s_scr[h] = s_all[h] * wc_h[h] + upd
        return ys

    ys = []
    for first in range(0, RWKV_HEADS, SCAN_HEAD_GROUP):
        ys += head_group(range(first, first + SCAN_HEAD_GROUP))
    y = jnp.concatenate(ys, axis=-1)
    inv_n = 1.0 / RWKV_HEAD
    mean = head_sum(y) * inv_n
    d = y - mean
    var = head_sum(d * d) * inv_n
    yn = d * lax.rsqrt(var + LN_X_EPS) * lng_ref[...] + lnb_ref[...]
    o_ref[...] = ((yn + bonus) * g).astype(o_ref.dtype)

    @pl.when(c == pl.num_programs(1) - 1)
    def _():
        sout_ref[...] = s_scr[...]


def _rwkv(z, shift0, s0, wts, C):
    B, L, _ = z.shape
    fixed = lambda b, c: (0, 0)
    full = lambda a: pl.BlockSpec(a.shape, fixed)
    names = ["mu", "w0", "w2", "a0", "a2", "g2", "k_k", "k_a", "r_k", "lnx_g", "lnx_b", "ones_bd"]
    return pl.pallas_call(
        functools.partial(_rwkv_kernel, C=C),
        grid=(B, L // C),
        in_specs=[pl.BlockSpec((None, C, RWKV_COLS), lambda b, c: (b, c, 0)),
                  pl.BlockSpec((None, 1, RWKV_COLS), lambda b, c: (b, 0, 0)),
                  pl.BlockSpec((None, RWKV_HEADS, RWKV_HEAD, RWKV_HEAD), lambda b, c: (b, 0, 0, 0))]
                 + [full(wts[n]) for n in names],
        out_specs=[pl.BlockSpec((None, C, RWKV_DIM), lambda b, c: (b, c, 0)),
                   pl.BlockSpec((None, RWKV_HEADS, RWKV_HEAD, RWKV_HEAD), lambda b, c: (b, 0, 0, 0))],
        out_shape=[jax.ShapeDtypeStruct((B, L, RWKV_DIM), BF16),
                   jax.ShapeDtypeStruct((B, RWKV_HEADS, RWKV_HEAD, RWKV_HEAD), F32)],
        scratch_shapes=[pltpu.VMEM((RWKV_HEADS, RWKV_HEAD, RWKV_HEAD), F32), pltpu.VMEM((1, RWKV_COLS), F32)],
        compiler_params=_cparams(("arbitrary", "arbitrary")),
        name="rwkv",
    )(z, shift0, s0, *[wts[n] for n in names])


def _merge_kernel(x_ref, oa_ref, ob_ref, gl_ref, woa_ref, wob_ref, fg_ref, wrh_ref, wrl_ref, br_ref,
                  x1_ref, h2_ref, ri_ref, rw_ref, cnt_ref, run_scr, *, tm):
    @pl.when(pl.program_id(0) == 0)
    def _():
        run_scr[...] = jnp.zeros_like(run_scr)

    pa = _dot(oa_ref[...], woa_ref[...])
    pb = _dot(ob_ref[...], wob_ref[...])
    gate = lambda n: _sigmoid(gl_ref[:, n * D_MODEL:(n + 1) * D_MODEL].astype(F32))
    x1 = x_ref[...] + (gate(0) * pa + gate(1) * pb)
    x1_ref[...] = x1
    h2 = x1 * _rms(x1) * fg_ref[...]
    _store_token_tiles(h2_ref, h2)
    hh, hl = _split(h2)
    lg = _dot(hh, wrh_ref[...]) + (_dot(hl, wrh_ref[...]) + _dot(hh, wrl_ref[...])) + br_ref[...]

    lane = lax.broadcasted_iota(I32, (tm, LANES), 1)
    lane_f = lane.astype(F32)
    first = lambda hit: jnp.min(jnp.where(hit, lane_f, float(LANES)), axis=-1, keepdims=True).astype(I32)
    is_group = jnp.logical_and(lane >= ROUTE_GROUP_LANE, lane < ROUTE_GROUP_LANE + N_GROUPS)
    gl = jnp.where(is_group, lg, MASK_VALUE)
    gmax = jnp.max(gl, axis=-1, keepdims=True)
    gidx = first(gl == gmax) - ROUTE_GROUP_LANE
    g_top = 1.0 / jnp.sum(jnp.where(is_group, jnp.exp(gl - gmax), 0.0), axis=-1, keepdims=True)
    in_group = jnp.logical_and(lane < N_EXPERTS, jnp.right_shift(lane, 3) == gidx)
    el = jnp.where(in_group, lg, MASK_VALUE)
    e1 = jnp.max(el, axis=-1, keepdims=True)
    i1 = first(el == e1)
    el2 = jnp.where(lane == i1, MASK_VALUE, el)
    e2 = jnp.max(el2, axis=-1, keepdims=True)
    i2 = first(el2 == e2)
    t = jnp.exp(e2 - e1)
    w1 = g_top / (1.0 + t)
    w2 = g_top * t / (1.0 + t)

    hit1 = lane == i1
    hit2 = lane == i2
    oh = jnp.where(jnp.logical_or(hit1, hit2), 1.0, 0.0)
    ti = lax.broadcasted_iota(I32, (tm, tm), 0)
    tj = lax.broadcasted_iota(I32, (tm, tm), 1)
    before = _dot(jnp.where(ti > tj, 1.0, 0.0).astype(BF16), oh.astype(BF16)) + run_scr[...]
    rank1 = jnp.sum(jnp.where(hit1, before, 0.0), axis=-1, keepdims=True)
    rank2 = jnp.sum(jnp.where(hit2, before, 0.0), axis=-1, keepdims=True)
    run = run_scr[...] + jnp.sum(oh, axis=0, keepdims=True)
    run_scr[...] = run
    cnt_ref[...] = run.astype(I32)

    sel = lambda vals: functools.reduce(lambda acc, kv: jnp.where(lane == kv[0], kv[1], acc), enumerate(vals), 0)
    fields = sel([i1.astype(F32), i2.astype(F32), rank1, rank2])
    ri_ref[...] = jnp.transpose(fields)[:ROUTE_FIELDS, :].astype(I32)
    rw_ref[...] = sel([w1, w2])


def _merge(x, oa, ob, gl, wts, tm):
    T = x.shape[0]
    row = lambda i: (i, 0)
    full = lambda a: pl.BlockSpec(a.shape, lambda i: (0,) * a.ndim)
    names = ["w_out_a", "w_out_b", "ffn_g", "wr_hi", "wr_lo", "b_r"]
    return pl.pallas_call(
        functools.partial(_merge_kernel, tm=tm),
        grid=(T // tm,),
        in_specs=[pl.BlockSpec((tm, D_MODEL), row), pl.BlockSpec((tm, MLA_HEADS * HEAD_PAD), row),
                  pl.BlockSpec((tm, RWKV_DIM), row), pl.BlockSpec((tm, GATE_COLS), row)]
                 + [full(wts[n]) for n in names],
        out_specs=[pl.BlockSpec((tm, D_MODEL), row), pl.BlockSpec((tm * ROW_TILE, LANES), row),
                   pl.BlockSpec((ROUTE_FIELDS, tm), lambda i: (0, i)), pl.BlockSpec((tm, LANES), row),
                   pl.BlockSpec((1, LANES), lambda i: (0, 0))],
        out_shape=[jax.ShapeDtypeStruct((T, D_MODEL), F32), jax.ShapeDtypeStruct((T * ROW_TILE, LANES), F32),
                   jax.ShapeDtypeStruct((ROUTE_FIELDS, T), I32), jax.ShapeDtypeStruct((T, LANES), F32),
                   jax.ShapeDtypeStruct((1, LANES), I32)],
        scratch_shapes=[pltpu.VMEM((1, LANES), F32)],
        compiler_params=_cparams(("arbitrary",)),
        name="merge",
    )(x, oa, ob, gl, *[wts[n] for n in names])


def _dispatch_kernel(pend_ref, cnt_ref, dest_ref, h_ref, xs_ref, zero_scr, buf, sem_fill, sem_in, sem_out, *,
                     tm, n_steps):
    i = pl.program_id(0)
    tile = lambda ref, slot, rows=1: ref.at[pl.ds(slot * ROW_TILE, rows * ROW_TILE), :]
    load = lambda step, slot: pltpu.make_async_copy(tile(h_ref, step * tm, tm), buf.at[slot], sem_in.at[slot])
    step_drain = lambda slot: pltpu.make_async_copy(buf.at[slot], tile(xs_ref, 0, tm), sem_out.at[slot])

    @pl.when(i == 0)
    def _():
        zero_scr[...] = jnp.zeros_like(zero_scr)
        fill = lambda e: pltpu.make_async_copy(zero_scr, tile(xs_ref, pend_ref[e] - FFN_ROWS, FFN_ROWS), sem_fill)
        for e in range(N_EXPERTS):
            pl.when(cnt_ref[e] > 0)(lambda e=e: fill(e).start())
        for e in range(N_EXPERTS):
            pl.when(cnt_ref[e] > 0)(lambda e=e: fill(e).wait())
        dead = lambda b: pltpu.make_async_copy(zero_scr, tile(xs_ref, b * FFN_ROWS, FFN_ROWS), sem_fill)
        first_dead = pend_ref[N_EXPERTS - 1] // FFN_ROWS
        n_blocks = xs_ref.shape[0] // (FFN_ROWS * ROW_TILE)
        lax.fori_loop(first_dead, n_blocks, lambda b, c: (dead(b).start(), c)[1], 0)
        lax.fori_loop(first_dead, n_blocks, lambda b, c: (dead(b).wait(), c)[1], 0)
        load(0, 0).start()

    slot = i % DISPATCH_SLOTS
    nxt = (i + 1) % DISPATCH_SLOTS

    @pl.when(i >= DISPATCH_SLOTS - 1)
    def _():
        for s in range(TOP_K):
            step_drain(nxt).wait()

    pl.when(i + 1 < n_steps)(lambda: load(i + 1, nxt).start())
    load(i, slot).wait()

    def issue(t, carry):
        for s in range(TOP_K):
            d = dest_ref[0, 0, s * tm + t]
            pltpu.make_async_copy(tile(buf.at[slot], t), tile(xs_ref, d), sem_out.at[slot]).start()
        return carry

    lax.fori_loop(0, tm, issue, 0, unroll=DMA_ISSUE_UNROLL)

    @pl.when(i == n_steps - 1)
    def _():
        for back in range(min(DISPATCH_SLOTS - 1, n_steps)):
            for s in range(TOP_K):
                step_drain((i - back) % DISPATCH_SLOTS).wait()


def _dest_blocks(dest, tm):
    T = dest.shape[1]
    return dest.reshape(TOP_K, T // tm, tm).transpose(1, 0, 2).reshape(T // tm, 1, TOP_K * tm)


def _dispatch(h2t, dest, pend, counts, n_slots, tm):
    T = h2t.shape[0] // ROW_TILE
    n_steps = T // tm
    dest3 = _dest_blocks(dest, tm)
    grid_spec = pltpu.PrefetchScalarGridSpec(
        num_scalar_prefetch=2,
        grid=(n_steps,),
        in_specs=[pl.BlockSpec((1, 1, TOP_K * tm), lambda i, pe, cn: (i, 0, 0), memory_space=pltpu.SMEM),
                  pl.BlockSpec(memory_space=pl.ANY)],
        out_specs=pl.BlockSpec(memory_space=pl.ANY),
        scratch_shapes=[pltpu.VMEM((FFN_ROWS * ROW_TILE, LANES), F32),
                        pltpu.VMEM((DISPATCH_SLOTS, tm * ROW_TILE, LANES), F32),
                        pltpu.SemaphoreType.DMA(()), pltpu.SemaphoreType.DMA((DISPATCH_SLOTS,)),
                        pltpu.SemaphoreType.DMA((DISPATCH_SLOTS,))],
    )
    return pl.pallas_call(
        functools.partial(_dispatch_kernel, tm=tm, n_steps=n_steps),
        grid_spec=grid_spec,
        out_shape=jax.ShapeDtypeStruct((n_slots * ROW_TILE, LANES), F32),
        compiler_params=_cparams(("arbitrary",)),
        name="dispatch",
    )(pend, counts, dest3, h2t)


def _ffn_kernel(be_ref, nv_ref, x_ref, wg_ref, wu_ref, wd_ref, y_ref):
    del be_ref
    live = pl.program_id(0) < nv_ref[0]

    @pl.when(live)
    def _():
        x = _load_token_tiles(x_ref, FFN_ROWS).astype(BF16)
        gate = _dot(x, wg_ref[...])
        up = _dot(x, wu_ref[...])
        _store_token_tiles(y_ref, _dot((gate * _sigmoid(gate) * up).astype(BF16), wd_ref[...]))

    @pl.when(jnp.logical_not(live))
    def _():
        y_ref[...] = jnp.zeros_like(y_ref)


def _ffn(xs, block_e, n_live, wts):
    nb = xs.shape[0] // (FFN_ROWS * ROW_TILE)
    live_block = lambda i, be, nv: (jnp.minimum(i, nv[0] - 1), 0)
    grid_spec = pltpu.PrefetchScalarGridSpec(
        num_scalar_prefetch=2,
        grid=(nb,),
        in_specs=[pl.BlockSpec((FFN_ROWS * ROW_TILE, LANES), live_block),
                  pl.BlockSpec((None, D_MODEL, EXPERT_FF), lambda i, be, nv: (be[i], 0, 0)),
                  pl.BlockSpec((None, D_MODEL, EXPERT_FF), lambda i, be, nv: (be[i], 0, 0)),
                  pl.BlockSpec((None, EXPERT_FF, D_MODEL), lambda i, be, nv: (be[i], 0, 0))],
        out_specs=pl.BlockSpec((FFN_ROWS * ROW_TILE, LANES), lambda i, be, nv: (i, 0)),
    )
    return pl.pallas_call(
        _ffn_kernel,
        grid_spec=grid_spec,
        out_shape=jax.ShapeDtypeStruct(xs.shape, F32),
        compiler_params=_cparams(("arbitrary",)),
        name="ffn",
    )(block_e, n_live, xs, wts["w_eg"], wts["w_eu"], wts["w_ed"])


def _combine_kernel(dest_ref, dest_next_ref, x1_ref, rw_ref, yb_ref, o_ref, y_scr, sems, *, tm):
    i = pl.program_id(0)
    n = pl.num_programs(0)
    slot = i % 2
    tile = lambda ref, t, rows=1: ref.at[pl.ds(t * ROW_TILE, rows * ROW_TILE), :]

    def gather(dref, buf):
        def issue(t, carry):
            for s in range(TOP_K):
                d = dref[0, 0, s * tm + t]
                pltpu.make_async_copy(tile(yb_ref, d), tile(y_scr.at[buf, s], t), sems.at[buf]).start()
            return carry

        lax.fori_loop(0, tm, issue, 0, unroll=DMA_ISSUE_UNROLL)

    pl.when(i == 0)(lambda: gather(dest_ref, 0))
    pl.when(i + 1 < n)(lambda: gather(dest_next_ref, 1 - slot))
    for s in range(TOP_K):
        pltpu.make_async_copy(tile(yb_ref, 0, tm), y_scr.at[slot, s], sems.at[slot]).wait()
    rw = rw_ref[...]
    y = [_load_token_tiles(y_scr.at[slot, s], tm) for s in range(TOP_K)]
    o_ref[...] = x1_ref[...] + (rw[:, 0:1] * y[0] + rw[:, 1:2] * y[1])


def _combine(x1, rw, yb, dest, tm):
    T = x1.shape[0]
    nt = T // tm
    dest3 = _dest_blocks(dest, tm)
    dest_spec = lambda step: pl.BlockSpec((1, 1, TOP_K * tm), lambda i: (step(i), 0, 0), memory_space=pltpu.SMEM)
    return pl.pallas_call(
        functools.partial(_combine_kernel, tm=tm),
        grid=(nt,),
        in_specs=[dest_spec(lambda i: i), dest_spec(lambda i: jnp.minimum(i + 1, nt - 1)),
                  pl.BlockSpec((tm, D_MODEL), lambda i: (i, 0)),
                  pl.BlockSpec((tm, LANES), lambda i: (i, 0)),
                  pl.BlockSpec(memory_space=pl.ANY)],
        out_specs=pl.BlockSpec((tm, D_MODEL), lambda i: (i, 0)),
        out_shape=jax.ShapeDtypeStruct((T, D_MODEL), F32),
        scratch_shapes=[pltpu.VMEM((2, TOP_K, tm * ROW_TILE, LANES), F32), pltpu.SemaphoreType.DMA((2,))],
        compiler_params=_cparams(("arbitrary",)),
        name="combine",
    )(dest3, dest3, x1, rw, yb)


def _moe(x1, h2, ri, rw, cnt, wts, tm):
    T = x1.shape[0]
    ids = ri[0:TOP_K]
    ranks = ri[TOP_K:2 * TOP_K]
    counts = cnt[0, :N_EXPERTS]
    padded = (counts + FFN_ROWS - 1) // FFN_ROWS * FFN_ROWS
    pend = jnp.cumsum(padded)
    pstart = pend - padded
    experts = jnp.arange(N_EXPERTS, dtype=I32)[:, None, None]
    dest = (jnp.sum(jnp.where(ids[None] == experts, pstart[:, None, None], 0), axis=0) + ranks).astype(I32)
    nb = -(-T * TOP_K // FFN_ROWS) + N_EXPERTS
    block_pos = jnp.arange(nb, dtype=I32) * FFN_ROWS
    block_e = jnp.minimum(jnp.sum(pend[None, :] <= block_pos[:, None], axis=1), N_EXPERTS - 1).astype(I32)
    n_live = (pend[-1:] // FFN_ROWS).astype(I32)
    xs = _dispatch(h2, dest, pend.astype(I32), counts.astype(I32), nb * FFN_ROWS, min(DISPATCH_TOKENS, T))
    yb = _ffn(xs, block_e, n_live, wts)
    return _combine(x1, rw, yb, dest, tm)


def _rope_tables(pos):
    inv = 1.0 / (ROPE_BASE ** (jnp.arange(HALF_ROPE, dtype=F32) * (2.0 / QK_ROPE)))
    ang = pos.astype(F32)[:, None] * inv[None, :]
    cos, sin = jnp.cos(ang), jnp.sin(ang)
    n = pos.shape[0]
    pad = jnp.zeros((n, HEAD_PAD - QK_DIM), F32)
    ct = jnp.concatenate([jnp.ones((n, QK_NOPE), F32), cos, cos, pad], axis=1)
    st = jnp.concatenate([jnp.zeros((n, QK_NOPE), F32), -sin, sin, pad], axis=1)
    kt = jnp.concatenate([cos, sin], axis=1)
    return ct, st, kt


def _score_bound(q_head_norm_g, k_head_norm_g):
    return (1.02 * QK_DIM * ATTN_SCALE * math.log2(math.e)) * (
        jnp.max(jnp.abs(q_head_norm_g)) * jnp.max(jnp.abs(k_head_norm_g)))


def _prep_weights(norm_mix_g, w_in, q_norm_g, w_uq, kv_norm_g, w_ukv, q_head_norm_g, k_head_norm_g,
                  rwkv_mu, rwkv_w0, rwkv_w2, rwkv_a0, rwkv_a2, rwkv_g2, rwkv_k_k, rwkv_k_a, rwkv_r_k,
                  rwkv_lnx_g, rwkv_lnx_b, w_out, norm_ffn_g, w_router_group, b_router_group,
                  w_router_expert, b_router_expert, w_expert_gate, w_expert_up, w_expert_down):
    row = lambda a: a.reshape(1, -1).astype(F32)
    head_pad = lambda a: jnp.pad(a, ((0, 0), (0, 0), (0, HEAD_PAD - a.shape[-1])))
    w = {}
    w["mix_g"] = row(norm_mix_g)
    w["w_in"] = jnp.concatenate([w_in[:, :MLA_COLS], jnp.zeros((D_MODEL, LANES - QK_ROPE), F32),
                                 w_in[:, MLA_COLS:]], axis=1).astype(BF16)
    w["q_g"] = row(q_norm_g)
    w["kv_g"] = row(kv_norm_g)
    w["wqa"] = head_pad(w_uq).reshape(Q_LORA, MLA_HEADS * HEAD_PAD).astype(BF16)
    partner = jnp.concatenate([jnp.zeros_like(w_uq[..., :QK_NOPE]), w_uq[..., QK_NOPE + HALF_ROPE:],
                               w_uq[..., QK_NOPE:QK_NOPE + HALF_ROPE]], axis=-1)
    w["wqb"] = head_pad(partner).reshape(Q_LORA, MLA_HEADS * HEAD_PAD).astype(BF16)
    pad_g = lambda g: jnp.pad(g, (0, HEAD_PAD - QK_DIM)).reshape(1, HEAD_PAD).astype(F32)
    w["qh_g"] = pad_g(q_head_norm_g) * (ATTN_SCALE * math.log2(math.e))
    w["kh_g"] = pad_g(k_head_norm_g)
    lane_one = lambda lane: (jnp.arange(HEAD_PAD) == lane).astype(F32).reshape(1, HEAD_PAD)
    w["score_bound"] = _score_bound(q_head_norm_g, k_head_norm_g)
    w["q_one"] = lane_one(QK_DIM)
    w["k_shift"] = -w["score_bound"] * lane_one(QK_DIM)
    w["v_one"] = jnp.tile(lane_one(V_DIM), (1, MLA_HEADS))
    w["wk"] = head_pad(w_ukv[..., :QK_NOPE]).reshape(KV_LORA, MLA_HEADS * HEAD_PAD).astype(BF16)
    w["wv"] = head_pad(w_ukv[..., QK_NOPE:]).reshape(KV_LORA, MLA_HEADS * HEAD_PAD).astype(BF16)
    w["place"] = jnp.pad(jnp.eye(QK_ROPE, dtype=F32), ((0, 0), (QK_NOPE, HEAD_PAD - QK_DIM))).astype(BF16)
    w["mu"] = row(rwkv_mu)
    w["w0"] = row(rwkv_w0)
    w["w2"] = jnp.concatenate([rwkv_w2, jnp.zeros((A_LORA, RWKV_DIM), F32)], axis=0).astype(BF16)
    w["a0"] = row(rwkv_a0)
    w["a2"] = jnp.concatenate([jnp.zeros((W_LORA, RWKV_DIM), F32), rwkv_a2], axis=0).astype(BF16)
    w["g2"] = rwkv_g2.astype(BF16)
    w["k_k"] = row(rwkv_k_k)
    w["k_a"] = row(rwkv_k_a)
    w["r_k"] = row(rwkv_r_k)
    w["lnx_g"] = row(rwkv_lnx_g)
    w["lnx_b"] = row(rwkv_lnx_b)
    head_of = jnp.arange(RWKV_DIM) // RWKV_HEAD
    w["ones_bd"] = (head_of[:, None] == head_of[None, :]).astype(BF16)
    w["w_out_a"] = jnp.pad(w_out[0].reshape(MLA_HEADS, V_DIM, D_MODEL), ((0, 0), (0, HEAD_PAD - V_DIM), (0, 0))
                           ).reshape(MLA_HEADS * HEAD_PAD, D_MODEL).astype(BF16)
    w["w_out_b"] = w_out[1].astype(BF16)
    w["ffn_g"] = row(norm_ffn_g)
    wr = jnp.concatenate([w_router_expert.reshape(D_MODEL, N_EXPERTS), w_router_group,
                          jnp.zeros((D_MODEL, LANES - N_EXPERTS - N_GROUPS), F32)], axis=1)
    w["wr_hi"] = wr.astype(BF16)
    w["wr_lo"] = (wr - w["wr_hi"].astype(F32)).astype(BF16)
    w["b_r"] = jnp.concatenate([b_router_expert.reshape(-1), b_router_group,
                                jnp.zeros((LANES - N_EXPERTS - N_GROUPS,), F32)]).reshape(1, LANES)
    w["w_eg"] = w_expert_gate.astype(BF16)
    w["w_eu"] = w_expert_up.astype(BF16)
    w["w_ed"] = w_expert_down.astype(BF16)
    return w


def _round_up(n, m):
    return -(-n // m) * m


def _layer(x, pos, lat_past, kpe_past, wkv0, shift0, wts):
    B, L, _ = x.shape
    T = B * L
    tm = min(256, T)
    ct, st, kt = _rope_tables(pos)
    if L % tm:
        reps = tm // L
        ct, st, kt = (jnp.tile(t, (reps, 1)) for t in (ct, st, kt))
    q, lat, kpe, z, gl = _inproj(x.reshape(T, D_MODEL), (ct, st, kt), wts, tm)
    lat3 = lat.reshape(B, L, KV_LORA)
    kpe3 = kpe.reshape(B, L, QK_ROPE)
    if lat_past is None:
        n_past, lk, lat_all, kpe_all = 0, L, lat3, kpe3
        tq, tk = min(ATTN_QUERY_TILE, L), min(ATTN_KEY_TILE, L)
        diag = True
    else:
        n_past = lat_past.shape[1]
        lk_valid = n_past + L
        lk = _round_up(lk_valid, LANES)
        padk = lambda a: jnp.pad(a, ((0, 0), (0, lk - lk_valid), (0, 0)))
        lat_all = padk(jnp.concatenate([lat_past, lat3], axis=1))
        kpe_all = padk(jnp.concatenate([kpe_past, kpe3], axis=1))
        tq, tk, diag = L, lk, False
    k, v = _kvprep(lat_all.reshape(B * lk, KV_LORA), kpe_all.reshape(B * lk, QK_ROPE), wts, min(512, B * lk))
    oa = _attention(q.reshape(B, L, -1), k.reshape(B, lk, -1), v.reshape(B, lk, -1),
                    wts["score_bound"] <= MAX_DIRECT_EXP2_SHIFT, tq=tq, tk=tk, n_past=n_past,
                    lk_valid=n_past + L, diag_tiles=diag)
    z3 = z.reshape(B, L, RWKV_COLS)
    ob, s_new = _rwkv(z3, shift0, wkv0, wts, min(SCAN_CHUNK, L))
    x1, h2, ri, rw, cnt = _merge(x.reshape(T, D_MODEL), oa.reshape(T, -1), ob.reshape(T, -1), gl, wts, tm)
    y = _moe(x1, h2, ri, rw, cnt, wts, tm)
    return y.reshape(B, L, D_MODEL), lat3, kpe3, s_new, z3[:, L - 1:, :]


def kernel(x_prompt, x_sample, cache_kv_latent, cache_k_rope, state_wkv, state_shift, norm_mix_g, w_in, q_norm_g, w_uq, kv_norm_g, w_ukv, q_head_norm_g, k_head_norm_g, rwkv_mu, rwkv_w0, rwkv_w2, rwkv_a0, rwkv_a2, rwkv_g2, rwkv_k_k, rwkv_k_a, rwkv_r_k, rwkv_lnx_g, rwkv_lnx_b, w_out, norm_ffn_g, w_router_group, b_router_group, w_router_expert, b_router_expert, w_expert_gate, w_expert_up, w_expert_down):
    weights = (norm_mix_g, w_in, q_norm_g, w_uq, kv_norm_g, w_ukv, q_head_norm_g, k_head_norm_g,
               rwkv_mu, rwkv_w0, rwkv_w2, rwkv_a0, rwkv_a2, rwkv_g2, rwkv_k_k, rwkv_k_a, rwkv_r_k,
               rwkv_lnx_g, rwkv_lnx_b, w_out, norm_ffn_g, w_router_group, b_router_group,
               w_router_expert, b_router_expert, w_expert_gate, w_expert_up, w_expert_down)
    depth = w_in.shape[0]
    bp, lp, _ = x_prompt.shape
    n_past = cache_kv_latent.shape[2]
    pos_p = jnp.arange(lp, dtype=I32)
    pos_s = n_past + jnp.arange(x_sample.shape[1], dtype=I32)
    wkv_zero = jnp.zeros((bp, RWKV_HEADS, RWKV_HEAD, RWKV_HEAD), F32)
    shift_zero = jnp.zeros((bp, 1, RWKV_COLS), F32)
    xp, xs = x_prompt, x_sample
    outs_p, outs_s = [], []
    for l in range(depth):
        wts = _prep_weights(*(wt[l] for wt in weights))
        xp, *rest_p = _layer(xp, pos_p, None, None, wkv_zero, shift_zero, wts)
        xs, *rest_s = _layer(xs, pos_s, cache_kv_latent[l], cache_k_rope[l], state_wkv[l], state_shift[l], wts)
        outs_p.append(rest_p)
        outs_s.append(rest_s)
    stack = lambda outs, i: jnp.stack([o[i] for o in outs], 0)
    return (xp, xs, stack(outs_p, 0), stack(outs_p, 1), stack(outs_p, 2), stack(outs_p, 3),
            stack(outs_s, 0), stack(outs_s, 1), stack(outs_s, 2), stack(outs_s, 3))
```

```python
import functools
import math

import jax
import jax.numpy as jnp
from jax import lax
from jax.experimental import pallas as pl
from jax.experimental.pallas import tpu as pltpu

F32 = jnp.float32
BF16 = jnp.bfloat16
I32 = jnp.int32

D_MODEL = 1024
CHUNK = 64
RMS_EPS = 1e-6
MASK_VALUE = -1e30
MLA_HEADS = 8
QK_NOPE = 64
QK_ROPE = 32
QK_DIM = QK_NOPE + QK_ROPE
V_DIM = 64
Q_LORA = 256
KV_LORA = 128
ROPE_BASE = 10000.0
ATTN_SCALE = QK_DIM ** -0.5
RWKV_HEADS = 8
RWKV_HEAD = 64
RWKV_DIM = RWKV_HEADS * RWKV_HEAD
W_LORA = 64
A_LORA = 64
G_LORA = 128
LN_X_EPS = 64e-5
MLA_COLS = Q_LORA + KV_LORA + QK_ROPE
RWKV_COLS = 3 * RWKV_DIM + W_LORA + A_LORA + G_LORA
GATE_COLS = 2 * D_MODEL
N_GROUPS = 4
EXPERTS_PER_GROUP = 8
N_EXPERTS = N_GROUPS * EXPERTS_PER_GROUP
EXPERT_FF = 256
TOP_K = 2

LANES = 128
ROW_TILE = D_MODEL // LANES
DMA_ISSUE_UNROLL = 4
DISPATCH_TOKENS = 512
DISPATCH_SLOTS = 3
HEAD_PAD = LANES
HALF_ROPE = QK_ROPE // 2
SEG_Q = 0
SEG_KV = SEG_Q + Q_LORA
SEG_PE = SEG_KV + KV_LORA
SEG_Z = SEG_PE + LANES
SEG_G = SEG_Z + RWKV_COLS
IN_COLS_PAD = SEG_G + GATE_COLS
ROUTE_GROUP_LANE = N_EXPERTS
ROUTE_FIELDS = 8
SCAN_CHUNK = 128
SCAN_HEAD_GROUP = 8
FFN_ROWS = 256
ATTN_QUERY_TILE = 512
ATTN_KEY_TILE = 512
MAX_DIRECT_EXP2_SHIFT = 40.0
VMEM_LIMIT = 48 * 1024 * 1024


def _cparams(sem, vmem=VMEM_LIMIT):
    return pltpu.CompilerParams(dimension_semantics=sem, vmem_limit_bytes=vmem)


def _dot(a, b):
    return jnp.dot(a, b, preferred_element_type=F32)


def _dot_nt(a, b):
    return lax.dot_general(a, b, (((1,), (1,)), ((), ())), preferred_element_type=F32)


def _dot_tn(a, b):
    return lax.dot_general(a, b, (((0,), (0,)), ((), ())), preferred_element_type=F32)


def _split(x):
    hi = x.astype(BF16)
    lo = (x - hi.astype(F32)).astype(BF16)
    return hi, lo


def _dot2_exact_rhs(a, b_bf16):
    ah, al = _split(a)
    return _dot(ah, b_bf16) + _dot(al, b_bf16)


def _sigmoid(x):
    return 1.0 / (1.0 + jnp.exp(-x))


def _rms(x, n=None):
    n = x.shape[-1] if n is None else n
    return lax.rsqrt(jnp.sum(x * x, axis=-1, keepdims=True) * (1.0 / n) + RMS_EPS)


def _store_token_tiles(ref, x):
    n = x.shape[0]
    for c in range(ROW_TILE):
        ref[pl.ds(c, n, stride=ROW_TILE), :] = x[:, c * LANES:(c + 1) * LANES]


def _load_token_tiles(ref, n):
    return jnp.concatenate([ref[pl.ds(c, n, stride=ROW_TILE), :] for c in range(ROW_TILE)], axis=1)


def _inproj_kernel(x_ref, g_ref, w_ref, qg_ref, wqa_ref, wqb_ref, ct_ref, st_ref, kt_ref, kvg_ref, qhg_ref, qone_ref,
                   q_ref, lat_ref, kpe_ref, z_ref, gl_ref):
    x = x_ref[...]
    h = (x * _rms(x) * g_ref[...]).astype(BF16)
    cq = _dot(h, w_ref[:, SEG_Q:SEG_KV])
    ckv = _dot(h, w_ref[:, SEG_KV:SEG_PE])
    pe = _dot(h, w_ref[:, SEG_PE:SEG_Z])
    z_ref[...] = _dot(h, w_ref[:, SEG_Z:SEG_G])
    gl_ref[...] = _dot(h, w_ref[:, SEG_G:IN_COLS_PAD]).astype(gl_ref.dtype)
    lat_ref[...] = ckv * _rms(ckv) * kvg_ref[...]
    x1 = pe[:, :HALF_ROPE]
    x2 = pe[:, HALF_ROPE:QK_ROPE]
    c = kt_ref[:, :HALF_ROPE]
    s = kt_ref[:, HALF_ROPE:QK_ROPE]
    kpe_ref[...] = jnp.concatenate([x1 * c - x2 * s, x1 * s + x2 * c], axis=-1)
    cqn = (cq * _rms(cq) * qg_ref[...]).astype(BF16)
    ct = ct_ref[...]
    st = st_ref[...]
    qhg = qhg_ref[...]
    qone = qone_ref[...]
    for hd in range(MLA_HEADS):
        sl = slice(hd * HEAD_PAD, (hd + 1) * HEAD_PAD)
        xh = _dot(cqn, wqa_ref[:, sl]) * ct + _dot(cqn, wqb_ref[:, sl]) * st
        q_ref[:, sl] = (xh * _rms(xh, QK_DIM) * qhg + qone).astype(BF16)


def _inproj(x, tabs, wts, tm):
    T = x.shape[0]
    ct, st, kt = tabs
    nt = ct.shape[0] // tm
    row = lambda i: (i, 0)
    tab = lambda i: (i % nt, 0)
    fixed = lambda i: (0, 0)
    full = lambda a: pl.BlockSpec(a.shape, fixed)
    return pl.pallas_call(
        _inproj_kernel,
        grid=(T // tm,),
        in_specs=[pl.BlockSpec((tm, D_MODEL), row), full(wts["mix_g"]), full(wts["w_in"]), full(wts["q_g"]),
                  full(wts["wqa"]), full(wts["wqb"]),
                  pl.BlockSpec((tm, HEAD_PAD), tab), pl.BlockSpec((tm, HEAD_PAD), tab),
                  pl.BlockSpec((tm, QK_ROPE), tab), full(wts["kv_g"]), full(wts["qh_g"]), full(wts["q_one"])],
        out_specs=[pl.BlockSpec((tm, MLA_HEADS * HEAD_PAD), row), pl.BlockSpec((tm, KV_LORA), row),
                   pl.BlockSpec((tm, QK_ROPE), row), pl.BlockSpec((tm, RWKV_COLS), row),
                   pl.BlockSpec((tm, GATE_COLS), row)],
        out_shape=[jax.ShapeDtypeStruct((T, MLA_HEADS * HEAD_PAD), BF16), jax.ShapeDtypeStruct((T, KV_LORA), F32),
                   jax.ShapeDtypeStruct((T, QK_ROPE), F32), jax.ShapeDtypeStruct((T, RWKV_COLS), F32),
                   jax.ShapeDtypeStruct((T, GATE_COLS), BF16)],
        compiler_params=_cparams(("arbitrary",)),
        name="inproj",
    )(x, wts["mix_g"], wts["w_in"], wts["q_g"], wts["wqa"], wts["wqb"], ct, st, kt, wts["kv_g"], wts["qh_g"],
      wts["q_one"])


def _kvprep_kernel(lat_ref, kpe_ref, wk_ref, wv_ref, place_ref, kg_ref, kshift_ref, vone_ref, k_ref, v_ref):
    lat = lat_ref[...].astype(BF16)
    rot = _dot2_exact_rhs(kpe_ref[...], place_ref[...])
    kg = kg_ref[...]
    kshift = kshift_ref[...]
    for hd in range(MLA_HEADS):
        sl = slice(hd * HEAD_PAD, (hd + 1) * HEAD_PAD)
        kh = _dot(lat, wk_ref[:, sl]) + rot
        k_ref[:, sl] = (kh * _rms(kh, QK_DIM) * kg + kshift).astype(BF16)
    v_ref[...] = (_dot(lat, wv_ref[...]) + vone_ref[...]).astype(BF16)


def _kvprep(lat, kpe, wts, tm):
    T = lat.shape[0]
    row = lambda i: (i, 0)
    fixed = lambda i: (0, 0)
    full = lambda a: pl.BlockSpec(a.shape, fixed)
    return pl.pallas_call(
        _kvprep_kernel,
        grid=(T // tm,),
        in_specs=[pl.BlockSpec((tm, KV_LORA), row), pl.BlockSpec((tm, QK_ROPE), row), full(wts["wk"]),
                  full(wts["wv"]), full(wts["place"]), full(wts["kh_g"]), full(wts["k_shift"]),
                  full(wts["v_one"])],
        out_specs=[pl.BlockSpec((tm, MLA_HEADS * HEAD_PAD), row), pl.BlockSpec((tm, MLA_HEADS * HEAD_PAD), row)],
        out_shape=[jax.ShapeDtypeStruct((T, MLA_HEADS * HEAD_PAD), BF16),
                   jax.ShapeDtypeStruct((T, MLA_HEADS * HEAD_PAD), BF16)],
        compiler_params=_cparams(("arbitrary",)),
        name="kvprep",
    )(lat, kpe, wts["wk"], wts["wv"], wts["place"], wts["kh_g"], wts["k_shift"], wts["v_one"])


def _attn_kernel(q_ref, k_ref, v_ref, o_ref, *, tq, tk, n_past, lk_valid, diag_tiles, bounded):
    i = pl.program_id(2)
    n_full = i * (tq // tk) if diag_tiles else 0
    head_lanes = [slice(hh * HEAD_PAD, (hh + 1) * HEAD_PAD) for hh in range(2)]
    qs = [q_ref[:, hs] for hs in head_lanes]

    def scores(hh, j, masked):
        ks = pl.multiple_of(j * tk, tk)
        s = _dot_nt(qs[hh], k_ref[pl.ds(ks, tk), head_lanes[hh]])
        if masked:
            qpos = n_past + i * tq + lax.broadcasted_iota(I32, (tq, tk), 0)
            kpos = j * tk + lax.broadcasted_iota(I32, (tq, tk), 1)
            vis = jnp.logical_and(jnp.right_shift(kpos, 6) <= jnp.right_shift(qpos, 6), kpos < lk_valid)
            s = jnp.where(vis, s, MASK_VALUE)
        return s, v_ref[pl.ds(ks, tk), head_lanes[hh]]

    if bounded:
        def head_step(hh, j, acc, masked):
            s, v = scores(hh, j, masked)
            return acc + _dot(jnp.exp2(s).astype(BF16), v)

        init = jnp.zeros((tq, HEAD_PAD), F32)
    else:
        def head_step(hh, j, carry, masked):
            m, acc = carry
            s, v = scores(hh, j, masked)
            m_new = jnp.maximum(m, jnp.max(s, axis=-1, keepdims=True))
            return m_new, jnp.exp2(m - m_new) * acc + _dot(jnp.exp2(s - m_new).astype(BF16), v)

        init = (jnp.full((tq, 1), MASK_VALUE, F32), jnp.zeros((tq, HEAD_PAD), F32))

    step = lambda j, carry, masked: tuple(head_step(hh, j, carry[hh], masked) for hh in range(2))
    carry = (init, init)
    if diag_tiles:
        carry = lax.fori_loop(0, n_full, functools.partial(step, masked=False), carry)
    for d in range(tq // tk if diag_tiles else 1):
        carry = step(n_full + d, carry, True)
    for hh in range(2):
        acc = carry[hh] if bounded else carry[hh][1]
        o_ref[:, head_lanes[hh]] = (acc / acc[:, V_DIM:V_DIM + 1]).astype(o_ref.dtype)


def _attention(q, k, v, bound_ok, *, tq, tk, n_past, lk_valid, diag_tiles):
    B, L, _ = q.shape
    Lk = k.shape[1]

    def run(bounded):
        kern = functools.partial(_attn_kernel, tq=tq, tk=tk, n_past=n_past, lk_valid=lk_valid,
                                 diag_tiles=diag_tiles, bounded=bounded)
        return pl.pallas_call(
            kern,
            grid=(B, MLA_HEADS // 2, L // tq),
            in_specs=[pl.BlockSpec((None, tq, 2 * HEAD_PAD), lambda b, p, i: (b, i, p)),
                      pl.BlockSpec((None, Lk, 2 * HEAD_PAD), lambda b, p, i: (b, 0, p)),
                      pl.BlockSpec((None, Lk, 2 * HEAD_PAD), lambda b, p, i: (b, 0, p))],
            out_specs=pl.BlockSpec((None, tq, 2 * HEAD_PAD), lambda b, p, i: (b, i, p)),
            out_shape=jax.ShapeDtypeStruct((B, L, MLA_HEADS * HEAD_PAD), BF16),
            compiler_params=_cparams(("arbitrary", "arbitrary", "arbitrary")),
            name="attn_bounded" if bounded else "attn_running_max",
        )(q, k, v)

    return lax.cond(bound_ok, lambda: run(True), lambda: run(False))


def _rwkv_kernel(z_ref, shift_ref, s0_ref, mu_ref, w0_ref, w2_ref, a0_ref, a2_ref, g2_ref, kk_ref, ka_ref, rk_ref,
                 lng_ref, lnb_ref, ones_ref, o_ref, sout_ref, s_scr, prev_scr, *, C):
    c = pl.program_id(1)

    @pl.when(c == 0)
    def _():
        s_scr[...] = s0_ref[...]
        prev_scr[...] = shift_ref[...]

    zf = z_ref[...]
    row = lax.broadcasted_iota(I32, (C, 1), 0)
    pv = jnp.where(row == 0, prev_scr[...], pltpu.roll(zf, 1, axis=0))
    prev_scr[...] = zf[C - 1:C, :]
    zm = zf + (pv - zf) * mu_ref[...]
    H = RWKV_DIM
    r = zm[:, 0:H]
    k = zm[:, H:2 * H]
    v = zm[:, 2 * H:3 * H]
    xwa = zm[:, 3 * H:3 * H + W_LORA + A_LORA]
    xg = zm[:, 3 * H + W_LORA + A_LORA:]
    lw = (-math.exp(-0.5)) * _sigmoid(w0_ref[...] + _dot(jnp.tanh(xwa).astype(BF16), w2_ref[...]))
    a = _sigmoid(a0_ref[...] + _dot(xwa.astype(BF16), a2_ref[...]))
    g = _dot(_sigmoid(xg).astype(BF16), g2_ref[...])
    ones_bd = ones_ref[...]
    kk = k * kk_ref[...]
    kkn = kk * lax.rsqrt(jnp.maximum(_dot2_exact_rhs(kk * kk, ones_bd), 1e-24))
    k2 = k * (1.0 + (a - 1.0) * ka_ref[...])
    head_sum = lambda x: _dot(x.astype(BF16), ones_bd)
    bonus = head_sum(r * k2 * rk_ref[...]) * v

    ti = lax.broadcasted_iota(I32, (C, C), 0)
    tj = lax.broadcasted_iota(I32, (C, C), 1)
    incl = ti >= tj
    strict = ti > tj
    tri = jnp.where(incl, 1.0, 0.0).astype(BF16)
    p1 = lw.astype(BF16)
    r1 = lw - p1.astype(F32)
    p2 = r1.astype(BF16)
    p3 = (r1 - p2.astype(F32)).astype(BF16)
    cs = _dot(tri, p1) + (_dot(tri, p2) + _dot(tri, p3))
    w_incl = jnp.exp(cs)
    w_inv = jnp.exp(-cs)
    at = -kkn * jnp.exp(cs - lw)
    rt = r * w_incl
    bt = kkn * a * w_inv
    kt = k2 * w_inv
    wc = w_incl[C - 1:C, :]
    bh = bt * wc
    kh = kt * wc
    eye = jnp.where(ti == tj, 1.0, 0.0).astype(F32)

    per_head = lambda x: [x[:, hd * RWKV_HEAD:(hd + 1) * RWKV_HEAD] for hd in range(RWKV_HEADS)]
    rows = lambda xs: jnp.concatenate(xs, axis=0)
    side = lambda xs: jnp.concatenate(xs, axis=1)
    hi_lo = lambda x: tuple(per_head(p) for p in _split(x))
    cast = lambda x: per_head(x.astype(BF16))
    at_b, rt_b, bt_b, kt_b = cast(at), cast(rt), cast(bt), cast(kt)
    (v_hi, v_lo), (bh_hi, bh_lo), (kh_hi, kh_lo) = hi_lo(v), hi_lo(bh), hi_lo(kh)
    wc_h = per_head(wc)

    def dot_hl(a, b_hi, b_lo):
        ah, al = _split(a)
        return _dot(side([ah, al, ah]), rows([b_hi, b_hi, b_lo]))

    def head_group(heads):
        s_all = {h: s_scr[h] for h in heads}
        s_b = {h: s_all[h].astype(BF16) for h in heads}
        pr = {h: _dot_nt(rows([at_b[h], rt_b[h]]), rows([bt_b[h], kt_b[h]])) for h in heads}
        a_ab = {h: jnp.where(strict, pr[h][:C, :C], 0.0) for h in heads}
        a_ak = {h: jnp.where(strict, pr[h][:C, C:], 0.0).astype(BF16) for h in heads}
        a_rb = {h: jnp.where(incl, pr[h][C:, :C], 0.0).astype(BF16) for h in heads}
        a_rk = {h: jnp.where(incl, pr[h][C:, C:], 0.0).astype(BF16) for h in heads}
        pw = a_ab
        tinv = {h: eye + pw[h] for h in heads}
        n = 2
        while n < C:
            pw_b = {h: pw[h].astype(BF16) for h in heads}
            pw = {h: _dot(pw_b[h], pw_b[h]) for h in heads}
            tinv = {h: tinv[h] + _dot(tinv[h].astype(BF16), pw[h].astype(BF16)) for h in heads}
            n *= 2
        resid = {h: (eye - tinv[h]) + dot_hl(a_ab[h], *_split(tinv[h])) for h in heads}
        tinv = {h: tinv[h] + _dot(tinv[h].astype(BF16), resid[h].astype(BF16)) for h in heads}
        x = {h: _dot_nt(at_b[h], s_b[h]) + _dot(a_ak[h], v_hi[h]) for h in heads}
        u = {h: _split(_dot(tinv[h].astype(BF16), x[h].astype(BF16))) for h in heads}
        ys = [_dot_nt(rt_b[h], s_b[h]) + _dot(side([a_rb[h], a_rk[h]]), rows([u[h][0], v_hi[h]])) for h in heads]
        for h in heads:
            u_hi, u_lo = u[h]
            upd = _dot_tn(rows([u_hi, v_hi[h], u_lo, v_lo[h], u_hi, v_hi[h]]),
                          rows([bh_hi[h], kh_hi[h], bh_hi[h], kh_hi[h], bh_lo[h], kh_lo[h]]))
            s_scr[h] = s_all[h] * wc_h[h] + upd
        return ys

    ys = []
    for first in range(0, RWKV_HEADS, SCAN_HEAD_GROUP):
        ys += head_group(range(first, first + SCAN_HEAD_GROUP))
    y = jnp.concatenate(ys, axis=-1)
    inv_n = 1.0 / RWKV_HEAD
    mean = head_sum(y) * inv_n
    d = y - mean
    var = head_sum(d * d) * inv_n
    yn = d * lax.rsqrt(var + LN_X_EPS) * lng_ref[...] + lnb_ref[...]
    o_ref[...] = ((yn + bonus) * g).astype(o_ref.dtype)

    @pl.when(c == pl.num_programs(1) - 1)
    def _():
        sout_ref[...] = s_scr[...]


def _rwkv(z, shift0, s0, wts, C):
    B, L, _ = z.shape
    fixed = lambda b, c: (0, 0)
    full = lambda a: pl.BlockSpec(a.shape, fixed)
    names = ["mu", "w0", "w2", "a0", "a2", "g2", "k_k", "k_a", "r_k", "lnx_g", "lnx_b", "ones_bd"]
    return pl.pallas_call(
        functools.partial(_rwkv_kernel, C=C),
        grid=(B, L // C),
        in_specs=[pl.BlockSpec((None, C, RWKV_COLS), lambda b, c: (b, c, 0)),
                  pl.BlockSpec((None, 1, RWKV_COLS), lambda b, c: (b, 0, 0)),
                  pl.BlockSpec((None, RWKV_HEADS, RWKV_HEAD, RWKV_HEAD), lambda b, c: (b, 0, 0, 0))]
                 + [full(wts[n]) for n in names],
        out_specs=[pl.BlockSpec((None, C, RWKV_DIM), lambda b, c: (b, c, 0)),
                   pl.BlockSpec((None, RWKV_HEADS, RWKV_HEAD, RWKV_HEAD), lambda b, c: (b, 0, 0, 0))],
        out_shape=[jax.ShapeDtypeStruct((B, L, RWKV_DIM), BF16),
                   jax.ShapeDtypeStruct((B, RWKV_HEADS, RWKV_HEAD, RWKV_HEAD), F32)],
        scratch_shapes=[pltpu.VMEM((RWKV_HEADS, RWKV_HEAD, RWKV_HEAD), F32), pltpu.VMEM((1, RWKV_COLS), F32)],
        compiler_params=_cparams(("arbitrary", "arbitrary")),
        name="rwkv",
    )(z, shift0, s0, *[wts[n] for n in names])


def _merge_kernel(x_ref, oa_ref, ob_ref, gl_ref, woa_ref, wob_ref, fg_ref, wrh_ref, wrl_ref, br_ref,
                  x1_ref, h2_ref, ri_ref, rw_ref, cnt_ref, run_scr, *, tm):
    @pl.when(pl.program_id(0) == 0)
    def _():
        run_scr[...] = jnp.zeros_like(run_scr)

    pa = _dot(oa_ref[...], woa_ref[...])
    pb = _dot(ob_ref[...], wob_ref[...])
    gate = lambda n: _sigmoid(gl_ref[:, n * D_MODEL:(n + 1) * D_MODEL].astype(F32))
    x1 = x_ref[...] + (gate(0) * pa + gate(1) * pb)
    x1_ref[...] = x1
    h2 = x1 * _rms(x1) * fg_ref[...]
    _store_token_tiles(h2_ref, h2)
    hh, hl = _split(h2)
    lg = _dot(hh, wrh_ref[...]) + (_dot(hl, wrh_ref[...]) + _dot(hh, wrl_ref[...])) + br_ref[...]

    lane = lax.broadcasted_iota(I32, (tm, LANES), 1)
    lane_f = lane.astype(F32)
    first = lambda hit: jnp.min(jnp.where(hit, lane_f, float(LANES)), axis=-1, keepdims=True).astype(I32)
    is_group = jnp.logical_and(lane >= ROUTE_GROUP_LANE, lane < ROUTE_GROUP_LANE + N_GROUPS)
    gl = jnp.where(is_group, lg, MASK_VALUE)
    gmax = jnp.max(gl, axis=-1, keepdims=True)
    gidx = first(gl == gmax) - ROUTE_GROUP_LANE
    g_top = 1.0 / jnp.sum(jnp.where(is_group, jnp.exp(gl - gmax), 0.0), axis=-1, keepdims=True)
    in_group = jnp.logical_and(lane < N_EXPERTS, jnp.right_shift(lane, 3) == gidx)
    el = jnp.where(in_group, lg, MASK_VALUE)
    e1 = jnp.max(el, axis=-1, keepdims=True)
    i1 = first(el == e1)
    el2 = jnp.where(lane == i1, MASK_VALUE, el)
    e2 = jnp.max(el2, axis=-1, keepdims=True)
    i2 = first(el2 == e2)
    t = jnp.exp(e2 - e1)
    w1 = g_top / (1.0 + t)
    w2 = g_top * t / (1.0 + t)

    hit1 = lane == i1
    hit2 = lane == i2
    oh = jnp.where(jnp.logical_or(hit1, hit2), 1.0, 0.0)
    ti = lax.broadcasted_iota(I32, (tm, tm), 0)
    tj = lax.broadcasted_iota(I32, (tm, tm), 1)
    before = _dot(jnp.where(ti > tj, 1.0, 0.0).astype(BF16), oh.astype(BF16)) + run_scr[...]
    rank1 = jnp.sum(jnp.where(hit1, before, 0.0), axis=-1, keepdims=True)
    rank2 = jnp.sum(jnp.where(hit2, before, 0.0), axis=-1, keepdims=True)
    run = run_scr[...] + jnp.sum(oh, axis=0, keepdims=True)
    run_scr[...] = run
    cnt_ref[...] = run.astype(I32)

    sel = lambda vals: functools.reduce(lambda acc, kv: jnp.where(lane == kv[0], kv[1], acc), enumerate(vals), 0)
    fields = sel([i1.astype(F32), i2.astype(F32), rank1, rank2])
    ri_ref[...] = jnp.transpose(fields)[:ROUTE_FIELDS, :].astype(I32)
    rw_ref[...] = sel([w1, w2])


def _merge(x, oa, ob, gl, wts, tm):
    T = x.shape[0]
    row = lambda i: (i, 0)
    full = lambda a: pl.BlockSpec(a.shape, lambda i: (0,) * a.ndim)
    names = ["w_out_a", "w_out_b", "ffn_g", "wr_hi", "wr_lo", "b_r"]
    return pl.pallas_call(
        functools.partial(_merge_kernel, tm=tm),
        grid=(T // tm,),
        in_specs=[pl.BlockSpec((tm, D_MODEL), row), pl.BlockSpec((tm, MLA_HEADS * HEAD_PAD), row),
                  pl.BlockSpec((tm, RWKV_DIM), row), pl.BlockSpec((tm, GATE_COLS), row)]
                 + [full(wts[n]) for n in names],
        out_specs=[pl.BlockSpec((tm, D_MODEL), row), pl.BlockSpec((tm * ROW_TILE, LANES), row),
                   pl.BlockSpec((ROUTE_FIELDS, tm), lambda i: (0, i)), pl.BlockSpec((tm, LANES), row),
                   pl.BlockSpec((1, LANES), lambda i: (0, 0))],
        out_shape=[jax.ShapeDtypeStruct((T, D_MODEL), F32), jax.ShapeDtypeStruct((T * ROW_TILE, LANES), F32),
                   jax.ShapeDtypeStruct((ROUTE_FIELDS, T), I32), jax.ShapeDtypeStruct((T, LANES), F32),
                   jax.ShapeDtypeStruct((1, LANES), I32)],
        scratch_shapes=[pltpu.VMEM((1, LANES), F32)],
        compiler_params=_cparams(("arbitrary",)),
        name="merge",
    )(x, oa, ob, gl, *[wts[n] for n in names])


def _dispatch_kernel(pend_ref, cnt_ref, dest_ref, h_ref, xs_ref, zero_scr, buf, sem_fill, sem_in, sem_out, *,
                     tm, n_steps):
    i = pl.program_id(0)
    tile = lambda ref, slot, rows=1: ref.at[pl.ds(slot * ROW_TILE, rows * ROW_TILE), :]
    load = lambda step, slot: pltpu.make_async_copy(tile(h_ref, step * tm, tm), buf.at[slot], sem_in.at[slot])
    step_drain = lambda slot: pltpu.make_async_copy(buf.at[slot], tile(xs_ref, 0, tm), sem_out.at[slot])

    @pl.when(i == 0)
    def _():
        zero_scr[...] = jnp.zeros_like(zero_scr)
        fill = lambda e: pltpu.make_async_copy(zero_scr, tile(xs_ref, pend_ref[e] - FFN_ROWS, FFN_ROWS), sem_fill)
        for e in range(N_EXPERTS):
            pl.when(cnt_ref[e] > 0)(lambda e=e: fill(e).start())
        for e in range(N_EXPERTS):
            pl.when(cnt_ref[e] > 0)(lambda e=e: fill(e).wait())
        dead = lambda b: pltpu.make_async_copy(zero_scr, tile(xs_ref, b * FFN_ROWS, FFN_ROWS), sem_fill)
        first_dead = pend_ref[N_EXPERTS - 1] // FFN_ROWS
        n_blocks = xs_ref.shape[0] // (FFN_ROWS * ROW_TILE)
        lax.fori_loop(first_dead, n_blocks, lambda b, c: (dead(b).start(), c)[1], 0)
        lax.fori_loop(first_dead, n_blocks, lambda b, c: (dead(b).wait(), c)[1], 0)
        load(0, 0).start()

    slot = i % DISPATCH_SLOTS
    nxt = (i + 1) % DISPATCH_SLOTS

    @pl.when(i >= DISPATCH_SLOTS - 1)
    def _():
        for s in range(TOP_K):
            step_drain(nxt).wait()

    pl.when(i + 1 < n_steps)(lambda: load(i + 1, nxt).start())
    load(i, slot).wait()

    def issue(t, carry):
        for s in range(TOP_K):
            d = dest_ref[0, 0, s * tm + t]
            pltpu.make_async_copy(tile(buf.at[slot], t), tile(xs_ref, d), sem_out.at[slot]).start(priority=s)
        return carry

    lax.fori_loop(0, tm, issue, 0, unroll=DMA_ISSUE_UNROLL)

    @pl.when(i == n_steps - 1)
    def _():
        for back in range(min(DISPATCH_SLOTS - 1, n_steps)):
            for s in range(TOP_K):
                step_drain((i - back) % DISPATCH_SLOTS).wait()


def _dest_blocks(dest, tm):
    T = dest.shape[1]
    return dest.reshape(TOP_K, T // tm, tm).transpose(1, 0, 2).reshape(T // tm, 1, TOP_K * tm)


def _dispatch(h2t, dest, pend, counts, n_slots, tm):
    T = h2t.shape[0] // ROW_TILE
    n_steps = T // tm
    dest3 = _dest_blocks(dest, tm)
    grid_spec = pltpu.PrefetchScalarGridSpec(
        num_scalar_prefetch=2,
        grid=(n_steps,),
        in_specs=[pl.BlockSpec((1, 1, TOP_K * tm), lambda i, pe, cn: (i, 0, 0), memory_space=pltpu.SMEM),
                  pl.BlockSpec(memory_space=pl.ANY)],
        out_specs=pl.BlockSpec(memory_space=pl.ANY),
        scratch_shapes=[pltpu.VMEM((FFN_ROWS * ROW_TILE, LANES), F32),
                        pltpu.VMEM((DISPATCH_SLOTS, tm * ROW_TILE, LANES), F32),
                        pltpu.SemaphoreType.DMA(()), pltpu.SemaphoreType.DMA((DISPATCH_SLOTS,)),
                        pltpu.SemaphoreType.DMA((DISPATCH_SLOTS,))],
    )
    return pl.pallas_call(
        functools.partial(_dispatch_kernel, tm=tm, n_steps=n_steps),
        grid_spec=grid_spec,
        out_shape=jax.ShapeDtypeStruct((n_slots * ROW_TILE, LANES), F32),
        compiler_params=_cparams(("arbitrary",)),
        name="dispatch",
    )(pend, counts, dest3, h2t)


def _ffn_kernel(be_ref, nv_ref, x_ref, wg_ref, wu_ref, wd_ref, y_ref):
    del be_ref
    live = pl.program_id(0) < nv_ref[0]

    @pl.when(live)
    def _():
        x = _load_token_tiles(x_ref, FFN_ROWS).astype(BF16)
        gate = _dot(x, wg_ref[...])
        up = _dot(x, wu_ref[...])
        _store_token_tiles(y_ref, _dot((gate * _sigmoid(gate) * up).astype(BF16), wd_ref[...]))

    @pl.when(jnp.logical_not(live))
    def _():
        y_ref[...] = jnp.zeros_like(y_ref)


def _ffn(xs, block_e, n_live, wts):
    nb = xs.shape[0] // (FFN_ROWS * ROW_TILE)
    live_block = lambda i, be, nv: (jnp.minimum(i, nv[0] - 1), 0)
    grid_spec = pltpu.PrefetchScalarGridSpec(
        num_scalar_prefetch=2,
        grid=(nb,),
        in_specs=[pl.BlockSpec((FFN_ROWS * ROW_TILE, LANES), live_block),
                  pl.BlockSpec((None, D_MODEL, EXPERT_FF), lambda i, be, nv: (be[i], 0, 0)),
                  pl.BlockSpec((None, D_MODEL, EXPERT_FF), lambda i, be, nv: (be[i], 0, 0)),
                  pl.BlockSpec((None, EXPERT_FF, D_MODEL), lambda i, be, nv: (be[i], 0, 0))],
        out_specs=pl.BlockSpec((FFN_ROWS * ROW_TILE, LANES), lambda i, be, nv: (i, 0)),
    )
    return pl.pallas_call(
        _ffn_kernel,
        grid_spec=grid_spec,
        out_shape=jax.ShapeDtypeStruct(xs.shape, F32),
        compiler_params=_cparams(("arbitrary",)),
        name="ffn",
    )(block_e, n_live, xs, wts["w_eg"], wts["w_eu"], wts["w_ed"])


def _combine_kernel(dest_ref, dest_next_ref, x1_ref, rw_ref, yb_ref, o_ref, y_scr, sems, *, tm):
    i = pl.program_id(0)
    n = pl.num_programs(0)
    slot = i % 2
    tile = lambda ref, t, rows=1: ref.at[pl.ds(t * ROW_TILE, rows * ROW_TILE), :]

    def gather(dref, buf):
        def issue(t, carry):
            for s in range(TOP_K):
                d = dref[0, 0, s * tm + t]
                pltpu.make_async_copy(tile(yb_ref, d), tile(y_scr.at[buf, s], t), sems.at[buf]).start(priority=s)
            return carry

        lax.fori_loop(0, tm, issue, 0, unroll=DMA_ISSUE_UNROLL)

    pl.when(i == 0)(lambda: gather(dest_ref, 0))
    pl.when(i + 1 < n)(lambda: gather(dest_next_ref, 1 - slot))
    for s in range(TOP_K):
        pltpu.make_async_copy(tile(yb_ref, 0, tm), y_scr.at[slot, s], sems.at[slot]).wait()
    rw = rw_ref[...]
    y = [_load_token_tiles(y_scr.at[slot, s], tm) for s in range(TOP_K)]
    o_ref[...] = x1_ref[...] + (rw[:, 0:1] * y[0] + rw[:, 1:2] * y[1])


def _combine(x1, rw, yb, dest, tm):
    T = x1.shape[0]
    nt = T // tm
    dest3 = _dest_blocks(dest, tm)
    dest_spec = lambda step: pl.BlockSpec((1, 1, TOP_K * tm), lambda i: (step(i), 0, 0), memory_space=pltpu.SMEM)
    return pl.pallas_call(
        functools.partial(_combine_kernel, tm=tm),
        grid=(nt,),
        in_specs=[dest_spec(lambda i: i), dest_spec(lambda i: jnp.minimum(i + 1, nt - 1)),
                  pl.BlockSpec((tm, D_MODEL), lambda i: (i, 0)),
                  pl.BlockSpec((tm, LANES), lambda i: (i, 0)),
                  pl.BlockSpec(memory_space=pl.ANY)],
        out_specs=pl.BlockSpec((tm, D_MODEL), lambda i: (i, 0)),
        out_shape=jax.ShapeDtypeStruct((T, D_MODEL), F32),
        scratch_shapes=[pltpu.VMEM((2, TOP_K, tm * ROW_TILE, LANES), F32), pltpu.SemaphoreType.DMA((2,))],
        compiler_params=_cparams(("arbitrary",)),
        name="combine",
    )(dest3, dest3, x1, rw, yb)


def _moe(x1, h2, ri, rw, cnt, wts, tm):
    T = x1.shape[0]
    ids = ri[0:TOP_K]
    ranks = ri[TOP_K:2 * TOP_K]
    counts = cnt[0, :N_EXPERTS]
    padded = (counts + FFN_ROWS - 1) // FFN_ROWS * FFN_ROWS
    pend = jnp.cumsum(padded)
    pstart = pend - padded
    experts = jnp.arange(N_EXPERTS, dtype=I32)[:, None, None]
    dest = (jnp.sum(jnp.where(ids[None] == experts, pstart[:, None, None], 0), axis=0) + ranks).astype(I32)
    nb = -(-T * TOP_K // FFN_ROWS) + N_EXPERTS
    block_pos = jnp.arange(nb, dtype=I32) * FFN_ROWS
    block_e = jnp.minimum(jnp.sum(pend[None, :] <= block_pos[:, None], axis=1), N_EXPERTS - 1).astype(I32)
    n_live = (pend[-1:] // FFN_ROWS).astype(I32)
    xs = _dispatch(h2, dest, pend.astype(I32), counts.astype(I32), nb * FFN_ROWS, min(DISPATCH_TOKENS, T))
    yb = _ffn(xs, block_e, n_live, wts)
    return _combine(x1, rw, yb, dest, tm)


def _rope_tables(pos):
    inv = 1.0 / (ROPE_BASE ** (jnp.arange(HALF_ROPE, dtype=F32) * (2.0 / QK_ROPE)))
    ang = pos.astype(F32)[:, None] * inv[None, :]
    cos, sin = jnp.cos(ang), jnp.sin(ang)
    n = pos.shape[0]
    pad = jnp.zeros((n, HEAD_PAD - QK_DIM), F32)
    ct = jnp.concatenate([jnp.ones((n, QK_NOPE), F32), cos, cos, pad], axis=1)
    st = jnp.concatenate([jnp.zeros((n, QK_NOPE), F32), -sin, sin, pad], axis=1)
    kt = jnp.concatenate([cos, sin], axis=1)
    return ct, st, kt


def _score_bound(q_head_norm_g, k_head_norm_g):
    return (1.02 * QK_DIM * ATTN_SCALE * math.log2(math.e)) * (
        jnp.max(jnp.abs(q_head_norm_g)) * jnp.max(jnp.abs(k_head_norm_g)))


def _prep_weights(norm_mix_g, w_in, q_norm_g, w_uq, kv_norm_g, w_ukv, q_head_norm_g, k_head_norm_g,
                  rwkv_mu, rwkv_w0, rwkv_w2, rwkv_a0, rwkv_a2, rwkv_g2, rwkv_k_k, rwkv_k_a, rwkv_r_k,
                  rwkv_lnx_g, rwkv_lnx_b, w_out, norm_ffn_g, w_router_group, b_router_group,
                  w_router_expert, b_router_expert, w_expert_gate, w_expert_up, w_expert_down):
    row = lambda a: a.reshape(1, -1).astype(F32)
    head_pad = lambda a: jnp.pad(a, ((0, 0), (0, 0), (0, HEAD_PAD - a.shape[-1])))
    w = {}
    w["mix_g"] = row(norm_mix_g)
    w["w_in"] = jnp.concatenate([w_in[:, :MLA_COLS], jnp.zeros((D_MODEL, LANES - QK_ROPE), F32),
                                 w_in[:, MLA_COLS:]], axis=1).astype(BF16)
    w["q_g"] = row(q_norm_g)
    w["kv_g"] = row(kv_norm_g)
    w["wqa"] = head_pad(w_uq).reshape(Q_LORA, MLA_HEADS * HEAD_PAD).astype(BF16)
    partner = jnp.concatenate([jnp.zeros_like(w_uq[..., :QK_NOPE]), w_uq[..., QK_NOPE + HALF_ROPE:],
                               w_uq[..., QK_NOPE:QK_NOPE + HALF_ROPE]], axis=-1)
    w["wqb"] = head_pad(partner).reshape(Q_LORA, MLA_HEADS * HEAD_PAD).astype(BF16)
    pad_g = lambda g: jnp.pad(g, (0, HEAD_PAD - QK_DIM)).reshape(1, HEAD_PAD).astype(F32)
    w["qh_g"] = pad_g(q_head_norm_g) * (ATTN_SCALE * math.log2(math.e))
    w["kh_g"] = pad_g(k_head_norm_g)
    lane_one = lambda lane: (jnp.arange(HEAD_PAD) == lane).astype(F32).reshape(1, HEAD_PAD)
    w["score_bound"] = _score_bound(q_head_norm_g, k_head_norm_g)
    w["q_one"] = lane_one(QK_DIM)
    w["k_shift"] = -w["score_bound"] * lane_one(QK_DIM)
    w["v_one"] = jnp.tile(lane_one(V_DIM), (1, MLA_HEADS))
    w["wk"] = head_pad(w_ukv[..., :QK_NOPE]).reshape(KV_LORA, MLA_HEADS * HEAD_PAD).astype(BF16)
    w["wv"] = head_pad(w_ukv[..., QK_NOPE:]).reshape(KV_LORA, MLA_HEADS * HEAD_PAD).astype(BF16)
    w["place"] = jnp.pad(jnp.eye(QK_ROPE, dtype=F32), ((0, 0), (QK_NOPE, HEAD_PAD - QK_DIM))).astype(BF16)
    w["mu"] = row(rwkv_mu)
    w["w0"] = row(rwkv_w0)
    w["w2"] = jnp.concatenate([rwkv_w2, jnp.zeros((A_LORA, RWKV_DIM), F32)], axis=0).astype(BF16)
    w["a0"] = row(rwkv_a0)
    w["a2"] = jnp.concatenate([jnp.zeros((W_LORA, RWKV_DIM), F32), rwkv_a2], axis=0).astype(BF16)
    w["g2"] = rwkv_g2.astype(BF16)
    w["k_k"] = row(rwkv_k_k)
    w["k_a"] = row(rwkv_k_a)
    w["r_k"] = row(rwkv_r_k)
    w["lnx_g"] = row(rwkv_lnx_g)
    w["lnx_b"] = row(rwkv_lnx_b)
    head_of = jnp.arange(RWKV_DIM) // RWKV_HEAD
    w["ones_bd"] = (head_of[:, None] == head_of[None, :]).astype(BF16)
    w["w_out_a"] = jnp.pad(w_out[0].reshape(MLA_HEADS, V_DIM, D_MODEL), ((0, 0), (0, HEAD_PAD - V_DIM), (0, 0))
                           ).reshape(MLA_HEADS * HEAD_PAD, D_MODEL).astype(BF16)
    w["w_out_b"] = w_out[1].astype(BF16)
    w["ffn_g"] = row(norm_ffn_g)
    wr = jnp.concatenate([w_router_expert.reshape(D_MODEL, N_EXPERTS), w_router_group,
                          jnp.zeros((D_MODEL, LANES - N_EXPERTS - N_GROUPS), F32)], axis=1)
    w["wr_hi"] = wr.astype(BF16)
    w["wr_lo"] = (wr - w["wr_hi"].astype(F32)).astype(BF16)
    w["b_r"] = jnp.concatenate([b_router_expert.reshape(-1), b_router_group,
                                jnp.zeros((LANES - N_EXPERTS - N_GROUPS,), F32)]).reshape(1, LANES)
    w["w_eg"] = w_expert_gate.astype(BF16)
    w["w_eu"] = w_expert_up.astype(BF16)
    w["w_ed"] = w_expert_down.astype(BF16)
    return w


def _round_up(n, m):
    return -(-n // m) * m


def _layer(x, pos, lat_past, kpe_past, wkv0, shift0, wts):
    B, L, _ = x.shape
    T = B * L
    tm = min(256, T)
    ct, st, kt = _rope_tables(pos)
    if L % tm:
        reps = tm // L
        ct, st, kt = (jnp.tile(t, (reps, 1)) for t in (ct, st, kt))
    q, lat, kpe, z, gl = _inproj(x.reshape(T, D_MODEL), (ct, st, kt), wts, tm)
    lat3 = lat.reshape(B, L, KV_LORA)
    kpe3 = kpe.reshape(B, L, QK_ROPE)
    if lat_past is None:
        n_past, lk, lat_all, kpe_all = 0, L, lat3, kpe3
        tq, tk = min(ATTN_QUERY_TILE, L), min(ATTN_KEY_TILE, L)
        diag = True
    else:
        n_past = lat_past.shape[1]
        lk_valid = n_past + L
        lk = _round_up(lk_valid, LANES)
        padk = lambda a: jnp.pad(a, ((0, 0), (0, lk - lk_valid), (0, 0)))
        lat_all = padk(jnp.concatenate([lat_past, lat3], axis=1))
        kpe_all = padk(jnp.concatenate([kpe_past, kpe3], axis=1))
        tq, tk, diag = L, lk, False
    k, v = _kvprep(lat_all.reshape(B * lk, KV_LORA), kpe_all.reshape(B * lk, QK_ROPE), wts, min(512, B * lk))
    oa = _attention(q.reshape(B, L, -1), k.reshape(B, lk, -1), v.reshape(B, lk, -1),
                    wts["score_bound"] <= MAX_DIRECT_EXP2_SHIFT, tq=tq, tk=tk, n_past=n_past,
                    lk_valid=n_past + L, diag_tiles=diag)
    z3 = z.reshape(B, L, RWKV_COLS)
    ob, s_new = _rwkv(z3, shift0, wkv0, wts, min(SCAN_CHUNK, L))
    x1, h2, ri, rw, cnt = _merge(x.reshape(T, D_MODEL), oa.reshape(T, -1), ob.reshape(T, -1), gl, wts, tm)
    y = _moe(x1, h2, ri, rw, cnt, wts, tm)
    return y.reshape(B, L, D_MODEL), lat3, kpe3, s_new, z3[:, L - 1:, :]


def kernel(x_prompt, x_sample, cache_kv_latent, cache_k_rope, state_wkv, state_shift, norm_mix_g, w_in, q_norm_g, w_uq, kv_norm_g, w_ukv, q_head_norm_g, k_head_norm_g, rwkv_mu, rwkv_w0, rwkv_w2, rwkv_a0, rwkv_a2, rwkv_g2, rwkv_k_k, rwkv_k_a, rwkv_r_k, rwkv_lnx_g, rwkv_lnx_b, w_out, norm_ffn_g, w_router_group, b_router_group, w_router_expert, b_router_expert, w_expert_gate, w_expert_up, w_expert_down):
    weights = (norm_mix_g, w_in, q_norm_g, w_uq, kv_norm_g, w_ukv, q_head_norm_g, k_head_norm_g,
               rwkv_mu, rwkv_w0, rwkv_w2, rwkv_a0, rwkv_a2, rwkv_g2, rwkv_k_k, rwkv_k_a, rwkv_r_k,
               rwkv_lnx_g, rwkv_lnx_b, w_out, norm_ffn_g, w_router_group, b_router_group,
               w_router_expert, b_router_expert, w_expert_gate, w_expert_up, w_expert_down)
    depth = w_in.shape[0]
    bp, lp, _ = x_prompt.shape
    n_past = cache_kv_latent.shape[2]
    pos_p = jnp.arange(lp, dtype=I32)
    pos_s = n_past + jnp.arange(x_sample.shape[1], dtype=I32)
    wkv_zero = jnp.zeros((bp, RWKV_HEADS, RWKV_HEAD, RWKV_HEAD), F32)
    shift_zero = jnp.zeros((bp, 1, RWKV_COLS), F32)
    xp, xs = x_prompt, x_sample
    outs_p, outs_s = [], []
    for l in range(depth):
        wts = _prep_weights(*(wt[l] for wt in weights))
        xp, *rest_p = _layer(xp, pos_p, None, None, wkv_zero, shift_zero, wts)
        xs, *rest_s = _layer(xs, pos_s, cache_kv_latent[l], cache_k_rope[l], state_wkv[l], state_shift[l], wts)
        outs_p.append(rest_p)
        outs_s.append(rest_s)
    stack = lambda outs, i: jnp.stack([o[i] for o in outs], 0)
    return (xp, xs, stack(outs_p, 0), stack(outs_p, 1), stack(outs_p, 2), stack(outs_p, 3),
            stack(outs_s, 0), stack(outs_s, 1), stack(outs_s, 2), stack(outs_s, 3))
```
